```python
import math
import jax, jax.numpy as jnp
from jax import lax
import numpy as np

D_MODEL = 2048
BATCH = 1
SEQ = 8192
DEPTH = 1

CHUNK = 64
EPS = 1e-6
A_HEADS = 8
A_HEAD_DIM = 128
A_WIDTH = A_HEADS * A_HEAD_DIM
B_HEADS = 8
B_HEAD_DIM = 128
B_WIDTH = B_HEADS * B_HEAD_DIM
Q_RANK = 512
KV_RANK = 256
IDX_HEADS = 16
IDX_DIM = 128
TOPK_MAX = 256
Q_BLOCK = 128
REL_BUCKETS = 32
REL_MAX_DIST = 128

SPLIT_SIZES = (A_WIDTH, A_WIDTH, A_WIDTH, A_WIDTH,
               Q_RANK, KV_RANK, IDX_DIM, IDX_HEADS, B_WIDTH,
               D_MODEL, D_MODEL)
SPLIT_POINTS = tuple(sum(SPLIT_SIZES[:i + 1]) for i in range(len(SPLIT_SIZES) - 1))
IN_WIDTH = sum(SPLIT_SIZES)

kernel_name = 'hgrn2_dsa_gated_hybrid_block'

F32 = jnp.float32


def rmsnorm(x, w):
    x32 = x.astype(F32)
    y = x32 * lax.rsqrt(jnp.mean(x32 * x32, axis=-1, keepdims=True) + EPS) * w.astype(F32)
    return y.astype(x.dtype)


def layernorm(x, w, b):
    x32 = x.astype(F32)
    mu = jnp.mean(x32, axis=-1, keepdims=True)
    var = jnp.mean(jnp.square(x32 - mu), axis=-1, keepdims=True)
    y = (x32 - mu) * lax.rsqrt(var + EPS) * w.astype(F32) + b.astype(F32)
    return y.astype(x.dtype)


def t5_bucket(rel):
    half = REL_BUCKETS // 2
    max_exact = half // 2
    base = jnp.where(rel > 0, half, 0)
    n = jnp.abs(rel)
    large = max_exact + (jnp.log(jnp.maximum(n, 1).astype(F32) / max_exact)
                         / math.log(REL_MAX_DIST / max_exact) * (half - max_exact)).astype(jnp.int32)
    large = jnp.minimum(large, half - 1)
    return base + jnp.where(n < max_exact, n, large)


def hgrn2_recurrence(q, k, log_f, v):
    b_, s_, h_, dk = q.shape
    dv = v.shape[-1]
    nc = s_ // CHUNK

    def to_chunks(t):
        return jnp.moveaxis(t.reshape(b_, nc, CHUNK, h_, t.shape[-1]), 1, 0)

    causal = jnp.tril(jnp.ones((CHUNK, CHUNK), bool))

    def step(state, inp):
        q_c, k_c, g_c, v_c = inp
        cum = jnp.cumsum(g_c, axis=1)
        inter = jnp.einsum('bthk,bhkv->bthv', q_c * jnp.exp(cum), state)
        diff = cum[:, :, None] - cum[:, None, :]
        decay = jnp.exp(jnp.where(causal[None, :, :, None, None], diff, -jnp.inf))
        scores = jnp.einsum('bthk,bshk,btshk->bhts', q_c, k_c, decay)
        intra = jnp.einsum('bhts,bshv->bthv', scores, v_c)
        last = cum[:, -1]
        new_state = (jnp.exp(last)[..., None] * state
                     + jnp.einsum('bshk,bshv->bhkv', k_c * jnp.exp(last[:, None] - cum), v_c))
        return new_state, inter + intra

    init = jnp.zeros((b_, h_, dk, dv), F32)
    _, out = lax.scan(step, init, (to_chunks(q), to_chunks(k), to_chunks(log_f), to_chunks(v)))
    return jnp.moveaxis(out, 0, 1).reshape(b_, s_, h_, dv)


def hgrn2_branch(a_q, a_f, a_i, a_g, lb, gnorm_w):
    b_, s_, _ = a_q.shape
    shp = (b_, s_, A_HEADS, A_HEAD_DIM)
    q = (jax.nn.silu(a_q.astype(F32)) * A_HEAD_DIM ** -0.5).reshape(shp)
    f = lb + (1.0 - lb) * jax.nn.sigmoid(a_f.astype(F32))
    k = (1.0 - f).reshape(shp)
    log_f = jnp.log(f).reshape(shp)
    v = a_i.astype(F32).reshape(shp)
    o = hgrn2_recurrence(q, k, log_f, v)
    o = rmsnorm(o, gnorm_w).reshape(b_, s_, A_WIDTH)
    return (o * jax.nn.silu(a_g.astype(F32))).astype(a_q.dtype)


def dsa_branch(c_q, c_kv, k_idx_raw, w_idx_raw, b_g, q_norm_w, kv_norm_w, w_uq, w_qidx,
               w_ukv, kidx_norm_w, kidx_norm_b, rel_bias, topk):
    b_, s_, _ = c_q.shape
    nb = s_ // Q_BLOCK
    cq = rmsnorm(c_q, q_norm_w)
    q = (cq @ w_uq).reshape(b_, s_, B_HEADS, B_HEAD_DIM)
    q_idx = (cq @ w_qidx).reshape(b_, s_, IDX_HEADS, IDX_DIM)
    kv = (rmsnorm(c_kv, kv_norm_w) @ w_ukv).reshape(b_, s_, B_HEADS, 2 * B_HEAD_DIM)
    k, v = kv[..., :B_HEAD_DIM], kv[..., B_HEAD_DIM:]
    k_idx = layernorm(k_idx_raw, kidx_norm_w, kidx_norm_b).astype(F32)
    w_idx = w_idx_raw.astype(F32) * (IDX_HEADS ** -0.5 * IDX_DIM ** -0.5)
    key_chunk = jnp.arange(s_) // CHUNK
    bias_table = rel_bias.astype(F32)

    def to_blocks(t):
        return jnp.moveaxis(t.reshape((b_, nb, Q_BLOCK) + t.shape[2:]), 1, 0)

    def attend_block(inp):
        q_blk, qi_blk, wi_blk, t0 = inp
        t_pos = t0 + jnp.arange(Q_BLOCK)
        rel_scores = jax.nn.relu(jnp.einsum('bthd,bsd->bths', qi_blk.astype(F32), k_idx))
        score = jnp.einsum('bths,bth->bts', rel_scores, wi_blk)
        visible = key_chunk[None, :] <= (t_pos // CHUNK)[:, None]
        score = jnp.where(visible[None], score, -jnp.inf)
        top_val, top_idx = lax.top_k(score, topk)
        valid = jnp.isfinite(top_val)
        k_sel = jax.vmap(lambda kk, ii: kk[ii])(k, top_idx)
        v_sel = jax.vmap(lambda vv, ii: vv[ii])(v, top_idx)
        logits = jnp.einsum('bthd,btkhd->bthk', q_blk, k_sel).astype(F32) * B_HEAD_DIM ** -0.5
        bias = jnp.moveaxis(bias_table[t5_bucket(top_idx - t_pos[None, :, None])], -1, 2)
        logits = jnp.where(valid[:, :, None, :], logits + bias, -jnp.inf)
        probs = jax.nn.softmax(logits, axis=-1).astype(v.dtype)
        return jnp.einsum('bthk,btkhd->bthd', probs, v_sel)

    out = lax.map(attend_block, (to_blocks(q), to_blocks(q_idx), to_blocks(w_idx),
                                 jnp.arange(nb, dtype=jnp.int32) * Q_BLOCK))
    out = jnp.moveaxis(out, 0, 1).reshape(b_, s_, B_WIDTH)
    return out * jax.nn.silu(b_g)


def setup_inputs(seed: int = 0) -> dict:
    key = jax.random.key(seed)
    ks = jax.random.split(key, 20)
    nrm = lambda k, shape, scale: jax.random.normal(k, shape, F32) * scale
    return {
        'x': nrm(ks[0], (BATCH, SEQ, D_MODEL), 1.0),
        'norm_w': 1.0 + nrm(ks[1], (DEPTH, D_MODEL), 0.02),
        'w_in': nrm(ks[2], (DEPTH, D_MODEL, IN_WIDTH), D_MODEL ** -0.5),
        'lb_table': nrm(ks[3], (DEPTH + 1, A_WIDTH), 0.1),
        'gnorm_a': 1.0 + nrm(ks[4], (DEPTH, A_HEAD_DIM), 0.02),
        'q_norm_w': 1.0 + nrm(ks[5], (DEPTH, Q_RANK), 0.02),
        'kv_norm_w': 1.0 + nrm(ks[6], (DEPTH, KV_RANK), 0.02),
        'w_uq': nrm(ks[7], (DEPTH, Q_RANK, B_WIDTH), Q_RANK ** -0.5),
        'w_qidx': nrm(ks[8], (DEPTH, Q_RANK, IDX_HEADS * IDX_DIM), Q_RANK ** -0.5),
        'w_ukv': nrm(ks[9], (DEPTH, KV_RANK, 2 * B_WIDTH), KV_RANK ** -0.5),
        'kidx_norm_w': 1.0 + nrm(ks[10], (DEPTH, IDX_DIM), 0.02),
        'kidx_norm_b': nrm(ks[11], (DEPTH, IDX_DIM), 0.02),
        'w_pa': nrm(ks[12], (DEPTH, A_WIDTH, D_MODEL), A_WIDTH ** -0.5),
        'w_pb': nrm(ks[13], (DEPTH, B_WIDTH, D_MODEL), B_WIDTH ** -0.5),
        'w_out': nrm(ks[14], (DEPTH, D_MODEL, D_MODEL), D_MODEL ** -0.5),
        'rel_bias': nrm(ks[15], (REL_BUCKETS, B_HEADS), 0.5),
        'final_norm_w': 1.0 + nrm(ks[16], (D_MODEL,), 0.02),
    }


def reference(x, norm_w, w_in, lb_table, gnorm_a, q_norm_w, kv_norm_w, w_uq, w_qidx, w_ukv,
              kidx_norm_w, kidx_norm_b, w_pa, w_pb, w_out, rel_bias, final_norm_w):
    s_ = x.shape[1]
    topk = min(TOPK_MAX, s_ // 4)
    lb_all = jnp.cumsum(jax.nn.softmax(lb_table.astype(F32), axis=0), axis=0)
    for layer in range(DEPTH):
        h = rmsnorm(x, norm_w[layer])
        proj = h @ w_in[layer]
        (a_q, a_f, a_i, a_g, c_q, c_kv, k_idx_raw, w_idx_raw, b_g, m_a, m_b) = jnp.split(
            proj, SPLIT_POINTS, axis=-1)
        y_a = hgrn2_branch(a_q, a_f, a_i, a_g, lb_all[layer], gnorm_a[layer])
        y_b = dsa_branch(c_q, c_kv, k_idx_raw, w_idx_raw, b_g, q_norm_w[layer], kv_norm_w[layer],
                         w_uq[layer], w_qidx[layer], w_ukv[layer], kidx_norm_w[layer],
                         kidx_norm_b[layer], rel_bias, topk)
        merged = (jax.nn.sigmoid(m_a) * (y_a @ w_pa[layer])
                  + jax.nn.sigmoid(m_b) * (y_b @ w_pb[layer]))
        x = x + merged @ w_out[layer]
    return rmsnorm(x, final_norm_w)
```

```python
import functools
import math

import jax
import jax.numpy as jnp
import numpy as np
from jax import lax
from jax.experimental import pallas as pl
from jax.experimental.pallas import tpu as pltpu

F32 = jnp.float32
BF16 = jnp.bfloat16

D_MODEL = 2048
SEQ = 8192
EPS = 1e-6
A_HEADS = 8
A_HEAD_DIM = 128
A_WIDTH = 1024
B_HEADS = 8
B_HEAD_DIM = 128
B_WIDTH = 1024
Q_RANK = 512
KV_RANK = 256
IDX_HEADS = 16
IDX_DIM = 128
TOPK = 256
CHUNK = 64
REL_BUCKETS = 32
REL_MAX_DIST = 128

P_AQ, P_AF, P_AI, P_AG = 0, 1024, 2048, 3072
P_CQ = 4096
P_MISC = 4608
P_BG = 5120
P_MA = 6144
P_MB = 8192
P_WIDTH = 10240

VMEM_LIMIT = 60 * 1024 * 1024

QB = 256
KT = 256
NQB = SEQ // QB
HALF = 128

SUB = 16
TB = 256

NEG_INF = float("-inf")
INT_MIN = -2147483648
KEY_NEG_FLT_MAX = -2139095040


def _t5_thresholds():
    half = REL_BUCKETS // 2
    max_exact = half // 2
    n = np.arange(1, 4 * REL_MAX_DIST, dtype=np.int64)
    large = max_exact + (np.log(np.maximum(n, 1).astype(np.float64) / max_exact)
                         / math.log(REL_MAX_DIST / max_exact) * (half - max_exact)).astype(np.int32)
    large = np.minimum(large, half - 1)
    b = np.where(n < max_exact, n, large)
    thr = [int(n[i + 1]) for i in np.nonzero(np.diff(b))[0] if n[i + 1] > max_exact]
    assert len(thr) == half - 1 - max_exact and b[-1] == half - 1 and thr[-1] < REL_MAX_DIST
    return max_exact, half, thr


T5_MAX_EXACT, T5_HALF, T5_THR = _t5_thresholds()


def _in_proj_kernel(x_ref, nw_ref, w_ref, o_ref, h_ref):
    @pl.when(pl.program_id(1) == 0)
    def _():
        x = x_ref[...]
        ms = jnp.mean(x * x, axis=-1, keepdims=True)
        h_ref[...] = (x * lax.rsqrt(ms + EPS) * nw_ref[...]).astype(BF16)

    o_ref[...] = jnp.dot(h_ref[...], w_ref[...], preferred_element_type=F32)


def _in_proj(x2, norm_w, w_in_p):
    tm, tn = 1024, 1024
    return pl.pallas_call(
        _in_proj_kernel,
        grid=(SEQ // tm, P_WIDTH // tn),
        in_specs=[
            pl.BlockSpec((tm, D_MODEL), lambda i, j: (i, 0)),
            pl.BlockSpec((1, D_MODEL), lambda i, j: (0, 0)),
            pl.BlockSpec((D_MODEL, tn), lambda i, j: (0, j)),
        ],
        out_specs=pl.BlockSpec((tm, tn), lambda i, j: (i, j)),
        out_shape=jax.ShapeDtypeStruct((SEQ, P_WIDTH), F32),
        scratch_shapes=[pltpu.VMEM((tm, D_MODEL), BF16)],
        compiler_params=pltpu.CompilerParams(
            dimension_semantics=("arbitrary", "arbitrary"), vmem_limit_bytes=VMEM_LIMIT),
        name="in_proj",
    )(x2, norm_w, w_in_p)


def _rms(x, w):
    return x * lax.rsqrt(jnp.mean(x * x, axis=-1, keepdims=True) + EPS) * w


def _proj2_kernel(cq_ref, misc_ref, qnw_ref, kvnw_ref, wuq_ref, wqi_ref, wukv_ref, lnw_ref, lnb_ref,
                  q_ref, qi_ref, kv_ref, kidx_ref, w_ref):
    cqn = _rms(cq_ref[...], qnw_ref[...]).astype(BF16)
    q_ref[...] = jnp.dot(cqn, wuq_ref[...], preferred_element_type=F32).astype(BF16)
    qi = jnp.dot(cqn, wqi_ref[...], preferred_element_type=F32)
    for h in range(IDX_HEADS):
        qi_ref[h] = qi[:, h * IDX_DIM:(h + 1) * IDX_DIM].astype(BF16)
    misc = misc_ref[...]
    ckvn = _rms(misc[:, :KV_RANK], kvnw_ref[...]).astype(BF16)
    kv_ref[...] = jnp.dot(ckvn, wukv_ref[...], preferred_element_type=F32).astype(BF16)
    kr = misc[:, KV_RANK:KV_RANK + IDX_DIM]
    mu = jnp.mean(kr, axis=-1, keepdims=True)
    var = jnp.mean(jnp.square(kr - mu), axis=-1, keepdims=True)
    kidx_ref[...] = ((kr - mu) * lax.rsqrt(var + EPS) * lnw_ref[...] + lnb_ref[...]).astype(BF16)
    w_ref[...] = misc[:, KV_RANK + IDX_DIM:] * (IDX_HEADS ** -0.5 * IDX_DIM ** -0.5)


def _proj2(p, q_norm_w, kv_norm_w, w_uq, w_qidx, w_ukv, ln_w, ln_b):
    tm = 512
    const = lambda i: (0, 0)
    return pl.pallas_call(
        _proj2_kernel,
        grid=(SEQ // tm,),
        in_specs=[
            pl.BlockSpec((tm, 512), lambda i: (i, P_CQ // 512)),
            pl.BlockSpec((tm, 512), lambda i: (i, P_MISC // 512)),
            pl.BlockSpec((1, Q_RANK), const),
            pl.BlockSpec((1, KV_RANK), const),
            pl.BlockSpec((Q_RANK, B_WIDTH), const),
            pl.BlockSpec((Q_RANK, IDX_HEADS * IDX_DIM), const),
            pl.BlockSpec((KV_RANK, 2 * B_WIDTH), const),
            pl.BlockSpec((1, IDX_DIM), const),
            pl.BlockSpec((1, IDX_DIM), const),
        ],
        out_specs=[
            pl.BlockSpec((tm, B_WIDTH), lambda i: (i, 0)),
            pl.BlockSpec((IDX_HEADS, tm, IDX_DIM), lambda i: (0, i, 0)),
            pl.BlockSpec((tm, 2 * B_WIDTH), lambda i: (i, 0)),
            pl.BlockSpec((tm, IDX_DIM), lambda i: (i, 0)),
            pl.BlockSpec((tm, 128), lambda i: (i, 0)),
        ],
        out_shape=[
            jax.ShapeDtypeStruct((SEQ, B_WIDTH), BF16),
            jax.ShapeDtypeStruct((IDX_HEADS, SEQ, IDX_DIM), BF16),
            jax.ShapeDtypeStruct((SEQ, 2 * B_WIDTH), BF16),
            jax.ShapeDtypeStruct((SEQ, IDX_DIM), BF16),
            jax.ShapeDtypeStruct((SEQ, 128), F32),
        ],
        compiler_params=pltpu.CompilerParams(
            dimension_semantics=("arbitrary",), vmem_limit_bytes=VMEM_LIMIT),
        name="proj2",
    )(p, p, q_norm_w, kv_norm_w, w_uq, w_qidx, w_ukv, ln_w, ln_b)


def _split3(x):
    x1 = x.astype(BF16)
    r1 = x - x1.astype(F32)
    x2 = r1.astype(BF16)
    x3 = (r1 - x2.astype(F32)).astype(BF16)
    return x1, x2, x3


def _hgrn2_kernel(aq_ref, af_ref, ai_ref, ag_ref, lbt_ref, gn_ref, o_ref,
                  st_ref, sel_ref, a_ref, b_ref, kd_ref, v_ref, e_ref, oacc_ref):
    @pl.when(pl.program_id(0) == 0)
    def _():
        st_ref[...] = jnp.zeros_like(st_ref)
        r = lax.broadcasted_iota(jnp.int32, (TB, TB), 0)
        c = lax.broadcasted_iota(jnp.int32, (TB, TB), 1)
        same = (r // SUB) == (c // SUB)
        sel_ref[0:TB, :] = jnp.where(same & (c <= r), 1.0, 0.0).astype(BF16)
        sel_ref[TB:2 * TB, :] = jnp.where(same, 1.0, 0.0).astype(BF16)

    lbt = lbt_ref[...]
    mx = jnp.maximum(lbt[0:1], lbt[1:2])
    e0 = jnp.exp(lbt[0:1] - mx)
    e1 = jnp.exp(lbt[1:2] - mx)
    lb = e0 / (e0 + e1)

    f = lb + (1.0 - lb) * jax.nn.sigmoid(af_ref[...])
    g = jnp.log(f)
    sel = sel_ref[...]
    g1, g2, g3 = _split3(g)
    ct = (jnp.dot(sel, g1, preferred_element_type=F32) + jnp.dot(sel, g2, preferred_element_type=F32)
          + jnp.dot(sel, g3, preferred_element_type=F32))
    cum = ct[0:TB]
    tot = ct[TB:2 * TB]
    aq = aq_ref[...]
    q = aq * jax.nn.sigmoid(aq) * (A_HEAD_DIM ** -0.5)
    kk = 1.0 - f
    a_ref[...] = (q * jnp.exp(cum)).astype(BF16)
    b_ref[...] = (kk * jnp.exp(-cum)).astype(BF16)
    kd_ref[...] = (kk * jnp.exp(tot - cum)).astype(BF16)
    v_ref[...] = ai_ref[...].astype(BF16)
    e_ref[...] = jnp.exp(tot)

    tr = lax.broadcasted_iota(jnp.int32, (SUB, SUB), 0)
    tc = lax.broadcasted_iota(jnp.int32, (SUB, SUB), 1)
    causal = tc <= tr
    nt = (((1,), (1,)), ((), ()))
    tn = (((0,), (0,)), ((), ()))

    def chunk_body(c, carry):
        r0 = pl.multiple_of(c * SUB, SUB)
        for h in range(A_HEADS):
            cs = slice(h * A_HEAD_DIM, (h + 1) * A_HEAD_DIM)
            ah = a_ref[pl.ds(r0, SUB), cs]
            bh = b_ref[pl.ds(r0, SUB), cs]
            kdh = kd_ref[pl.ds(r0, SUB), cs]
            vh = v_ref[pl.ds(r0, SUB), cs]
            st = st_ref[h]
            inter = lax.dot_general(ah, st.astype(BF16), nt, preferred_element_type=F32)
            sc = lax.dot_general(ah, bh, nt, preferred_element_type=F32)
            sc = jnp.where(causal, sc, 0.0)
            intra = jnp.dot(sc.astype(BF16), vh, preferred_element_type=F32)
            oacc_ref[pl.ds(r0, SUB), cs] = inter + intra
            dst = lax.dot_general(vh, kdh, tn, preferred_element_type=F32)
            st_ref[h] = st * e_ref[pl.ds(r0, 1), cs] + dst
        return carry

    lax.fori_loop(0, TB // SUB, chunk_body, 0)

    gn = gn_ref[...]
    ag = ag_ref[...]
    gate = ag * jax.nn.sigmoid(ag)
    for h in range(A_HEADS):
        cs = slice(h * A_HEAD_DIM, (h + 1) * A_HEAD_DIM)
        o = oacc_ref[:, cs]
        on = o * lax.rsqrt(jnp.mean(o * o, axis=-1, keepdims=True) + EPS) * gn
        o_ref[:, cs] = (on * gate[:, cs]).astype(BF16)


def _hgrn2(p, lb_table, gnorm):
    blk = lambda col: pl.BlockSpec((TB, A_WIDTH), lambda i: (i, col // A_WIDTH))
    return pl.pallas_call(
        _hgrn2_kernel,
        grid=(SEQ // TB,),
        in_specs=[blk(P_AQ), blk(P_AF), blk(P_AI), blk(P_AG),
                  pl.BlockSpec((2, A_WIDTH), lambda i: (0, 0)),
                  pl.BlockSpec((1, A_HEAD_DIM), lambda i: (0, 0))],
        out_specs=pl.BlockSpec((TB, A_WIDTH), lambda i: (i, 0)),
        out_shape=jax.ShapeDtypeStruct((SEQ, A_WIDTH), BF16),
        scratch_shapes=[
            pltpu.VMEM((A_HEADS, A_HEAD_DIM, A_HEAD_DIM), F32),
            pltpu.VMEM((2 * TB, TB), BF16),
            pltpu.VMEM((TB, A_WIDTH), BF16),
            pltpu.VMEM((TB, A_WIDTH), BF16),
            pltpu.VMEM((TB, A_WIDTH), BF16),
            pltpu.VMEM((TB, A_WIDTH), BF16),
            pltpu.VMEM((TB, A_WIDTH), F32),
            pltpu.VMEM((TB, A_WIDTH), F32),
        ],
        compiler_params=pltpu.CompilerParams(
            dimension_semantics=("arbitrary",), vmem_limit_bytes=VMEM_LIMIT),
        name="hgrn2",
    )(p, p, p, p, lb_table, gnorm)


def _key_to_float(k):
    return lax.bitcast_convert_type(jnp.where(k >= 0, k, k ^ 0x7FFFFFFF), F32)


def _t5_bucket(rel):
    n = jnp.abs(rel)
    large = jnp.full(rel.shape, T5_MAX_EXACT, jnp.int32)
    for t in T5_THR:
        large = large + jnp.where(n >= t, 1, 0)
    return jnp.where(rel > 0, T5_HALF, 0) + jnp.where(n < T5_MAX_EXACT, n, large)


def _dsa_kernel(relb_ref, qi_ref, w_ref, q_ref, bg_ref, kidx_ref, kv_ref, o_ref,
                sc_ref, wb_ref, b0_ref, b1_ref, thr_ref, m_ref, l_ref, acc_ref):
    qb = pl.program_id(0)
    nt = (((1,), (1,)), ((), ()))

    @pl.when(qb == 0)
    def _():
        def bias_tile(rel, h):
            bucket = _t5_bucket(rel)
            val = jnp.zeros(rel.shape, F32)
            for b in range(REL_BUCKETS):
                val = jnp.where(bucket == b, relb_ref[b, h], val)
            return val - relb_ref[T5_HALF - 1, h]

        r0 = lax.broadcasted_iota(jnp.int32, (QB, KT), 0)
        c0 = lax.broadcasted_iota(jnp.int32, (QB, KT), 1)
        r1 = lax.broadcasted_iota(jnp.int32, (HALF, HALF), 0)
        c1 = lax.broadcasted_iota(jnp.int32, (HALF, HALF), 1)
        for h in range(B_HEADS):
            b0_ref[h] = bias_tile(c0 - r0, h)
            b1_ref[h] = bias_tile(c1 - HALF - r1, h)

    w = w_ref[...]
    for h in range(IDX_HEADS):
        wb_ref[h] = jnp.broadcast_to(w[:, h:h + 1], (QB, HALF))

    def idx_tile(j, diag):
        kt = kidx_ref[pl.ds(pl.multiple_of(j * KT, KT), KT), :]
        for half in range(QB // HALF):
            rows = slice(half * HALF, (half + 1) * HALF)
            acc0 = jnp.zeros((HALF, HALF), F32)
            acc1 = jnp.zeros((HALF, HALF), F32)
            for h in range(IDX_HEADS):
                r = lax.dot_general(qi_ref[h, rows, :], kt, nt, preferred_element_type=F32)
                r = jnp.maximum(r, 0.0)
                wbh = wb_ref[h, rows, :]
                acc0 = acc0 + wbh * r[:, :HALF]
                acc1 = acc1 + wbh * r[:, HALF:]
            if diag:
                tq = half * HALF + lax.broadcasted_iota(jnp.int32, (HALF, HALF), 0)
                ck = lax.broadcasted_iota(jnp.int32, (HALF, HALF), 1)
                acc0 = jnp.where((ck // CHUNK) <= (tq // CHUNK), acc0, NEG_INF)
                acc1 = jnp.where(((ck + HALF) // CHUNK) <= (tq // CHUNK), acc1, NEG_INF)
            sc_ref[j, rows, 0:HALF] = acc0
            sc_ref[j, rows, HALF:KT] = acc1

    def idx_body(j, carry):
        idx_tile(j, False)
        return carry

    lax.fori_loop(0, qb, idx_body, 0)
    idx_tile(qb, True)

    ntiles = qb + 1
    for half in range(QB // HALF):
        rows = slice(half * HALF, (half + 1) * HALF)

        def count_ge(thr_b):
            def body(j, cnt):
                t = sc_ref[j, rows, :]
                return (cnt + jnp.where(t[:, :HALF] >= thr_b, 1.0, 0.0)
                        + jnp.where(t[:, HALF:] >= thr_b, 1.0, 0.0))
            cnt = lax.fori_loop(0, ntiles, body, jnp.zeros((HALF, HALF), F32))
            return jnp.sum(cnt, axis=1, keepdims=True)

        c_pos = count_ge(jnp.zeros((HALF, HALF), F32))
        key0 = jnp.where(c_pos >= float(TOPK), jnp.zeros((HALF, HALF), jnp.int32),
                         jnp.full((HALF, HALF), INT_MIN, jnp.int32))

        def bit_body(i, key):
            cand = key | jnp.left_shift(jnp.int32(1), 30 - i)
            c = count_ge(_key_to_float(cand))
            return jnp.where(c >= float(TOPK), cand, key)

        key = lax.fori_loop(0, 31, bit_body, key0)
        thr_ref[rows, :] = jnp.where(key < KEY_NEG_FLT_MAX, NEG_INF, _key_to_float(key))

    m_ref[...] = jnp.full(m_ref.shape, NEG_INF, F32)
    l_ref[...] = jnp.zeros_like(l_ref)
    acc_ref[...] = jnp.zeros_like(acc_ref)
    scale = B_HEAD_DIM ** -0.5

    def att_tile(j, kind):
        k0 = pl.multiple_of(j * KT, KT)
        for half in range(QB // HALF):
            rows = slice(half * HALF, (half + 1) * HALF)
            s_t = sc_ref[j, rows, :]
            thr = thr_ref[rows, :]
            mask0 = (s_t[:, :HALF] >= thr) & (s_t[:, :HALF] > NEG_INF)
            mask1 = (s_t[:, HALF:] >= thr) & (s_t[:, HALF:] > NEG_INF)
            for h in range(B_HEADS):
                i = half * B_HEADS + h
                qh = q_ref[rows, h * B_HEAD_DIM:(h + 1) * B_HEAD_DIM]
                kh = kv_ref[pl.ds(k0, KT), 2 * h * B_HEAD_DIM:(2 * h + 1) * B_HEAD_DIM]
                vh = kv_ref[pl.ds(k0, KT), (2 * h + 1) * B_HEAD_DIM:(2 * h + 2) * B_HEAD_DIM]
                s = lax.dot_general(qh, kh, nt, preferred_element_type=F32) * scale
                s0 = s[:, :HALF]
                s1 = s[:, HALF:]
                if kind == "diag":
                    s0 = s0 + b0_ref[h, rows, 0:HALF]
                    s1 = s1 + b0_ref[h, rows, HALF:KT]
                elif kind == "prev" and half == 0:
                    s1 = s1 + b1_ref[h]
                s0 = jnp.where(mask0, s0, NEG_INF)
                s1 = jnp.where(mask1, s1, NEG_INF)
                m_prev = m_ref[i]
                m_cur = jnp.maximum(jnp.max(s0, axis=1, keepdims=True), jnp.max(s1, axis=1, keepdims=True))
                m_next = jnp.maximum(m_prev, m_cur)
                m_safe = jnp.where(m_next == NEG_INF, 0.0, m_next)
                p0 = jnp.exp(s0 - m_safe)
                p1 = jnp.exp(s1 - m_safe)
                alpha = jnp.exp(m_prev - m_safe)
                l_ref[i] = alpha * l_ref[i] + (jnp.sum(p0, axis=1, keepdims=True)
                                               + jnp.sum(p1, axis=1, keepdims=True))
                p = jnp.concatenate([p0, p1], axis=1).astype(BF16)
                acc_ref[i] = alpha * acc_ref[i] + jnp.dot(p, vh, preferred_element_type=F32)
                m_ref[i] = m_next

    def att_body(j, carry):
        att_tile(j, "far")
        return carry

    lax.fori_loop(0, qb - 1, att_body, 0)

    @pl.when(qb >= 1)
    def _():
        att_tile(qb - 1, "prev")

    att_tile(qb, "diag")

    for half in range(QB // HALF):
        rows = slice(half * HALF, (half + 1) * HALF)
        for h in range(B_HEADS):
            i = half * B_HEADS + h
            cs = slice(h * B_HEAD_DIM, (h + 1) * B_HEAD_DIM)
            bg = bg_ref[rows, cs]
            o_ref[rows, cs] = (acc_ref[i] / l_ref[i] * (bg * jax.nn.sigmoid(bg))).astype(BF16)


def _dsa(rel_bias, q_idx, w_idx, q, p, k_idx, kv):
    single = pl.Buffered(1)
    n_state = (QB // HALF) * B_HEADS
    return pl.pallas_call(
        _dsa_kernel,
        grid=(NQB,),
        in_specs=[
            pl.BlockSpec(memory_space=pltpu.SMEM),
            pl.BlockSpec((IDX_HEADS, QB, IDX_DIM), lambda i: (0, i, 0)),
            pl.BlockSpec((QB, 128), lambda i: (i, 0)),
            pl.BlockSpec((QB, B_WIDTH), lambda i: (i, 0)),
            pl.BlockSpec((QB, B_WIDTH), lambda i: (i, P_BG // B_WIDTH)),
            pl.BlockSpec((SEQ, IDX_DIM), lambda i: (0, 0), pipeline_mode=single),
            pl.BlockSpec((SEQ, 2 * B_WIDTH), lambda i: (0, 0), pipeline_mode=single),
        ],
        out_specs=pl.BlockSpec((QB, B_WIDTH), lambda i: (i, 0)),
        out_shape=jax.ShapeDtypeStruct((SEQ, B_WIDTH), BF16),
        scratch_shapes=[
            pltpu.VMEM((NQB, QB, KT), F32),
            pltpu.VMEM((IDX_HEADS, QB, HALF), F32),
            pltpu.VMEM((B_HEADS, QB, KT), F32),
            pltpu.VMEM((B_HEADS, HALF, HALF), F32),
            pltpu.VMEM((QB, HALF), F32),
            pltpu.VMEM((n_state, HALF, HALF), F32),
            pltpu.VMEM((n_state, HALF, HALF), F32),
            pltpu.VMEM((n_state, HALF, B_HEAD_DIM), F32),
        ],
        compiler_params=pltpu.CompilerParams(
            dimension_semantics=("arbitrary",), vmem_limit_bytes=VMEM_LIMIT),
        name="dsa",
    )(rel_bias, q_idx, w_idx, q, p, k_idx, kv)


def _out_proj_kernel(ya_ref, yb_ref, ma_ref, mb_ref, x_ref, wpa_ref, wpb_ref, wo_ref, fnw_ref, o_ref):
    pa = jnp.dot(ya_ref[...], wpa_ref[...], preferred_element_type=F32)
    pb = jnp.dot(yb_ref[...], wpb_ref[...], preferred_element_type=F32)
    merged = jax.nn.sigmoid(ma_ref[...]) * pa + jax.nn.sigmoid(mb_ref[...]) * pb
    y = x_ref[...] + jnp.dot(merged.astype(BF16), wo_ref[...], preferred_element_type=F32)
    o_ref[...] = _rms(y, fnw_ref[...])


def _out_proj(y_a, y_b, p, x2, w_pa, w_pb, w_out, final_norm_w):
    tm = 256
    const = lambda i: (0, 0)
    single = pl.Buffered(1)
    return pl.pallas_call(
        _out_proj_kernel,
        grid=(SEQ // tm,),
        in_specs=[
            pl.BlockSpec((tm, A_WIDTH), lambda i: (i, 0)),
            pl.BlockSpec((tm, B_WIDTH), lambda i: (i, 0)),
            pl.BlockSpec((tm, D_MODEL), lambda i: (i, P_MA // D_MODEL)),
            pl.BlockSpec((tm, D_MODEL), lambda i: (i, P_MB // D_MODEL)),
            pl.BlockSpec((tm, D_MODEL), lambda i: (i, 0)),
            pl.BlockSpec((A_WIDTH, D_MODEL), const, pipeline_mode=single),
            pl.BlockSpec((B_WIDTH, D_MODEL), const, pipeline_mode=single),
            pl.BlockSpec((D_MODEL, D_MODEL), const, pipeline_mode=single),
            pl.BlockSpec((1, D_MODEL), const),
        ],
        out_specs=pl.BlockSpec((tm, D_MODEL), lambda i: (i, 0)),
        out_shape=jax.ShapeDtypeStruct((SEQ, D_MODEL), F32),
        compiler_params=pltpu.CompilerParams(
            dimension_semantics=("arbitrary",), vmem_limit_bytes=VMEM_LIMIT),
        name="out_proj",
    )(y_a, y_b, p, p, x2, w_pa, w_pb, w_out, final_norm_w)


def _regroup_w_in(w):
    w = w.astype(BF16)
    a_end = 4 * A_WIDTH
    cq_end = a_end + Q_RANK
    misc_end = cq_end + KV_RANK + IDX_DIM + IDX_HEADS
    pad = jnp.zeros((D_MODEL, P_BG - P_MISC - (KV_RANK + IDX_DIM + IDX_HEADS)), BF16)
    return jnp.concatenate([w[:, :misc_end], pad, w[:, misc_end:]], axis=1)


def kernel(x, norm_w, w_in, lb_table, gnorm_a, q_norm_w, kv_norm_w, w_uq, w_qidx, w_ukv, kidx_norm_w,
           kidx_norm_b, w_pa, w_pb, w_out, rel_bias, final_norm_w):
    assert x.shape == (1, SEQ, D_MODEL) and w_in.shape[0] == 1
    x2 = x.reshape(SEQ, D_MODEL)
    p = _in_proj(x2, norm_w, _regroup_w_in(w_in[0]))
    q, q_idx, kv, k_idx, w_idx = _proj2(
        p, q_norm_w, kv_norm_w, w_uq[0].astype(BF16), w_qidx[0].astype(BF16), w_ukv[0].astype(BF16),
        kidx_norm_w, kidx_norm_b)
    y_a = _hgrn2(p, lb_table, gnorm_a)
    y_b = _dsa(rel_bias, q_idx, w_idx, q, p, k_idx, kv)
    out = _out_proj(y_a, y_b, p, x2, w_pa[0].astype(BF16), w_pb[0].astype(BF16), w_out[0].astype(BF16),
                    final_norm_w.reshape(1, D_MODEL))
    return out.reshape(1, SEQ, D_MODEL)
```

```python
import math

import jax
import jax.numpy as jnp
import numpy as np
from jax import lax
from jax.experimental import pallas as pl
from jax.experimental.pallas import tpu as pltpu

F32 = jnp.float32
BF16 = jnp.bfloat16

D_MODEL = 2048
SEQ = 8192
EPS = 1e-6
A_HEADS = 8
A_HEAD_DIM = 128
A_WIDTH = 1024
B_HEADS = 8
B_HEAD_DIM = 128
B_WIDTH = 1024
Q_RANK = 512
KV_RANK = 256
IDX_HEADS = 16
IDX_DIM = 128
TOPK = 256
CHUNK = 64
REL_BUCKETS = 32
REL_MAX_DIST = 128

P_AQ, P_AF, P_AI, P_AG = 0, 1024, 2048, 3072
P_CQ = 4096
P_MISC = 4608
P_BG = 5120
P_MA = 6144
P_MB = 8192
P_WIDTH = 10240

VMEM_LIMIT = 60 * 1024 * 1024

QB = 256
KT = 256
NKT = SEQ // KT
KSUB = 128
CNT_ROWS = 32

SUB = 16
TB = 256

NEG_INF = float("-inf")
INT_MIN = -2147483648
KEY_NEG_FLT_MAX = -2139095040

NT_DIMS = (((1,), (1,)), ((), ()))
TN_DIMS = (((0,), (0,)), ((), ()))


def _t5_thresholds():
    half = REL_BUCKETS // 2
    max_exact = half // 2
    n = np.arange(1, 4 * REL_MAX_DIST, dtype=np.int64)
    large = max_exact + (np.log(np.maximum(n, 1).astype(np.float64) / max_exact)
                         / math.log(REL_MAX_DIST / max_exact) * (half - max_exact)).astype(np.int32)
    large = np.minimum(large, half - 1)
    b = np.where(n < max_exact, n, large)
    thr = [int(n[i + 1]) for i in np.nonzero(np.diff(b))[0] if n[i + 1] > max_exact]
    assert len(thr) == half - 1 - max_exact and b[-1] == half - 1 and thr[-1] < REL_MAX_DIST
    return max_exact, half, thr


T5_MAX_EXACT, T5_HALF, T5_THR = _t5_thresholds()


def _in_proj_kernel(x_ref, nw_ref, w_ref, o_ref, h_ref):
    @pl.when(pl.program_id(1) == 0)
    def _():
        x = x_ref[...]
        ms = jnp.mean(x * x, axis=-1, keepdims=True)
        h_ref[...] = (x * lax.rsqrt(ms + EPS) * nw_ref[...]).astype(BF16)

    o_ref[...] = jnp.dot(h_ref[...], w_ref[...], preferred_element_type=F32)


def _in_proj(x2, norm_w, w_in_p):
    tm, tn = 1024, 1024
    return pl.pallas_call(
        _in_proj_kernel,
        grid=(SEQ // tm, P_WIDTH // tn),
        in_specs=[
            pl.BlockSpec((tm, D_MODEL), lambda i, j: (i, 0)),
            pl.BlockSpec((1, D_MODEL), lambda i, j: (0, 0)),
            pl.BlockSpec((D_MODEL, tn), lambda i, j: (0, j)),
        ],
        out_specs=pl.BlockSpec((tm, tn), lambda i, j: (i, j)),
        out_shape=jax.ShapeDtypeStruct((SEQ, P_WIDTH), F32),
        scratch_shapes=[pltpu.VMEM((tm, D_MODEL), BF16)],
        compiler_params=pltpu.CompilerParams(
            dimension_semantics=("arbitrary", "arbitrary"), vmem_limit_bytes=VMEM_LIMIT),
        name="in_proj",
    )(x2, norm_w, w_in_p)


def _rms(x, w):
    return x * lax.rsqrt(jnp.mean(x * x, axis=-1, keepdims=True) + EPS) * w


def _proj2_kernel(cq_ref, misc_ref, qnw_ref, kvnw_ref, wuqt_ref, wqit_ref, wk_ref, wvt_ref, lnw_ref, lnb_ref,
                  qt_ref, qit_ref, k_ref, vt_ref, kidx_ref, wt_ref):
    cqn = _rms(cq_ref[...], qnw_ref[...]).astype(BF16)
    qt_ref[...] = lax.dot_general(wuqt_ref[...], cqn, NT_DIMS, preferred_element_type=F32).astype(BF16)
    qit = lax.dot_general(wqit_ref[...], cqn, NT_DIMS, preferred_element_type=F32)
    for h in range(IDX_HEADS):
        qit_ref[h] = qit[h * IDX_DIM:(h + 1) * IDX_DIM, :].astype(BF16)
    misc = misc_ref[...]
    ckvn = _rms(misc[:, :KV_RANK], kvnw_ref[...]).astype(BF16)
    k_ref[...] = jnp.dot(ckvn, wk_ref[...], preferred_element_type=F32).astype(BF16)
    vt_ref[0] = lax.dot_general(wvt_ref[...], ckvn, NT_DIMS, preferred_element_type=F32).astype(BF16)
    kr = misc[:, KV_RANK:KV_RANK + IDX_DIM]
    mu = jnp.mean(kr, axis=-1, keepdims=True)
    var = jnp.mean(jnp.square(kr - mu), axis=-1, keepdims=True)
    kidx_ref[...] = ((kr - mu) * lax.rsqrt(var + EPS) * lnw_ref[...] + lnb_ref[...]).astype(BF16)
    wt_ref[...] = (misc[:, KV_RANK + IDX_DIM:] * (IDX_HEADS ** -0.5 * IDX_DIM ** -0.5)).T


def _proj2(p, q_norm_w, kv_norm_w, w_uq_t, w_qidx_t, w_k, w_v_t, ln_w, ln_b):
    tm = KT
    const = lambda i: (0, 0)
    return pl.pallas_call(
        _proj2_kernel,
        grid=(SEQ // tm,),
        in_specs=[
            pl.BlockSpec((tm, 512), lambda i: (i, P_CQ // 512)),
            pl.BlockSpec((tm, 512), lambda i: (i, P_MISC // 512)),
            pl.BlockSpec((1, Q_RANK), const),
            pl.BlockSpec((1, KV_RANK), const),
            pl.BlockSpec((B_WIDTH, Q_RANK), const),
            pl.BlockSpec((IDX_HEADS * IDX_DIM, Q_RANK), const),
            pl.BlockSpec((KV_RANK, B_WIDTH), const),
            pl.BlockSpec((B_WIDTH, KV_RANK), const),
            pl.BlockSpec((1, IDX_DIM), const),
            pl.BlockSpec((1, IDX_DIM), const),
        ],
        out_specs=[
            pl.BlockSpec((B_WIDTH, tm), lambda i: (0, i)),
            pl.BlockSpec((IDX_HEADS, IDX_DIM, tm), lambda i: (0, 0, i)),
            pl.BlockSpec((tm, B_WIDTH), lambda i: (i, 0)),
            pl.BlockSpec((1, B_WIDTH, tm), lambda i: (i, 0, 0)),
            pl.BlockSpec((tm, IDX_DIM), lambda i: (i, 0)),
            pl.BlockSpec((128, tm), lambda i: (0, i)),
        ],
        out_shape=[
            jax.ShapeDtypeStruct((B_WIDTH, SEQ), BF16),
            jax.ShapeDtypeStruct((IDX_HEADS, IDX_DIM, SEQ), BF16),
            jax.ShapeDtypeStruct((SEQ, B_WIDTH), BF16),
            jax.ShapeDtypeStruct((NKT, B_WIDTH, KT), BF16),
            jax.ShapeDtypeStruct((SEQ, IDX_DIM), BF16),
            jax.ShapeDtypeStruct((128, SEQ), F32),
        ],
        compiler_params=pltpu.CompilerParams(
            dimension_semantics=("arbitrary",), vmem_limit_bytes=VMEM_LIMIT),
        name="proj2",
    )(p, p, q_norm_w, kv_norm_w, w_uq_t, w_qidx_t, w_k, w_v_t, ln_w, ln_b)


def _split3(x):
    x1 = x.astype(BF16)
    r1 = x - x1.astype(F32)
    x2 = r1.astype(BF16)
    x3 = (r1 - x2.astype(F32)).astype(BF16)
    return x1, x2, x3


def _hgrn2_kernel(aq_ref, af_ref, ai_ref, ag_ref, lbt_ref, gn_ref, o_ref,
                  st_ref, sel_ref, a_ref, b_ref, kd_ref, v_ref, e_ref, oacc_ref):
    @pl.when(pl.program_id(0) == 0)
    def _():
        st_ref[...] = jnp.zeros_like(st_ref)
        r = lax.broadcasted_iota(jnp.int32, (TB, TB), 0)
        c = lax.broadcasted_iota(jnp.int32, (TB, TB), 1)
        same = (r // SUB) == (c // SUB)
        sel_ref[0:TB, :] = jnp.where(same & (c <= r), 1.0, 0.0).astype(BF16)
        sel_ref[TB:2 * TB, :] = jnp.where(same, 1.0, 0.0).astype(BF16)

    lbt = lbt_ref[...]
    mx = jnp.maximum(lbt[0:1], lbt[1:2])
    e0 = jnp.exp(lbt[0:1] - mx)
    e1 = jnp.exp(lbt[1:2] - mx)
    lb = e0 / (e0 + e1)

    f = lb + (1.0 - lb) * jax.nn.sigmoid(af_ref[...])
    g = jnp.log(f)
    sel = sel_ref[...]
    g1, g2, g3 = _split3(g)
    ct = (jnp.dot(sel, g1, preferred_element_type=F32) + jnp.dot(sel, g2, preferred_element_type=F32)
          + jnp.dot(sel, g3, preferred_element_type=F32))
    cum = ct[0:TB]
    tot = ct[TB:2 * TB]
    aq = aq_ref[...]
    q = aq * jax.nn.sigmoid(aq) * (A_HEAD_DIM ** -0.5)
    kk = 1.0 - f
    a_ref[...] = (q * jnp.exp(cum)).astype(BF16)
    b_ref[...] = (kk * jnp.exp(-cum)).astype(BF16)
    kd_ref[...] = (kk * jnp.exp(tot - cum)).astype(BF16)
    v_ref[...] = ai_ref[...].astype(BF16)
    e_ref[...] = jnp.exp(tot)

    tr = lax.broadcasted_iota(jnp.int32, (SUB, SUB), 0)
    tc = lax.broadcasted_iota(jnp.int32, (SUB, SUB), 1)
    causal = tc <= tr

    def chunk_body(c, carry):
        r0 = pl.multiple_of(c * SUB, SUB)
        heads = [slice(h * A_HEAD_DIM, (h + 1) * A_HEAD_DIM) for h in range(A_HEADS)]
        ah = [a_ref[pl.ds(r0, SUB), cs] for cs in heads]
        vh = [v_ref[pl.ds(r0, SUB), cs] for cs in heads]
        sc = [lax.dot_general(ah[h], b_ref[pl.ds(r0, SUB), heads[h]], NT_DIMS, preferred_element_type=F32)
              for h in range(A_HEADS)]
        inter = []
        for h in range(A_HEADS):
            st = st_ref[h]
            inter.append(lax.dot_general(ah[h], st.astype(BF16), NT_DIMS, preferred_element_type=F32))
            dst = lax.dot_general(vh[h], kd_ref[pl.ds(r0, SUB), heads[h]], TN_DIMS, preferred_element_type=F32)
            st_ref[h] = st * e_ref[pl.ds(r0, 1), heads[h]] + dst
        for h in range(A_HEADS):
            pm = jnp.where(causal, sc[h], 0.0).astype(BF16)
            oacc_ref[pl.ds(r0, SUB), heads[h]] = inter[h] + jnp.dot(pm, vh[h], preferred_element_type=F32)
        return carry

    lax.fori_loop(0, TB // SUB, chunk_body, 0)

    gn = gn_ref[...]
    ag = ag_ref[...]
    gate = ag * jax.nn.sigmoid(ag)
    for h in range(A_HEADS):
        cs = slice(h * A_HEAD_DIM, (h + 1) * A_HEAD_DIM)
        o = oacc_ref[:, cs]
        on = o * lax.rsqrt(jnp.mean(o * o, axis=-1, keepdims=True) + EPS) * gn
        o_ref[:, cs] = (on * gate[:, cs]).astype(BF16)


def _hgrn2(p, lb_table, gnorm):
    blk = lambda col: pl.BlockSpec((TB, A_WIDTH), lambda i: (i, col // A_WIDTH))
    return pl.pallas_call(
        _hgrn2_kernel,
        grid=(SEQ // TB,),
        in_specs=[blk(P_AQ), blk(P_AF), blk(P_AI), blk(P_AG),
                  pl.BlockSpec((2, A_WIDTH), lambda i: (0, 0)),
                  pl.BlockSpec((1, A_HEAD_DIM), lambda i: (0, 0))],
        out_specs=pl.BlockSpec((TB, A_WIDTH), lambda i: (i, 0)),
        out_shape=jax.ShapeDtypeStruct((SEQ, A_WIDTH), BF16),
        scratch_shapes=[
            pltpu.VMEM((A_HEADS, A_HEAD_DIM, A_HEAD_DIM), F32),
            pltpu.VMEM((2 * TB, TB), BF16),
            pltpu.VMEM((TB, A_WIDTH), BF16),
            pltpu.VMEM((TB, A_WIDTH), BF16),
            pltpu.VMEM((TB, A_WIDTH), BF16),
            pltpu.VMEM((TB, A_WIDTH), BF16),
            pltpu.VMEM((TB, A_WIDTH), F32),
            pltpu.VMEM((TB, A_WIDTH), F32),
        ],
        compiler_params=pltpu.CompilerParams(
            dimension_semantics=("arbitrary",), vmem_limit_bytes=VMEM_LIMIT),
        name="hgrn2",
    )(p, p, p, p, lb_table, gnorm)


def _key_to_float(k):
    return lax.bitcast_convert_type(jnp.where(k >= 0, k, k ^ 0x7FFFFFFF), F32)


def _t5_bucket(rel):
    n = jnp.abs(rel)
    large = jnp.full(rel.shape, T5_MAX_EXACT, jnp.int32)
    for t in T5_THR:
        large = large + jnp.where(n >= t, 1, 0)
    return jnp.where(rel > 0, T5_HALF, 0) + jnp.where(n < T5_MAX_EXACT, n, large)


def _dsa_kernel(relb_ref, qi_ref, w_ref, q_ref, bg_ref, kidx_ref, k_ref, v_ref, o_ref,
                sc_ref, b0_ref, b1_ref, thr_ref, m_ref, l_ref, acc_ref):
    qb = pl.program_id(0)

    @pl.when(qb == 0)
    def _():
        def bias_tile(rel, h):
            bucket = _t5_bucket(rel)
            val = jnp.zeros(rel.shape, F32)
            for b in range(REL_BUCKETS):
                val = jnp.where(bucket == b, relb_ref[b, h], val)
            return val - relb_ref[T5_HALF - 1, h]

        r0 = lax.broadcasted_iota(jnp.int32, (KT, QB), 0)
        c0 = lax.broadcasted_iota(jnp.int32, (KT, QB), 1)
        r1 = lax.broadcasted_iota(jnp.int32, (128, 128), 0)
        c1 = lax.broadcasted_iota(jnp.int32, (128, 128), 1)
        for h in range(B_HEADS):
            b0_ref[h] = bias_tile(r0 - c0, h)
            b1_ref[h] = bias_tile(r1 - 128 - c1, h)

    def idx_tile(j, diag):
        for ks in range(KT // KSUB):
            kt = kidx_ref[pl.ds(pl.multiple_of(j * KT + ks * KSUB, KSUB), KSUB), :]
            acc = jnp.zeros((KSUB, QB), F32)
            for h in range(IDX_HEADS):
                r = jnp.dot(kt, qi_ref[h], preferred_element_type=F32)
                acc = acc + w_ref[h:h + 1, :] * jnp.maximum(r, 0.0)
            if diag:
                rk = ks * KSUB + lax.broadcasted_iota(jnp.int32, (KSUB, QB), 0)
                ct = lax.broadcasted_iota(jnp.int32, (KSUB, QB), 1)
                acc = jnp.where((rk // CHUNK) <= (ct // CHUNK), acc, NEG_INF)
            sc_ref[j, ks * KSUB:(ks + 1) * KSUB, :] = acc

    def idx_body(j, carry):
        idx_tile(j, False)
        return carry

    lax.fori_loop(0, qb, idx_body, 0)
    idx_tile(qb, True)

    ntiles = qb + 1

    def count_ge(thr):
        def body(j, cnt):
            sel = jnp.where(sc_ref[j] >= thr, 1.0, 0.0)
            return cnt + jnp.sum(sel.reshape(KT // CNT_ROWS, CNT_ROWS, QB), axis=0)
        cnt = lax.fori_loop(0, ntiles, body, jnp.zeros((CNT_ROWS, QB), F32))
        return jnp.sum(cnt, axis=0, keepdims=True)

    c_pos = count_ge(jnp.zeros((1, QB), F32))
    key0 = jnp.where(c_pos >= float(TOPK), jnp.zeros((1, QB), jnp.int32), jnp.full((1, QB), INT_MIN, jnp.int32))

    def bit_body(i, key):
        cand = key | jnp.left_shift(jnp.int32(1), 30 - i)
        c = count_ge(_key_to_float(cand))
        return jnp.where(c >= float(TOPK), cand, key)

    key = lax.fori_loop(0, 31, bit_body, key0)
    thr_ref[...] = jnp.where(key < KEY_NEG_FLT_MAX, NEG_INF, _key_to_float(key))

    m_ref[...] = jnp.full(m_ref.shape, NEG_INF, F32)
    l_ref[...] = jnp.zeros_like(l_ref)
    acc_ref[...] = jnp.zeros_like(acc_ref)
    scale = B_HEAD_DIM ** -0.5

    def att_tile(j, kind):
        k0 = pl.multiple_of(j * KT, KT)
        s_t = sc_ref[j]
        mask = (s_t >= thr_ref[...]) & (s_t > NEG_INF)
        def qk(h):
            hs = slice(h * B_HEAD_DIM, (h + 1) * B_HEAD_DIM)
            return jnp.dot(k_ref[pl.ds(k0, KT), hs], q_ref[hs, :], preferred_element_type=F32)

        s_all = [qk(h) for h in range(B_HEADS)]
        for h in range(B_HEADS):
            hs = slice(h * B_HEAD_DIM, (h + 1) * B_HEAD_DIM)
            s = s_all[h] * scale
            if kind == "diag":
                s = s + b0_ref[h]
            elif kind == "prev":
                lo = jnp.concatenate([s[128:, :128] + b1_ref[h], s[128:, 128:]], axis=1)
                s = jnp.concatenate([s[:128, :], lo], axis=0)
            s = jnp.where(mask, s, NEG_INF)
            m_prev = m_ref[h]
            m_next = jnp.maximum(m_prev, jnp.max(s, axis=0, keepdims=True))
            m_safe = jnp.where(m_next == NEG_INF, 0.0, m_next)
            p = jnp.exp(s - m_safe)
            alpha = jnp.exp(m_prev - m_safe)
            l_ref[h] = alpha * l_ref[h] + jnp.sum(p, axis=0, keepdims=True)
            acc_ref[h] = alpha * acc_ref[h] + jnp.dot(v_ref[j, hs, :], p.astype(BF16), preferred_element_type=F32)
            m_ref[h] = m_next

    def att_body(j, carry):
        att_tile(j, "far")
        return carry

    lax.fori_loop(0, qb - 1, att_body, 0)

    @pl.when(qb >= 1)
    def _():
        att_tile(qb - 1, "prev")

    att_tile(qb, "diag")

    for h in range(B_HEADS):
        hs = slice(h * B_HEAD_DIM, (h + 1) * B_HEAD_DIM)
        bg = bg_ref[:, hs]
        o_ref[:, hs] = ((acc_ref[h] / l_ref[h]).T * (bg * jax.nn.sigmoid(bg))).astype(BF16)


def _dsa(rel_bias, q_idx_t, w_idx_t, q_t, p, k_idx, k, v_t):
    single = pl.Buffered(1)
    return pl.pallas_call(
        _dsa_kernel,
        grid=(SEQ // QB,),
        in_specs=[
            pl.BlockSpec(memory_space=pltpu.SMEM),
            pl.BlockSpec((IDX_HEADS, IDX_DIM, QB), lambda i: (0, 0, i)),
            pl.BlockSpec((128, QB), lambda i: (0, i)),
            pl.BlockSpec((B_WIDTH, QB), lambda i: (0, i)),
            pl.BlockSpec((QB, B_WIDTH), lambda i: (i, P_BG // B_WIDTH)),
            pl.BlockSpec((SEQ, IDX_DIM), lambda i: (0, 0), pipeline_mode=single),
            pl.BlockSpec((SEQ, B_WIDTH), lambda i: (0, 0), pipeline_mode=single),
            pl.BlockSpec((NKT, B_WIDTH, KT), lambda i: (0, 0, 0), pipeline_mode=single),
        ],
        out_specs=pl.BlockSpec((QB, B_WIDTH), lambda i: (i, 0)),
        out_shape=jax.ShapeDtypeStruct((SEQ, B_WIDTH), BF16),
        scratch_shapes=[
            pltpu.VMEM((NKT, KT, QB), F32),
            pltpu.VMEM((B_HEADS, KT, QB), F32),
            pltpu.VMEM((B_HEADS, 128, 128), F32),
            pltpu.VMEM((1, QB), F32),
            pltpu.VMEM((B_HEADS, 1, QB), F32),
            pltpu.VMEM((B_HEADS, 1, QB), F32),
            pltpu.VMEM((B_HEADS, B_HEAD_DIM, QB), F32),
        ],
        compiler_params=pltpu.CompilerParams(
            dimension_semantics=("arbitrary",), vmem_limit_bytes=VMEM_LIMIT),
        name="dsa",
    )(rel_bias, q_idx_t, w_idx_t, q_t, p, k_idx, k, v_t)


def _out_proj_kernel(ya_ref, yb_ref, ma_ref, mb_ref, x_ref, wpa_ref, wpb_ref, wo_ref, fnw_ref, o_ref):
    pa = jnp.dot(ya_ref[...], wpa_ref[...], preferred_element_type=F32)
    pb = jnp.dot(yb_ref[...], wpb_ref[...], preferred_element_type=F32)
    merged = jax.nn.sigmoid(ma_ref[...]) * pa + jax.nn.sigmoid(mb_ref[...]) * pb
    y = x_ref[...] + jnp.dot(merged.astype(BF16), wo_ref[...], preferred_element_type=F32)
    o_ref[...] = _rms(y, fnw_ref[...])


def _out_proj(y_a, y_b, p, x2, w_pa, w_pb, w_out, final_norm_w):
    tm = 256
    const = lambda i: (0, 0)
    single = pl.Buffered(1)
    return pl.pallas_call(
        _out_proj_kernel,
        grid=(SEQ // tm,),
        in_specs=[
            pl.BlockSpec((tm, A_WIDTH), lambda i: (i, 0)),
            pl.BlockSpec((tm, B_WIDTH), lambda i: (i, 0)),
            pl.BlockSpec((tm, D_MODEL), lambda i: (i, P_MA // D_MODEL)),
            pl.BlockSpec((tm, D_MODEL), lambda i: (i, P_MB // D_MODEL)),
            pl.BlockSpec((tm, D_MODEL), lambda i: (i, 0)),
            pl.BlockSpec((A_WIDTH, D_MODEL), const, pipeline_mode=single),
            pl.BlockSpec((B_WIDTH, D_MODEL), const, pipeline_mode=single),
            pl.BlockSpec((D_MODEL, D_MODEL), const, pipeline_mode=single),
            pl.BlockSpec((1, D_MODEL), const),
        ],
        out_specs=pl.BlockSpec((tm, D_MODEL), lambda i: (i, 0)),
        out_shape=jax.ShapeDtypeStruct((SEQ, D_MODEL), F32),
        compiler_params=pltpu.CompilerParams(
            dimension_semantics=("arbitrary",), vmem_limit_bytes=VMEM_LIMIT),
        name="out_proj",
    )(y_a, y_b, p, p, x2, w_pa, w_pb, w_out, final_norm_w)


def _regroup_w_in(w):
    w = w.astype(BF16)
    misc_end = 4 * A_WIDTH + Q_RANK + KV_RANK + IDX_DIM + IDX_HEADS
    pad = jnp.zeros((D_MODEL, P_BG - P_MISC - (KV_RANK + IDX_DIM + IDX_HEADS)), BF16)
    return jnp.concatenate([w[:, :misc_end], pad, w[:, misc_end:]], axis=1)


def kernel(x, norm_w, w_in, lb_table, gnorm_a, q_norm_w, kv_norm_w, w_uq, w_qidx, w_ukv, kidx_norm_w,
           kidx_norm_b, w_pa, w_pb, w_out, rel_bias, final_norm_w):
    assert x.shape == (1, SEQ, D_MODEL) and w_in.shape[0] == 1
    x2 = x.reshape(SEQ, D_MODEL)
    p = _in_proj(x2, norm_w, _regroup_w_in(w_in[0]))
    w_kv = w_ukv[0].astype(BF16).reshape(KV_RANK, B_HEADS, 2, B_HEAD_DIM)
    w_k = w_kv[:, :, 0, :].reshape(KV_RANK, B_WIDTH)
    w_v_t = w_kv[:, :, 1, :].reshape(KV_RANK, B_WIDTH).T
    q_t, q_idx_t, k, v_t, k_idx, w_idx_t = _proj2(
        p, q_norm_w, kv_norm_w, w_uq[0].astype(BF16).T, w_qidx[0].astype(BF16).T, w_k, w_v_t,
        kidx_norm_w, kidx_norm_b)
    y_a = _hgrn2(p, lb_table, gnorm_a)
    y_b = _dsa(rel_bias, q_idx_t, w_idx_t, q_t, p, k_idx, k, v_t)
    out = _out_proj(y_a, y_b, p, x2, w_pa[0].astype(BF16), w_pb[0].astype(BF16), w_out[0].astype(BF16),
                    final_norm_w.reshape(1, D_MODEL))
    return out.reshape(1, SEQ, D_MODEL)
```

```python
import math

import jax
import jax.numpy as jnp
import numpy as np
from jax import lax
from jax.experimental import pallas as pl
from jax.experimental.pallas import tpu as pltpu

F32 = jnp.float32
BF16 = jnp.bfloat16

D_MODEL = 2048
SEQ = 8192
EPS = 1e-6
A_HEADS = 8
A_HEAD_DIM = 128
A_WIDTH = 1024
B_HEADS = 8
B_HEAD_DIM = 128
B_WIDTH = 1024
Q_RANK = 512
KV_RANK = 256
IDX_HEADS = 16
IDX_DIM = 128
TOPK = 256
CHUNK = 64
REL_BUCKETS = 32
REL_MAX_DIST = 128

P_AQ, P_AF, P_AI, P_AG = 0, 1024, 2048, 3072
P_CQ = 4096
P_MISC = 4608
P_BG = 5120
P_MA = 6144
P_MB = 8192
P_WIDTH = 10240

VMEM_LIMIT = 60 * 1024 * 1024

QB = 256
KT = 256
NKT = SEQ // KT
KSUB = 128
CNT_ROWS = 32
SUM_ROWS = 16
ATT_TILES = 2
UPDATE_LAG = 2

SUB = 16
TB = 256

NEG_INF = float("-inf")
LOG2E = math.log2(math.e)
INT_MIN = -2147483648
KEY_NEG_FLT_MAX = -2139095040

NT_DIMS = (((1,), (1,)), ((), ()))
TN_DIMS = (((0,), (0,)), ((), ()))


def _t5_thresholds():
    half = REL_BUCKETS // 2
    max_exact = half // 2
    n = np.arange(1, 4 * REL_MAX_DIST, dtype=np.int64)
    large = max_exact + (np.log(np.maximum(n, 1).astype(np.float64) / max_exact)
                         / math.log(REL_MAX_DIST / max_exact) * (half - max_exact)).astype(np.int32)
    large = np.minimum(large, half - 1)
    b = np.where(n < max_exact, n, large)
    thr = [int(n[i + 1]) for i in np.nonzero(np.diff(b))[0] if n[i + 1] > max_exact]
    assert len(thr) == half - 1 - max_exact and b[-1] == half - 1 and thr[-1] < REL_MAX_DIST
    return max_exact, half, thr


T5_MAX_EXACT, T5_HALF, T5_THR = _t5_thresholds()


def _in_proj_kernel(x_ref, nw_ref, w_ref, o_ref, h_ref):
    @pl.when(pl.program_id(1) == 0)
    def _():
        x = x_ref[...]
        ms = jnp.mean(x * x, axis=-1, keepdims=True)
        h_ref[...] = (x * lax.rsqrt(ms + EPS) * nw_ref[...]).astype(BF16)

    o_ref[...] = jnp.dot(h_ref[...], w_ref[...], preferred_element_type=F32)


def _in_proj(x2, norm_w, w_in_p):
    tm, tn = 1024, 1024
    return pl.pallas_call(
        _in_proj_kernel,
        grid=(SEQ // tm, P_WIDTH // tn),
        in_specs=[
            pl.BlockSpec((tm, D_MODEL), lambda i, j: (i, 0)),
            pl.BlockSpec((1, D_MODEL), lambda i, j: (0, 0)),
            pl.BlockSpec((D_MODEL, tn), lambda i, j: (0, j)),
        ],
        out_specs=pl.BlockSpec((tm, tn), lambda i, j: (i, j)),
        out_shape=jax.ShapeDtypeStruct((SEQ, P_WIDTH), F32),
        scratch_shapes=[pltpu.VMEM((tm, D_MODEL), BF16)],
        compiler_params=pltpu.CompilerParams(
            dimension_semantics=("arbitrary", "arbitrary"), vmem_limit_bytes=VMEM_LIMIT),
        name="in_proj",
    )(x2, norm_w, w_in_p)


def _rms(x, w):
    return x * lax.rsqrt(jnp.mean(x * x, axis=-1, keepdims=True) + EPS) * w


def _proj2_kernel(cq_ref, misc_ref, qnw_ref, kvnw_ref, wuqt_ref, wqit_ref, wk_ref, wvt_ref, lnw_ref, lnb_ref,
                  qt_ref, qit_ref, k_ref, vt_ref, kidx_ref, wt_ref):
    cqn = _rms(cq_ref[...], qnw_ref[...]).astype(BF16)
    qt = lax.dot_general(wuqt_ref[...], cqn, NT_DIMS, preferred_element_type=F32)
    qt_ref[...] = (qt * (B_HEAD_DIM ** -0.5 * LOG2E)).astype(BF16)
    qit = lax.dot_general(wqit_ref[...], cqn, NT_DIMS, preferred_element_type=F32)
    for h in range(IDX_HEADS):
        qit_ref[h] = qit[h * IDX_DIM:(h + 1) * IDX_DIM, :].astype(BF16)
    misc = misc_ref[...]
    ckvn = _rms(misc[:, :KV_RANK], kvnw_ref[...]).astype(BF16)
    k_ref[...] = jnp.dot(ckvn, wk_ref[...], preferred_element_type=F32).astype(BF16)
    vt_ref[0] = lax.dot_general(wvt_ref[...], ckvn, NT_DIMS, preferred_element_type=F32).astype(BF16)
    kr = misc[:, KV_RANK:KV_RANK + IDX_DIM]
    mu = jnp.mean(kr, axis=-1, keepdims=True)
    var = jnp.mean(jnp.square(kr - mu), axis=-1, keepdims=True)
    kidx_ref[...] = ((kr - mu) * lax.rsqrt(var + EPS) * lnw_ref[...] + lnb_ref[...]).astype(BF16)
    wt_ref[...] = (misc[:, KV_RANK + IDX_DIM:] * (IDX_HEADS ** -0.5 * IDX_DIM ** -0.5)).T


def _proj2(p, q_norm_w, kv_norm_w, w_uq_t, w_qidx_t, w_k, w_v_t, ln_w, ln_b):
    tm = KT
    const = lambda i: (0, 0)
    return pl.pallas_call(
        _proj2_kernel,
        grid=(SEQ // tm,),
        in_specs=[
            pl.BlockSpec((tm, 512), lambda i: (i, P_CQ // 512)),
            pl.BlockSpec((tm, 512), lambda i: (i, P_MISC // 512)),
            pl.BlockSpec((1, Q_RANK), const),
            pl.BlockSpec((1, KV_RANK), const),
            pl.BlockSpec((B_WIDTH, Q_RANK), const),
            pl.BlockSpec((IDX_HEADS * IDX_DIM, Q_RANK), const),
            pl.BlockSpec((KV_RANK, B_WIDTH), const),
            pl.BlockSpec((B_WIDTH, KV_RANK), const),
            pl.BlockSpec((1, IDX_DIM), const),
            pl.BlockSpec((1, IDX_DIM), const),
        ],
        out_specs=[
            pl.BlockSpec((B_WIDTH, tm), lambda i: (0, i)),
            pl.BlockSpec((IDX_HEADS, IDX_DIM, tm), lambda i: (0, 0, i)),
            pl.BlockSpec((tm, B_WIDTH), lambda i: (i, 0)),
            pl.BlockSpec((1, B_WIDTH, tm), lambda i: (i, 0, 0)),
            pl.BlockSpec((tm, IDX_DIM), lambda i: (i, 0)),
            pl.BlockSpec((128, tm), lambda i: (0, i)),
        ],
        out_shape=[
            jax.ShapeDtypeStruct((B_WIDTH, SEQ), BF16),
            jax.ShapeDtypeStruct((IDX_HEADS, IDX_DIM, SEQ), BF16),
            jax.ShapeDtypeStruct((SEQ, B_WIDTH), BF16),
            jax.ShapeDtypeStruct((NKT, B_WIDTH, KT), BF16),
            jax.ShapeDtypeStruct((SEQ, IDX_DIM), BF16),
            jax.ShapeDtypeStruct((128, SEQ), F32),
        ],
        compiler_params=pltpu.CompilerParams(
            dimension_semantics=("arbitrary",), vmem_limit_bytes=VMEM_LIMIT),
        name="proj2",
    )(p, p, q_norm_w, kv_norm_w, w_uq_t, w_qidx_t, w_k, w_v_t, ln_w, ln_b)


def _split3(x):
    x1 = x.astype(BF16)
    r1 = x - x1.astype(F32)
    x2 = r1.astype(BF16)
    x3 = (r1 - x2.astype(F32)).astype(BF16)
    return x1, x2, x3


def _hgrn2_kernel(aq_ref, af_ref, ai_ref, ag_ref, lbt_ref, gn_ref, o_ref,
                  st_ref, sel_ref, a_ref, b_ref, kd_ref, v_ref, e_ref, oacc_ref):
    @pl.when(pl.program_id(0) == 0)
    def _():
        st_ref[...] = jnp.zeros_like(st_ref)
        r = lax.broadcasted_iota(jnp.int32, (TB, TB), 0)
        c = lax.broadcasted_iota(jnp.int32, (TB, TB), 1)
        same = (r // SUB) == (c // SUB)
        sel_ref[0:TB, :] = jnp.where(same & (c <= r), 1.0, 0.0).astype(BF16)
        sel_ref[TB:2 * TB, :] = jnp.where(same, 1.0, 0.0).astype(BF16)

    lbt = lbt_ref[...]
    mx = jnp.maximum(lbt[0:1], lbt[1:2])
    e0 = jnp.exp(lbt[0:1] - mx)
    e1 = jnp.exp(lbt[1:2] - mx)
    lb = e0 / (e0 + e1)

    f = lb + (1.0 - lb) * jax.nn.sigmoid(af_ref[...])
    g = jnp.log(f)
    sel = sel_ref[...]
    g1, g2, g3 = _split3(g)
    ct = (jnp.dot(sel, g1, preferred_element_type=F32) + jnp.dot(sel, g2, preferred_element_type=F32)
          + jnp.dot(sel, g3, preferred_element_type=F32))
    cum = ct[0:TB]
    tot = ct[TB:2 * TB]
    aq = aq_ref[...]
    q = aq * jax.nn.sigmoid(aq) * (A_HEAD_DIM ** -0.5)
    kk = 1.0 - f
    a_ref[...] = (q * jnp.exp(cum)).astype(BF16)
    b_ref[...] = (kk * jnp.exp(-cum)).astype(BF16)
    kd_ref[...] = (kk * jnp.exp(tot - cum)).astype(BF16)
    v_ref[...] = ai_ref[...].astype(BF16)
    e_ref[...] = jnp.exp(tot)

    tr = lax.broadcasted_iota(jnp.int32, (SUB, SUB), 0)
    tc = lax.broadcasted_iota(jnp.int32, (SUB, SUB), 1)
    causal = tc <= tr

    def chunk_body(c, carry):
        r0 = pl.multiple_of(c * SUB, SUB)
        heads = [slice(h * A_HEAD_DIM, (h + 1) * A_HEAD_DIM) for h in range(A_HEADS)]
        ah = [a_ref[pl.ds(r0, SUB), cs] for cs in heads]
        vh = [v_ref[pl.ds(r0, SUB), cs] for cs in heads]
        sc = [lax.dot_general(ah[h], b_ref[pl.ds(r0, SUB), heads[h]], NT_DIMS, preferred_element_type=F32)
              for h in range(A_HEADS)]
        inter = []
        for h in range(A_HEADS):
            st = st_ref[h]
            inter.append(lax.dot_general(ah[h], st.astype(BF16), NT_DIMS, preferred_element_type=F32))
            dst = lax.dot_general(vh[h], kd_ref[pl.ds(r0, SUB), heads[h]], TN_DIMS, preferred_element_type=F32)
            st_ref[h] = st * e_ref[pl.ds(r0, 1), heads[h]] + dst
        for h in range(A_HEADS):
            pm = jnp.where(causal, sc[h], 0.0).astype(BF16)
            oacc_ref[pl.ds(r0, SUB), heads[h]] = inter[h] + jnp.dot(pm, vh[h], preferred_element_type=F32)
        return carry

    lax.fori_loop(0, TB // SUB, chunk_body, 0, unroll=8)

    gn = gn_ref[...]
    ag = ag_ref[...]
    gate = ag * jax.nn.sigmoid(ag)
    for h in range(A_HEADS):
        cs = slice(h * A_HEAD_DIM, (h + 1) * A_HEAD_DIM)
        o = oacc_ref[:, cs]
        on = o * lax.rsqrt(jnp.mean(o * o, axis=-1, keepdims=True) + EPS) * gn
        o_ref[:, cs] = (on * gate[:, cs]).astype(BF16)


def _hgrn2(p, lb_table, gnorm):
    blk = lambda col: pl.BlockSpec((TB, A_WIDTH), lambda i: (i, col // A_WIDTH))
    return pl.pallas_call(
        _hgrn2_kernel,
        grid=(SEQ // TB,),
        in_specs=[blk(P_AQ), blk(P_AF), blk(P_AI), blk(P_AG),
                  pl.BlockSpec((2, A_WIDTH), lambda i: (0, 0)),
                  pl.BlockSpec((1, A_HEAD_DIM), lambda i: (0, 0))],
        out_specs=pl.BlockSpec((TB, A_WIDTH), lambda i: (i, 0)),
        out_shape=jax.ShapeDtypeStruct((SEQ, A_WIDTH), BF16),
        scratch_shapes=[
            pltpu.VMEM((A_HEADS, A_HEAD_DIM, A_HEAD_DIM), F32),
            pltpu.VMEM((2 * TB, TB), BF16),
            pltpu.VMEM((TB, A_WIDTH), BF16),
            pltpu.VMEM((TB, A_WIDTH), BF16),
            pltpu.VMEM((TB, A_WIDTH), BF16),
            pltpu.VMEM((TB, A_WIDTH), BF16),
            pltpu.VMEM((TB, A_WIDTH), F32),
            pltpu.VMEM((TB, A_WIDTH), F32),
        ],
        compiler_params=pltpu.CompilerParams(
            dimension_semantics=("arbitrary",), vmem_limit_bytes=VMEM_LIMIT),
        name="hgrn2",
    )(p, p, p, p, lb_table, gnorm)


def _key_to_float(k):
    return lax.bitcast_convert_type(jnp.where(k >= 0, k, k ^ 0x7FFFFFFF), F32)


def _t5_bucket(rel):
    n = jnp.abs(rel)
    large = jnp.full(rel.shape, T5_MAX_EXACT, jnp.int32)
    for t in T5_THR:
        large = large + jnp.where(n >= t, 1, 0)
    return jnp.where(rel > 0, T5_HALF, 0) + jnp.where(n < T5_MAX_EXACT, n, large)


def _dsa_kernel(relb_ref, qi_ref, w_ref, q_ref, bg_ref, kidx_ref, k_ref, v_ref, o_ref,
                sc_ref, b0_ref, b1_ref, thr_ref, m_ref, l_ref, acc_ref, s_ref):
    qb = pl.program_id(0)

    @pl.when(qb == 0)
    def _():
        def bias_tile(rel, h):
            bucket = _t5_bucket(rel)
            val = jnp.zeros(rel.shape, F32)
            for b in range(REL_BUCKETS):
                val = jnp.where(bucket == b, relb_ref[b, h], val)
            return (val - relb_ref[T5_HALF - 1, h]) * LOG2E

        r0 = lax.broadcasted_iota(jnp.int32, (KT, QB), 0)
        c0 = lax.broadcasted_iota(jnp.int32, (KT, QB), 1)
        r1 = lax.broadcasted_iota(jnp.int32, (128, 128), 0)
        c1 = lax.broadcasted_iota(jnp.int32, (128, 128), 1)
        for h in range(B_HEADS):
            b0_ref[h] = bias_tile(r0 - c0, h)
            b1_ref[h] = bias_tile(r1 - 128 - c1, h)

    def idx_tile(j, diag):
        for ks in range(KT // KSUB):
            kt = kidx_ref[pl.ds(pl.multiple_of(j * KT + ks * KSUB, KSUB), KSUB), :]
            acc = jnp.zeros((KSUB, QB), F32)
            for h in range(IDX_HEADS):
                r = jnp.dot(kt, qi_ref[h], preferred_element_type=F32)
                acc = acc + w_ref[h:h + 1, :] * jnp.maximum(r, 0.0)
            if diag:
                rk = ks * KSUB + lax.broadcasted_iota(jnp.int32, (KSUB, QB), 0)
                ct = lax.broadcasted_iota(jnp.int32, (KSUB, QB), 1)
                acc = jnp.where((rk // CHUNK) <= (ct // CHUNK), acc, NEG_INF)
            sc_ref[j, ks * KSUB:(ks + 1) * KSUB, :] = acc

    def idx_body(j, carry):
        idx_tile(j, False)
        return carry

    lax.fori_loop(0, qb, idx_body, 0)
    idx_tile(qb, True)

    npairs = (qb + 2) // 2

    @pl.when(qb % 2 == 0)
    def _():
        sc_ref[qb + 1] = jnp.full((KT, QB), NEG_INF, F32)

    def count_ge(thr):
        def body(jp, cnt):
            for u in range(2):
                sel = jnp.where(sc_ref[2 * jp + u] >= thr, 1.0, 0.0)
                cnt = cnt + jnp.sum(sel.reshape(KT // CNT_ROWS, CNT_ROWS, QB), axis=0)
            return cnt
        cnt = lax.fori_loop(0, npairs, body, jnp.zeros((CNT_ROWS, QB), F32))
        return jnp.sum(cnt, axis=0, keepdims=True)

    c_pos = count_ge(jnp.zeros((1, QB), F32))
    key0 = jnp.where(c_pos >= float(TOPK), jnp.zeros((1, QB), jnp.int32), jnp.full((1, QB), INT_MIN, jnp.int32))

    def bit_body(i, key):
        cand = key | jnp.left_shift(jnp.int32(1), 30 - i)
        c = count_ge(_key_to_float(cand))
        return jnp.where(c >= float(TOPK), cand, key)

    key = lax.fori_loop(0, 31, bit_body, key0)
    thr_ref[...] = jnp.where(key < KEY_NEG_FLT_MAX, NEG_INF, _key_to_float(key))

    m_ref[...] = jnp.full(m_ref.shape, NEG_INF, F32)
    l_ref[...] = jnp.zeros_like(l_ref)
    acc_ref[...] = jnp.zeros_like(acc_ref)
    ones_rows = jnp.ones((SUM_ROWS, KT), BF16)

    def att_step(tiles):
        masks = []
        for j, _ in tiles:
            s_t = sc_ref[j]
            masks.append(jnp.where((s_t >= thr_ref[...]) & (s_t > NEG_INF), 0.0, NEG_INF))

        def qk(h, j):
            hs = slice(h * B_HEAD_DIM, (h + 1) * B_HEAD_DIM)
            return jnp.dot(k_ref[pl.ds(pl.multiple_of(j * KT, KT), KT), hs], q_ref[hs, :],
                           preferred_element_type=F32)

        m_safe, alpha = [], []
        for h in range(B_HEADS):
            m_prev = m_ref[h]
            m_next = m_prev
            for ti, ((j, kind), mask) in enumerate(zip(tiles, masks)):
                s = qk(h, j)
                if kind == "diag":
                    s = s + b0_ref[h]
                elif kind == "prev":
                    lo = jnp.concatenate([s[128:, :128] + b1_ref[h], s[128:, 128:]], axis=1)
                    s = jnp.concatenate([s[:128, :], lo], axis=0)
                s = s + mask
                s_ref[h, ti * KT:(ti + 1) * KT, :] = s
                m_next = jnp.maximum(m_next, jnp.max(s, axis=0, keepdims=True))
            m_safe.append(jnp.where(m_next == NEG_INF, 0.0, m_next))
            alpha.append(jnp.exp2(m_prev - m_safe[h]))
            m_ref[h] = m_next

        def update(h, pv):
            l_ref[h] = alpha[h] * l_ref[h] + pv[B_HEAD_DIM:B_HEAD_DIM + 1, :]
            acc_ref[h] = alpha[h] * acc_ref[h] + pv[:B_HEAD_DIM, :]

        pvs = []
        for h in range(B_HEADS):
            hs = slice(h * B_HEAD_DIM, (h + 1) * B_HEAD_DIM)
            pv = None
            for ti, (j, _) in enumerate(tiles):
                p = jnp.exp2(s_ref[h, ti * KT:(ti + 1) * KT, :] - m_safe[h]).astype(BF16)
                d = jnp.dot(jnp.concatenate([v_ref[j, hs, :], ones_rows], axis=0), p,
                            preferred_element_type=F32)
                pv = d if pv is None else pv + d
            pvs.append(pv)
            if h >= UPDATE_LAG:
                update(h - UPDATE_LAG, pvs[h - UPDATE_LAG])
        for h in range(B_HEADS - UPDATE_LAG, B_HEADS):
            update(h, pvs[h])

    nfar = jnp.maximum(qb - 1, 0)

    def att_body(i, carry):
        att_step([(2 * i, "far"), (2 * i + 1, "far")])
        return carry

    lax.fori_loop(0, nfar // 2, att_body, 0)

    @pl.when(nfar % 2 == 1)
    def _():
        att_step([(qb - 2, "far")])

    @pl.when(qb == 0)
    def _():
        att_step([(qb, "diag")])

    @pl.when(qb >= 1)
    def _():
        att_step([(qb - 1, "prev"), (qb, "diag")])

    for h in range(B_HEADS):
        hs = slice(h * B_HEAD_DIM, (h + 1) * B_HEAD_DIM)
        bg = bg_ref[:, hs]
        o_ref[:, hs] = ((acc_ref[h] / l_ref[h]).T * (bg * jax.nn.sigmoid(bg))).astype(BF16)


def _dsa(rel_bias, q_idx_t, w_idx_t, q_t, p, k_idx, k, v_t):
    single = pl.Buffered(1)
    return pl.pallas_call(
        _dsa_kernel,
        grid=(SEQ // QB,),
        in_specs=[
            pl.BlockSpec(memory_space=pltpu.SMEM),
            pl.BlockSpec((IDX_HEADS, IDX_DIM, QB), lambda i: (0, 0, i)),
            pl.BlockSpec((128, QB), lambda i: (0, i)),
            pl.BlockSpec((B_WIDTH, QB), lambda i: (0, i)),
            pl.BlockSpec((QB, B_WIDTH), lambda i: (i, P_BG // B_WIDTH)),
            pl.BlockSpec((SEQ, IDX_DIM), lambda i: (0, 0), pipeline_mode=single),
            pl.BlockSpec((SEQ, B_WIDTH), lambda i: (0, 0), pipeline_mode=single),
            pl.BlockSpec((NKT, B_WIDTH, KT), lambda i: (0, 0, 0), pipeline_mode=single),
        ],
        out_specs=pl.BlockSpec((QB, B_WIDTH), lambda i: (i, 0)),
        out_shape=jax.ShapeDtypeStruct((SEQ, B_WIDTH), BF16),
        scratch_shapes=[
            pltpu.VMEM((NKT, KT, QB), F32),
            pltpu.VMEM((B_HEADS, KT, QB), F32),
            pltpu.VMEM((B_HEADS, 128, 128), F32),
            pltpu.VMEM((1, QB), F32),
            pltpu.VMEM((B_HEADS, 1, QB), F32),
            pltpu.VMEM((B_HEADS, 1, QB), F32),
            pltpu.VMEM((B_HEADS, B_HEAD_DIM, QB), F32),
            pltpu.VMEM((B_HEADS, ATT_TILES * KT, QB), F32),
        ],
        compiler_params=pltpu.CompilerParams(
            dimension_semantics=("arbitrary",), vmem_limit_bytes=VMEM_LIMIT),
        name="dsa",
    )(rel_bias, q_idx_t, w_idx_t, q_t, p, k_idx, k, v_t)


def _out_proj_kernel(ya_ref, yb_ref, ma_ref, mb_ref, x_ref, wpa_ref, wpb_ref, wo_ref, fnw_ref, o_ref):
    pa = jnp.dot(ya_ref[...], wpa_ref[...], preferred_element_type=F32)
    pb = jnp.dot(yb_ref[...], wpb_ref[...], preferred_element_type=F32)
    merged = jax.nn.sigmoid(ma_ref[...]) * pa + jax.nn.sigmoid(mb_ref[...]) * pb
    y = x_ref[...] + jnp.dot(merged.astype(BF16), wo_ref[...], preferred_element_type=F32)
    o_ref[...] = _rms(y, fnw_ref[...])


def _out_proj(y_a, y_b, p, x2, w_pa, w_pb, w_out, final_norm_w):
    tm = 256
    const = lambda i: (0, 0)
    single = pl.Buffered(1)
    return pl.pallas_call(
        _out_proj_kernel,
        grid=(SEQ // tm,),
        in_specs=[
            pl.BlockSpec((tm, A_WIDTH), lambda i: (i, 0)),
            pl.BlockSpec((tm, B_WIDTH), lambda i: (i, 0)),
            pl.BlockSpec((tm, D_MODEL), lambda i: (i, P_MA // D_MODEL)),
            pl.BlockSpec((tm, D_MODEL), lambda i: (i, P_MB // D_MODEL)),
            pl.BlockSpec((tm, D_MODEL), lambda i: (i, 0)),
            pl.BlockSpec((A_WIDTH, D_MODEL), const, pipeline_mode=single),
            pl.BlockSpec((B_WIDTH, D_MODEL), const, pipeline_mode=single),
            pl.BlockSpec((D_MODEL, D_MODEL), const, pipeline_mode=single),
            pl.BlockSpec((1, D_MODEL), const),
        ],
        out_specs=pl.BlockSpec((tm, D_MODEL), lambda i: (i, 0)),
        out_shape=jax.ShapeDtypeStruct((SEQ, D_MODEL), F32),
        compiler_params=pltpu.CompilerParams(
            dimension_semantics=("arbitrary",), vmem_limit_bytes=VMEM_LIMIT),
        name="out_proj",
    )(y_a, y_b, p, p, x2, w_pa, w_pb, w_out, final_norm_w)


def _regroup_w_in(w):
    misc_end = 4 * A_WIDTH + Q_RANK + KV_RANK + IDX_DIM + IDX_HEADS
    pad = jnp.zeros((D_MODEL, P_BG - P_MISC - (KV_RANK + IDX_DIM + IDX_HEADS)), w.dtype)
    return jnp.concatenate([w[:, :misc_end], pad, w[:, misc_end:]], axis=1).astype(BF16)


def kernel(x, norm_w, w_in, lb_table, gnorm_a, q_norm_w, kv_norm_w, w_uq, w_qidx, w_ukv, kidx_norm_w,
           kidx_norm_b, w_pa, w_pb, w_out, rel_bias, final_norm_w):
    assert x.shape == (1, SEQ, D_MODEL) and w_in.shape[0] == 1
    x2 = x.reshape(SEQ, D_MODEL)
    p = _in_proj(x2, norm_w, _regroup_w_in(w_in[0]))
    w_kv = w_ukv[0].astype(BF16).reshape(KV_RANK, B_HEADS, 2, B_HEAD_DIM)
    w_k = w_kv[:, :, 0, :].reshape(KV_RANK, B_WIDTH)
    w_v_t = w_kv[:, :, 1, :].reshape(KV_RANK, B_WIDTH).T
    q_t, q_idx_t, k, v_t, k_idx, w_idx_t = _proj2(
        p, q_norm_w, kv_norm_w, w_uq[0].astype(BF16).T, w_qidx[0].astype(BF16).T, w_k, w_v_t,
        kidx_norm_w, kidx_norm_b)
    y_a = _hgrn2(p, lb_table, gnorm_a)
    y_b = _dsa(rel_bias, q_idx_t, w_idx_t, q_t, p, k_idx, k, v_t)
    out = _out_proj(y_a, y_b, p, x2, w_pa[0].astype(BF16), w_pb[0].astype(BF16), w_out[0].astype(BF16),
                    final_norm_w.reshape(1, D_MODEL))
    return out.reshape(1, SEQ, D_MODEL)
```

```python
import math

import jax
import jax.numpy as jnp
import numpy as np
from jax import lax
from jax.experimental import pallas as pl
from jax.experimental.pallas import tpu as pltpu

F32 = jnp.float32
BF16 = jnp.bfloat16

D_MODEL = 2048
SEQ = 8192
EPS = 1e-6
A_HEADS = 8
A_HEAD_DIM = 128
A_WIDTH = 1024
B_HEADS = 8
B_HEAD_DIM = 128
B_WIDTH = 1024
Q_RANK = 512
KV_RANK = 256
IDX_HEADS = 16
IDX_DIM = 128
TOPK = 256
CHUNK = 64
REL_BUCKETS = 32
REL_MAX_DIST = 128

P_AQ, P_AF, P_AI, P_AG = 0, 1024, 2048, 3072
P_CQ = 4096
P_MISC = 4608
P_BG = 5120
P_MA = 6144
P_MB = 8192
P_WIDTH = 10240
IN_WIDTH = 10128
IN_TN = 1024
IN_SHIFT = 112

VMEM_LIMIT = 60 * 1024 * 1024

QB = 256
KT = 256
NKT = SEQ // KT
KSUB = 128
CNT_ROWS = 64
SUM_ROWS = 16
ATT_TILES = 2
UPDATE_LAG = 2

SUB = 16
TB = 256

NEG_INF = float("-inf")
LOG2E = math.log2(math.e)
I16 = jnp.int16
I16_MIN = -32768
KEY_NEG_FLT_MAX = -2139095040
KEY_NEG_INF = -2139095041

NT_DIMS = (((1,), (1,)), ((), ()))
TN_DIMS = (((0,), (0,)), ((), ()))


def _t5_thresholds():
    half = REL_BUCKETS // 2
    max_exact = half // 2
    n = np.arange(1, 4 * REL_MAX_DIST, dtype=np.int64)
    large = max_exact + (np.log(np.maximum(n, 1).astype(np.float64) / max_exact)
                         / math.log(REL_MAX_DIST / max_exact) * (half - max_exact)).astype(np.int32)
    large = np.minimum(large, half - 1)
    b = np.where(n < max_exact, n, large)
    thr = [int(n[i + 1]) for i in np.nonzero(np.diff(b))[0] if n[i + 1] > max_exact]
    assert len(thr) == half - 1 - max_exact and b[-1] == half - 1 and thr[-1] < REL_MAX_DIST
    return max_exact, half, thr


T5_MAX_EXACT, T5_HALF, T5_THR = _t5_thresholds()


def _in_proj_kernel(x_ref, nw_ref, w_ref, o_ref, h_ref):
    @pl.when(pl.program_id(1) == 0)
    def _():
        x = x_ref[...]
        ms = jnp.mean(x * x, axis=-1, keepdims=True)
        h_ref[...] = (x * lax.rsqrt(ms + EPS) * nw_ref[...]).astype(BF16)

    j = pl.program_id(1)

    @pl.when(j < P_BG // IN_TN)
    def _():
        o_ref[...] = jnp.dot(h_ref[...], w_ref[:, :IN_TN], preferred_element_type=F32)

    @pl.when(j >= P_BG // IN_TN)
    def _():
        w = pltpu.roll(w_ref[...], IN_TN + 128 - (128 - IN_SHIFT), axis=1)[:, :IN_TN]
        o_ref[...] = jnp.dot(h_ref[...], w, preferred_element_type=F32)


def _in_proj(x2, norm_w, w_in_b):
    tm, tn = 1024, IN_TN
    n_plain = P_BG // tn

    def w_index(i, j):
        return 0, pl.multiple_of(jnp.where(j < n_plain, j * tn, j * tn - 128), 128)

    return pl.pallas_call(
        _in_proj_kernel,
        grid=(SEQ // tm, P_WIDTH // tn),
        in_specs=[
            pl.BlockSpec((tm, D_MODEL), lambda i, j: (i, 0)),
            pl.BlockSpec((1, D_MODEL), lambda i, j: (0, 0)),
            pl.BlockSpec((pl.Element(D_MODEL), pl.Element(tn + 128, padding=(0, P_WIDTH - IN_WIDTH))), w_index),
        ],
        out_specs=pl.BlockSpec((tm, tn), lambda i, j: (i, j)),
        out_shape=jax.ShapeDtypeStruct((SEQ, P_WIDTH), F32),
        scratch_shapes=[pltpu.VMEM((tm, D_MODEL), BF16)],
        compiler_params=pltpu.CompilerParams(
            dimension_semantics=("arbitrary", "arbitrary"), vmem_limit_bytes=VMEM_LIMIT),
        name="in_proj",
    )(x2, norm_w, w_in_b)


def _rms(x, w):
    return x * lax.rsqrt(jnp.mean(x * x, axis=-1, keepdims=True) + EPS) * w


def _proj2_kernel(cq_ref, misc_ref, qnw_ref, kvnw_ref, wuqt_ref, wqit_ref, wk_ref, wvt_ref, lnw_ref, lnb_ref,
                  qt_ref, qit_ref, k_ref, vt_ref, kidx_ref, wt_ref):
    cqn = _rms(cq_ref[...], qnw_ref[...]).astype(BF16)
    qt = lax.dot_general(wuqt_ref[...], cqn, NT_DIMS, preferred_element_type=F32)
    qt_ref[...] = (qt * (B_HEAD_DIM ** -0.5 * LOG2E)).astype(BF16)
    qit = lax.dot_general(wqit_ref[...], cqn, NT_DIMS, preferred_element_type=F32)
    for h in range(IDX_HEADS):
        qit_ref[h] = qit[h * IDX_DIM:(h + 1) * IDX_DIM, :].astype(BF16)
    misc = misc_ref[...]
    ckvn = _rms(misc[:, :KV_RANK], kvnw_ref[...]).astype(BF16)
    k_ref[...] = jnp.dot(ckvn, wk_ref[...], preferred_element_type=F32).astype(BF16)
    vt_ref[0] = lax.dot_general(wvt_ref[...], ckvn, NT_DIMS, preferred_element_type=F32).astype(BF16)
    kr = misc[:, KV_RANK:KV_RANK + IDX_DIM]
    mu = jnp.mean(kr, axis=-1, keepdims=True)
    var = jnp.mean(jnp.square(kr - mu), axis=-1, keepdims=True)
    kidx_ref[...] = ((kr - mu) * lax.rsqrt(var + EPS) * lnw_ref[...] + lnb_ref[...]).astype(BF16)
    wt_ref[...] = (misc[:, KV_RANK + IDX_DIM:] * (IDX_HEADS ** -0.5 * IDX_DIM ** -0.5)).T


def _proj2(p, q_norm_w, kv_norm_w, w_uq_t, w_qidx_t, w_k, w_v_t, ln_w, ln_b):
    tm = KT
    const = lambda i: (0, 0)
    return pl.pallas_call(
        _proj2_kernel,
        grid=(SEQ // tm,),
        in_specs=[
            pl.BlockSpec((tm, 512), lambda i: (i, P_CQ // 512)),
            pl.BlockSpec((tm, 512), lambda i: (i, P_MISC // 512)),
            pl.BlockSpec((1, Q_RANK), const),
            pl.BlockSpec((1, KV_RANK), const),
            pl.BlockSpec((B_WIDTH, Q_RANK), const),
            pl.BlockSpec((IDX_HEADS * IDX_DIM, Q_RANK), const),
            pl.BlockSpec((KV_RANK, B_WIDTH), const),
            pl.BlockSpec((B_WIDTH, KV_RANK), const),
            pl.BlockSpec((1, IDX_DIM), const),
            pl.BlockSpec((1, IDX_DIM), const),
        ],
        out_specs=[
            pl.BlockSpec((B_WIDTH, tm), lambda i: (0, i)),
            pl.BlockSpec((IDX_HEADS, IDX_DIM, tm), lambda i: (0, 0, i)),
            pl.BlockSpec((tm, B_WIDTH), lambda i: (i, 0)),
            pl.BlockSpec((1, B_WIDTH, tm), lambda i: (i, 0, 0)),
            pl.BlockSpec((tm, IDX_DIM), lambda i: (i, 0)),
            pl.BlockSpec((128, tm), lambda i: (0, i)),
        ],
        out_shape=[
            jax.ShapeDtypeStruct((B_WIDTH, SEQ), BF16),
            jax.ShapeDtypeStruct((IDX_HEADS, IDX_DIM, SEQ), BF16),
            jax.ShapeDtypeStruct((SEQ, B_WIDTH), BF16),
            jax.ShapeDtypeStruct((NKT, B_WIDTH, KT), BF16),
            jax.ShapeDtypeStruct((SEQ, IDX_DIM), BF16),
            jax.ShapeDtypeStruct((128, SEQ), F32),
        ],
        compiler_params=pltpu.CompilerParams(
            dimension_semantics=("arbitrary",), vmem_limit_bytes=VMEM_LIMIT),
        name="proj2",
    )(p, p, q_norm_w, kv_norm_w, w_uq_t, w_qidx_t, w_k, w_v_t, ln_w, ln_b)


def _split3(x):
    x1 = x.astype(BF16)
    r1 = x - x1.astype(F32)
    x2 = r1.astype(BF16)
    x3 = (r1 - x2.astype(F32)).astype(BF16)
    return x1, x2, x3


def _hgrn2_kernel(aq_ref, af_ref, ai_ref, ag_ref, lbt_ref, gn_ref, o_ref,
                  st_ref, sel_ref, a_ref, b_ref, kd_ref, v_ref, e_ref, oacc_ref):
    @pl.when(pl.program_id(0) == 0)
    def _():
        st_ref[...] = jnp.zeros_like(st_ref)
        r = lax.broadcasted_iota(jnp.int32, (TB, TB), 0)
        c = lax.broadcasted_iota(jnp.int32, (TB, TB), 1)
        same = (r // SUB) == (c // SUB)
        sel_ref[0:TB, :] = jnp.where(same & (c <= r), 1.0, 0.0).astype(BF16)
        sel_ref[TB:2 * TB, :] = jnp.where(same, 1.0, 0.0).astype(BF16)

    lbt = lbt_ref[...]
    mx = jnp.maximum(lbt[0:1], lbt[1:2])
    e0 = jnp.exp(lbt[0:1] - mx)
    e1 = jnp.exp(lbt[1:2] - mx)
    lb = e0 / (e0 + e1)

    f = lb + (1.0 - lb) * jax.nn.sigmoid(af_ref[...])
    g = jnp.log(f)
    sel = sel_ref[...]
    g1, g2, g3 = _split3(g)
    ct = (jnp.dot(sel, g1, preferred_element_type=F32) + jnp.dot(sel, g2, preferred_element_type=F32)
          + jnp.dot(sel, g3, preferred_element_type=F32))
    cum = ct[0:TB]
    tot = ct[TB:2 * TB]
    aq = aq_ref[...]
    q = aq * jax.nn.sigmoid(aq) * (A_HEAD_DIM ** -0.5)
    kk = 1.0 - f
    a_ref[...] = (q * jnp.exp(cum)).astype(BF16)
    b_ref[...] = (kk * jnp.exp(-cum)).astype(BF16)
    kd_ref[...] = (kk * jnp.exp(tot - cum)).astype(BF16)
    v_ref[...] = ai_ref[...].astype(BF16)
    e_ref[...] = jnp.exp(tot)

    tr = lax.broadcasted_iota(jnp.int32, (SUB, SUB), 0)
    tc = lax.broadcasted_iota(jnp.int32, (SUB, SUB), 1)
    causal = tc <= tr

    def chunk_body(c, carry):
        r0 = pl.multiple_of(c * SUB, SUB)
        heads = [slice(h * A_HEAD_DIM, (h + 1) * A_HEAD_DIM) for h in range(A_HEADS)]
        ah = [a_ref[pl.ds(r0, SUB), cs] for cs in heads]
        vh = [v_ref[pl.ds(r0, SUB), cs] for cs in heads]
        sc = [lax.dot_general(ah[h], b_ref[pl.ds(r0, SUB), heads[h]], NT_DIMS, preferred_element_type=F32)
              for h in range(A_HEADS)]
        inter = []
        for h in range(A_HEADS):
            st = st_ref[h]
            inter.append(lax.dot_general(ah[h], st.astype(BF16), NT_DIMS, preferred_element_type=F32))
            dst = lax.dot_general(vh[h], kd_ref[pl.ds(r0, SUB), heads[h]], TN_DIMS, preferred_element_type=F32)
            st_ref[h] = st * e_ref[pl.ds(r0, 1), heads[h]] + dst
        for h in range(A_HEADS):
            pm = jnp.where(causal, sc[h], 0.0).astype(BF16)
            oacc_ref[pl.ds(r0, SUB), heads[h]] = inter[h] + jnp.dot(pm, vh[h], preferred_element_type=F32)
        return carry

    lax.fori_loop(0, TB // SUB, chunk_body, 0, unroll=8)

    gn = gn_ref[...]
    ag = ag_ref[...]
    gate = ag * jax.nn.sigmoid(ag)
    for h in range(A_HEADS):
        cs = slice(h * A_HEAD_DIM, (h + 1) * A_HEAD_DIM)
        o = oacc_ref[:, cs]
        on = o * lax.rsqrt(jnp.mean(o * o, axis=-1, keepdims=True) + EPS) * gn
        o_ref[:, cs] = (on * gate[:, cs]).astype(BF16)


def _hgrn2(p, lb_table, gnorm):
    blk = lambda col: pl.BlockSpec((TB, A_WIDTH), lambda i: (i, col // A_WIDTH))
    return pl.pallas_call(
        _hgrn2_kernel,
        grid=(SEQ // TB,),
        in_specs=[blk(P_AQ), blk(P_AF), blk(P_AI), blk(P_AG),
                  pl.BlockSpec((2, A_WIDTH), lambda i: (0, 0)),
                  pl.BlockSpec((1, A_HEAD_DIM), lambda i: (0, 0))],
        out_specs=pl.BlockSpec((TB, A_WIDTH), lambda i: (i, 0)),
        out_shape=jax.ShapeDtypeStruct((SEQ, A_WIDTH), BF16),
        scratch_shapes=[
            pltpu.VMEM((A_HEADS, A_HEAD_DIM, A_HEAD_DIM), F32),
            pltpu.VMEM((2 * TB, TB), BF16),
            pltpu.VMEM((TB, A_WIDTH), BF16),
            pltpu.VMEM((TB, A_WIDTH), BF16),
            pltpu.VMEM((TB, A_WIDTH), BF16),
            pltpu.VMEM((TB, A_WIDTH), BF16),
            pltpu.VMEM((TB, A_WIDTH), F32),
            pltpu.VMEM((TB, A_WIDTH), F32),
        ],
        compiler_params=pltpu.CompilerParams(
            dimension_semantics=("arbitrary",), vmem_limit_bytes=VMEM_LIMIT),
        name="hgrn2",
    )(p, p, p, p, lb_table, gnorm)


def _t5_bucket(rel):
    n = jnp.abs(rel)
    large = jnp.full(rel.shape, T5_MAX_EXACT, jnp.int32)
    for t in T5_THR:
        large = large + jnp.where(n >= t, 1, 0)
    return jnp.where(rel > 0, T5_HALF, 0) + jnp.where(n < T5_MAX_EXACT, n, large)


def _dsa_kernel(relb_ref, qi_ref, w_ref, q_ref, bg_ref, kidx_ref, k_ref, v_ref, o_ref,
                hi_ref, lo_ref, b0_ref, b1_ref, thr_ref, m_ref, l_ref, acc_ref, s_ref):
    qb = pl.program_id(0)

    @pl.when(qb == 0)
    def _():
        def bias_tile(rel, h):
            bucket = _t5_bucket(rel)
            val = jnp.zeros(rel.shape, F32)
            for b in range(REL_BUCKETS):
                val = jnp.where(bucket == b, relb_ref[b, h], val)
            return (val - relb_ref[T5_HALF - 1, h]) * LOG2E

        r0 = lax.broadcasted_iota(jnp.int32, (KT, QB), 0)
        c0 = lax.broadcasted_iota(jnp.int32, (KT, QB), 1)
        r1 = lax.broadcasted_iota(jnp.int32, (128, 128), 0)
        c1 = lax.broadcasted_iota(jnp.int32, (128, 128), 1)
        for h in range(B_HEADS):
            b0_ref[h] = bias_tile(r0 - c0, h)
            b1_ref[h] = bias_tile(r1 - 128 - c1, h)

    def idx_tile(j, diag):
        for ks in range(KT // KSUB):
            kt = kidx_ref[pl.ds(pl.multiple_of(j * KT + ks * KSUB, KSUB), KSUB), :]
            acc = jnp.zeros((KSUB, QB), F32)
            for h in range(IDX_HEADS):
                r = jnp.dot(kt, qi_ref[h], preferred_element_type=F32)
                acc = acc + w_ref[h:h + 1, :] * jnp.maximum(r, 0.0)
            if diag:
                rk = ks * KSUB + lax.broadcasted_iota(jnp.int32, (KSUB, QB), 0)
                ct = lax.broadcasted_iota(jnp.int32, (KSUB, QB), 1)
                acc = jnp.where((rk // CHUNK) <= (ct // CHUNK), acc, NEG_INF)
            bits = lax.bitcast_convert_type(acc, jnp.int32)
            key = bits ^ ((bits >> 31) & 0x7FFFFFFF)
            hi_ref[j, ks * KSUB:(ks + 1) * KSUB, :] = (key >> 16).astype(I16)
            lo_ref[j, ks * KSUB:(ks + 1) * KSUB, :] = ((key & 0xFFFF) + I16_MIN).astype(I16)

    def idx_body(j, carry):
        idx_tile(j, False)
        return carry

    lax.fori_loop(0, qb, idx_body, 0)
    idx_tile(qb, True)

    npairs = (qb + 2) // 2

    @pl.when(qb % 2 == 0)
    def _():
        hi_ref[qb + 1] = jnp.full((KT, QB), KEY_NEG_INF >> 16, I16)
        lo_ref[qb + 1] = jnp.full((KT, QB), (KEY_NEG_INF & 0xFFFF) + I16_MIN, I16)

    one16 = jnp.ones((KT, QB), I16)
    zero16 = jnp.zeros((KT, QB), I16)

    def count(ref, cand, strict):
        c16 = cand.astype(I16)

        def body(jp, cnt):
            for u in range(2):
                x = ref[2 * jp + u]
                sel = jnp.where((x > c16) if strict else (x >= c16), one16, zero16)
                for r in range(KT // CNT_ROWS):
                    cnt = cnt + sel[r * CNT_ROWS:(r + 1) * CNT_ROWS]
            return cnt

        cnt = lax.fori_loop(0, npairs, body, jnp.zeros((CNT_ROWS, QB), I16))
        return jnp.sum(cnt.astype(jnp.int32), axis=0, keepdims=True)

    def bisect(ref, need):
        c0 = count(ref, jnp.zeros((1, QB), jnp.int32), False)
        t0 = jnp.where(c0 >= need, 0, I16_MIN).astype(jnp.int32)

        def bit_body(i, t):
            cand = t | jnp.left_shift(jnp.int32(1), 14 - i)
            return jnp.where(count(ref, cand, False) >= need, cand, t)

        return lax.fori_loop(0, 15, bit_body, t0)

    t_hi = bisect(hi_ref, jnp.full((1, QB), TOPK, jnp.int32))
    need_lo = TOPK - count(hi_ref, t_hi, True)
    t_hi16 = t_hi.astype(I16)

    def cand_body(j, carry):
        lo_ref[j] = jnp.where(hi_ref[j] == t_hi16, lo_ref[j], jnp.full((KT, QB), I16_MIN, I16))
        return carry

    lax.fori_loop(0, 2 * npairs, cand_body, 0)
    t_lo = bisect(lo_ref, need_lo)
    t_key = jnp.maximum((t_hi << 16) | (t_lo - I16_MIN), KEY_NEG_FLT_MAX)
    thr_ref[0:1, :] = t_key >> 16
    thr_ref[1:2, :] = (t_key & 0xFFFF) + I16_MIN

    m_ref[...] = jnp.full(m_ref.shape, NEG_INF, F32)
    l_ref[...] = jnp.zeros_like(l_ref)
    acc_ref[...] = jnp.zeros_like(acc_ref)
    ones_rows = jnp.ones((SUM_ROWS, KT), BF16)

    def att_step(tiles):
        t_hi16 = thr_ref[0:1, :].astype(I16)
        t_lo16 = thr_ref[1:2, :].astype(I16)
        masks = []
        for j, _ in tiles:
            hi = hi_ref[j]
            keep = (hi > t_hi16) | ((hi == t_hi16) & (lo_ref[j] >= t_lo16))
            flag = jnp.where(keep, jnp.zeros((KT, QB), I16), jnp.full((KT, QB), -1, I16))
            masks.append(jnp.where(flag.astype(jnp.int32) < 0, NEG_INF, 0.0))

        def qk(h, j):
            hs = slice(h * B_HEAD_DIM, (h + 1) * B_HEAD_DIM)
            return jnp.dot(k_ref[pl.ds(pl.multiple_of(j * KT, KT), KT), hs], q_ref[hs, :],
                           preferred_element_type=F32)

        m_safe, alpha = [], []
        for h in range(B_HEADS):
            m_prev = m_ref[h]
            m_next = m_prev
            for ti, ((j, kind), mask) in enumerate(zip(tiles, masks)):
                s = qk(h, j)
                if kind == "diag":
                    s = s + b0_ref[h]
                elif kind == "prev":
                    lo = jnp.concatenate([s[128:, :128] + b1_ref[h], s[128:, 128:]], axis=1)
                    s = jnp.concatenate([s[:128, :], lo], axis=0)
                s = s + mask
                s_ref[h, ti * KT:(ti + 1) * KT, :] = s
                m_next = jnp.maximum(m_next, jnp.max(s, axis=0, keepdims=True))
            m_safe.append(jnp.where(m_next == NEG_INF, 0.0, m_next))
            alpha.append(jnp.exp2(m_prev - m_safe[h]))
            m_ref[h] = m_next

        def update(h, pv):
            l_ref[h] = alpha[h] * l_ref[h] + pv[B_HEAD_DIM:B_HEAD_DIM + 1, :]
            acc_ref[h] = alpha[h] * acc_ref[h] + pv[:B_HEAD_DIM, :]

        pvs = []
        for h in range(B_HEADS):
            hs = slice(h * B_HEAD_DIM, (h + 1) * B_HEAD_DIM)
            pv = None
            for ti, (j, _) in enumerate(tiles):
                p = jnp.exp2(s_ref[h, ti * KT:(ti + 1) * KT, :] - m_safe[h]).astype(BF16)
                d = jnp.dot(jnp.concatenate([v_ref[j, hs, :], ones_rows], axis=0), p,
                            preferred_element_type=F32)
                pv = d if pv is None else pv + d
            pvs.append(pv)
            if h >= UPDATE_LAG:
                update(h - UPDATE_LAG, pvs[h - UPDATE_LAG])
        for h in range(B_HEADS - UPDATE_LAG, B_HEADS):
            update(h, pvs[h])

    nfar = jnp.maximum(qb - 1, 0)

    def att_body(i, carry):
        att_step([(2 * i, "far"), (2 * i + 1, "far")])
        return carry

    lax.fori_loop(0, nfar // 2, att_body, 0)

    @pl.when(nfar % 2 == 1)
    def _():
        att_step([(qb - 2, "far")])

    @pl.when(qb == 0)
    def _():
        att_step([(qb, "diag")])

    @pl.when(qb >= 1)
    def _():
        att_step([(qb - 1, "prev"), (qb, "diag")])

    for h in range(B_HEADS):
        hs = slice(h * B_HEAD_DIM, (h + 1) * B_HEAD_DIM)
        bg = bg_ref[:, hs]
        o_ref[:, hs] = ((acc_ref[h] / l_ref[h]).T * (bg * jax.nn.sigmoid(bg))).astype(BF16)


def _dsa(rel_bias, q_idx_t, w_idx_t, q_t, p, k_idx, k, v_t):
    single = pl.Buffered(1)
    return pl.pallas_call(
        _dsa_kernel,
        grid=(SEQ // QB,),
        in_specs=[
            pl.BlockSpec(memory_space=pltpu.SMEM),
            pl.BlockSpec((IDX_HEADS, IDX_DIM, QB), lambda i: (0, 0, i)),
            pl.BlockSpec((128, QB), lambda i: (0, i)),
            pl.BlockSpec((B_WIDTH, QB), lambda i: (0, i)),
            pl.BlockSpec((QB, B_WIDTH), lambda i: (i, P_BG // B_WIDTH)),
            pl.BlockSpec((SEQ, IDX_DIM), lambda i: (0, 0), pipeline_mode=single),
            pl.BlockSpec((SEQ, B_WIDTH), lambda i: (0, 0), pipeline_mode=single),
            pl.BlockSpec((NKT, B_WIDTH, KT), lambda i: (0, 0, 0), pipeline_mode=single),
        ],
        out_specs=pl.BlockSpec((QB, B_WIDTH), lambda i: (i, 0)),
        out_shape=jax.ShapeDtypeStruct((SEQ, B_WIDTH), BF16),
        scratch_shapes=[
            pltpu.VMEM((NKT, KT, QB), I16),
            pltpu.VMEM((NKT, KT, QB), I16),
            pltpu.VMEM((B_HEADS, KT, QB), F32),
            pltpu.VMEM((B_HEADS, 128, 128), F32),
            pltpu.VMEM((8, QB), jnp.int32),
            pltpu.VMEM((B_HEADS, 1, QB), F32),
            pltpu.VMEM((B_HEADS, 1, QB), F32),
            pltpu.VMEM((B_HEADS, B_HEAD_DIM, QB), F32),
            pltpu.VMEM((B_HEADS, ATT_TILES * KT, QB), F32),
        ],
        compiler_params=pltpu.CompilerParams(
            dimension_semantics=("arbitrary",), vmem_limit_bytes=VMEM_LIMIT),
        name="dsa",
    )(rel_bias, q_idx_t, w_idx_t, q_t, p, k_idx, k, v_t)


def _out_proj_kernel(ya_ref, yb_ref, ma_ref, mb_ref, x_ref, wpa_ref, wpb_ref, wo_ref, fnw_ref, o_ref):
    pa = jnp.dot(ya_ref[...], wpa_ref[...], preferred_element_type=F32)
    pb = jnp.dot(yb_ref[...], wpb_ref[...], preferred_element_type=F32)
    merged = jax.nn.sigmoid(ma_ref[...]) * pa + jax.nn.sigmoid(mb_ref[...]) * pb
    y = x_ref[...] + jnp.dot(merged.astype(BF16), wo_ref[...], preferred_element_type=F32)
    o_ref[...] = _rms(y, fnw_ref[...])


def _out_proj(y_a, y_b, p, x2, w_pa, w_pb, w_out, final_norm_w):
    tm = 256
    const = lambda i: (0, 0)
    single = pl.Buffered(1)
    return pl.pallas_call(
        _out_proj_kernel,
        grid=(SEQ // tm,),
        in_specs=[
            pl.BlockSpec((tm, A_WIDTH), lambda i: (i, 0)),
            pl.BlockSpec((tm, B_WIDTH), lambda i: (i, 0)),
            pl.BlockSpec((tm, D_MODEL), lambda i: (i, P_MA // D_MODEL)),
            pl.BlockSpec((tm, D_MODEL), lambda i: (i, P_MB // D_MODEL)),
            pl.BlockSpec((tm, D_MODEL), lambda i: (i, 0)),
            pl.BlockSpec((A_WIDTH, D_MODEL), const, pipeline_mode=single),
            pl.BlockSpec((B_WIDTH, D_MODEL), const, pipeline_mode=single),
            pl.BlockSpec((D_MODEL, D_MODEL), const, pipeline_mode=single),
            pl.BlockSpec((1, D_MODEL), const),
        ],
        out_specs=pl.BlockSpec((tm, D_MODEL), lambda i: (i, 0)),
        out_shape=jax.ShapeDtypeStruct((SEQ, D_MODEL), F32),
        compiler_params=pltpu.CompilerParams(
            dimension_semantics=("arbitrary",), vmem_limit_bytes=VMEM_LIMIT),
        name="out_proj",
    )(y_a, y_b, p, p, x2, w_pa, w_pb, w_out, final_norm_w)


def kernel(x, norm_w, w_in, lb_table, gnorm_a, q_norm_w, kv_norm_w, w_uq, w_qidx, w_ukv, kidx_norm_w,
           kidx_norm_b, w_pa, w_pb, w_out, rel_bias, final_norm_w):
    assert x.shape == (1, SEQ, D_MODEL) and w_in.shape[0] == 1
    x2 = x.reshape(SEQ, D_MODEL)
    p = _in_proj(x2, norm_w, w_in[0].astype(BF16))
    w_kv = w_ukv[0].astype(BF16).reshape(KV_RANK, B_HEADS, 2, B_HEAD_DIM)
    w_k = w_kv[:, :, 0, :].reshape(KV_RANK, B_WIDTH)
    w_v_t = w_kv[:, :, 1, :].reshape(KV_RANK, B_WIDTH).T
    q_t, q_idx_t, k, v_t, k_idx, w_idx_t = _proj2(
        p, q_norm_w, kv_norm_w, w_uq[0].astype(BF16).T, w_qidx[0].astype(BF16).T, w_k, w_v_t,
        kidx_norm_w, kidx_norm_b)
    y_a = _hgrn2(p, lb_table, gnorm_a)
    y_b = _dsa(rel_bias, q_idx_t, w_idx_t, q_t, p, k_idx, k, v_t)
    out = _out_proj(y_a, y_b, p, x2, w_pa[0].astype(BF16), w_pb[0].astype(BF16), w_out[0].astype(BF16),
                    final_norm_w.reshape(1, D_MODEL))
    return out.reshape(1, SEQ, D_MODEL)
```

```python
import math

import jax
import jax.numpy as jnp
import numpy as np
from jax import lax
from jax.experimental import pallas as pl
from jax.experimental.pallas import tpu as pltpu

F32 = jnp.float32
BF16 = jnp.bfloat16

D_MODEL = 2048
SEQ = 8192
EPS = 1e-6
A_HEADS = 8
A_HEAD_DIM = 128
A_WIDTH = 1024
B_HEADS = 8
B_HEAD_DIM = 128
B_WIDTH = 1024
Q_RANK = 512
KV_RANK = 256
IDX_HEADS = 16
IDX_DIM = 128
TOPK = 256
CHUNK = 64
REL_BUCKETS = 32
REL_MAX_DIST = 128

P_AQ, P_AF, P_AI, P_AG = 0, 1024, 2048, 3072
P_CQ = 4096
P_MISC = 4608
P_BG = 5120
P_MA = 6144
P_MB = 8192
P_WIDTH = 10240
IN_WIDTH = 10128
IN_TN = 1024
IN_SHIFT = 112

VMEM_LIMIT = 63 * 1024 * 1024

QB = 256
KT = 256
NKT = SEQ // KT
KSUB = 128
CNT_ROWS = 64
SUM_ROWS = 16
ATT_TILES = 2
UPDATE_LAG = 2

SUB = 32
MID = SUB // 2
TB = 256

NEG_INF = float("-inf")
LOG2E = math.log2(math.e)
I16_MIN = -32768
I32_MIN = -2147483648
KEY16_NEG_INF = -32641
FINE_BELOW = 1 << 16
FINE_BITS = 17

NT_DIMS = (((1,), (1,)), ((), ()))
TN_DIMS = (((0,), (0,)), ((), ()))


def _t5_thresholds():
    half = REL_BUCKETS // 2
    max_exact = half // 2
    n = np.arange(1, 4 * REL_MAX_DIST, dtype=np.int64)
    large = max_exact + (np.log(np.maximum(n, 1).astype(np.float64) / max_exact)
                         / math.log(REL_MAX_DIST / max_exact) * (half - max_exact)).astype(np.int32)
    large = np.minimum(large, half - 1)
    b = np.where(n < max_exact, n, large)
    thr = [int(n[i + 1]) for i in np.nonzero(np.diff(b))[0] if n[i + 1] > max_exact]
    assert len(thr) == half - 1 - max_exact and b[-1] == half - 1 and thr[-1] < REL_MAX_DIST
    return max_exact, half, thr


T5_MAX_EXACT, T5_HALF, T5_THR = _t5_thresholds()


def _in_proj_kernel(x_ref, nw_ref, w_ref, o_ref, h_ref):
    @pl.when(pl.program_id(1) == 0)
    def _():
        x = x_ref[...]
        ms = jnp.mean(x * x, axis=-1, keepdims=True)
        h_ref[...] = (x * lax.rsqrt(ms + EPS) * nw_ref[...]).astype(BF16)

    j = pl.program_id(1)

    @pl.when(j < P_BG // IN_TN)
    def _():
        o_ref[...] = jnp.dot(h_ref[...], w_ref[:, :IN_TN], preferred_element_type=F32)

    @pl.when(j >= P_BG // IN_TN)
    def _():
        w = pltpu.roll(w_ref[...], IN_TN + 128 - (128 - IN_SHIFT), axis=1)[:, :IN_TN]
        o_ref[...] = jnp.dot(h_ref[...], w, preferred_element_type=F32)


def _in_proj(x2, norm_w, w_in_b):
    tm, tn = 1024, IN_TN
    n_plain = P_BG // tn

    def w_index(i, j):
        return 0, pl.multiple_of(jnp.where(j < n_plain, j * tn, j * tn - 128), 128)

    return pl.pallas_call(
        _in_proj_kernel,
        grid=(SEQ // tm, P_WIDTH // tn),
        in_specs=[
            pl.BlockSpec((tm, D_MODEL), lambda i, j: (i, 0)),
            pl.BlockSpec((1, D_MODEL), lambda i, j: (0, 0)),
            pl.BlockSpec((pl.Element(D_MODEL), pl.Element(tn + 128, padding=(0, P_WIDTH - IN_WIDTH))), w_index),
        ],
        out_specs=pl.BlockSpec((tm, tn), lambda i, j: (i, j)),
        out_shape=jax.ShapeDtypeStruct((SEQ, P_WIDTH), F32),
        scratch_shapes=[pltpu.VMEM((tm, D_MODEL), BF16)],
        compiler_params=pltpu.CompilerParams(
            dimension_semantics=("arbitrary", "arbitrary"), vmem_limit_bytes=VMEM_LIMIT),
        name="in_proj",
    )(x2, norm_w, w_in_b)


def _rms(x, w):
    return x * lax.rsqrt(jnp.mean(x * x, axis=-1, keepdims=True) + EPS) * w


def _proj2_kernel(cq_ref, misc_ref, qnw_ref, kvnw_ref, wuqt_ref, wqit_ref, wk_ref, wvt_ref, lnw_ref, lnb_ref,
                  qt_ref, qit_ref, k_ref, vt_ref, kidx_ref, wt_ref):
    cqn = _rms(cq_ref[...], qnw_ref[...]).astype(BF16)
    qt = lax.dot_general(wuqt_ref[...], cqn, NT_DIMS, preferred_element_type=F32)
    qt_ref[...] = (qt * (B_HEAD_DIM ** -0.5 * LOG2E)).astype(BF16)
    qit = lax.dot_general(wqit_ref[...], cqn, NT_DIMS, preferred_element_type=F32)
    for h in range(IDX_HEADS):
        qit_ref[h] = qit[h * IDX_DIM:(h + 1) * IDX_DIM, :].astype(BF16)
    misc = misc_ref[...]
    ckvn = _rms(misc[:, :KV_RANK], kvnw_ref[...]).astype(BF16)
    k_ref[...] = jnp.dot(ckvn, wk_ref[...], preferred_element_type=F32).astype(BF16)
    vt_ref[0] = lax.dot_general(wvt_ref[...], ckvn, NT_DIMS, preferred_element_type=F32).astype(BF16)
    kr = misc[:, KV_RANK:KV_RANK + IDX_DIM]
    mu = jnp.mean(kr, axis=-1, keepdims=True)
    var = jnp.mean(jnp.square(kr - mu), axis=-1, keepdims=True)
    kidx_ref[...] = ((kr - mu) * lax.rsqrt(var + EPS) * lnw_ref[...] + lnb_ref[...]).astype(BF16)
    wt_ref[...] = (misc[:, KV_RANK + IDX_DIM:] * (IDX_HEADS ** -0.5 * IDX_DIM ** -0.5)).T


def _proj2(p, q_norm_w, kv_norm_w, w_uq_t, w_qidx_t, w_k, w_v_t, ln_w, ln_b):
    tm = KT
    const = lambda i: (0, 0)
    return pl.pallas_call(
        _proj2_kernel,
        grid=(SEQ // tm,),
        in_specs=[
            pl.BlockSpec((tm, 512), lambda i: (i, P_CQ // 512)),
            pl.BlockSpec((tm, 512), lambda i: (i, P_MISC // 512)),
            pl.BlockSpec((1, Q_RANK), const),
            pl.BlockSpec((1, KV_RANK), const),
            pl.BlockSpec((B_WIDTH, Q_RANK), const),
            pl.BlockSpec((IDX_HEADS * IDX_DIM, Q_RANK), const),
            pl.BlockSpec((KV_RANK, B_WIDTH), const),
            pl.BlockSpec((B_WIDTH, KV_RANK), const),
            pl.BlockSpec((1, IDX_DIM), const),
            pl.BlockSpec((1, IDX_DIM), const),
        ],
        out_specs=[
            pl.BlockSpec((B_WIDTH, tm), lambda i: (0, i)),
            pl.BlockSpec((IDX_HEADS, IDX_DIM, tm), lambda i: (0, 0, i)),
            pl.BlockSpec((tm, B_WIDTH), lambda i: (i, 0)),
            pl.BlockSpec((1, B_WIDTH, tm), lambda i: (i, 0, 0)),
            pl.BlockSpec((tm, IDX_DIM), lambda i: (i, 0)),
            pl.BlockSpec((128, tm), lambda i: (0, i)),
        ],
        out_shape=[
            jax.ShapeDtypeStruct((B_WIDTH, SEQ), BF16),
            jax.ShapeDtypeStruct((IDX_HEADS, IDX_DIM, SEQ), BF16),
            jax.ShapeDtypeStruct((SEQ, B_WIDTH), BF16),
            jax.ShapeDtypeStruct((NKT, B_WIDTH, KT), BF16),
            jax.ShapeDtypeStruct((SEQ, IDX_DIM), BF16),
            jax.ShapeDtypeStruct((128, SEQ), F32),
        ],
        compiler_params=pltpu.CompilerParams(
            dimension_semantics=("arbitrary",), vmem_limit_bytes=VMEM_LIMIT),
        name="proj2",
    )(p, p, q_norm_w, kv_norm_w, w_uq_t, w_qidx_t, w_k, w_v_t, ln_w, ln_b)


def _split3(x):
    x1 = x.astype(BF16)
    r1 = x - x1.astype(F32)
    x2 = r1.astype(BF16)
    x3 = (r1 - x2.astype(F32)).astype(BF16)
    return x1, x2, x3


def _hgrn2_kernel(aq_ref, af_ref, ai_ref, ag_ref, lbt_ref, gn_ref, o_ref,
                  st_ref, sel_ref, a_ref, a2_ref, b2_ref, kd_ref, v_ref, e_ref, oacc_ref):
    @pl.when(pl.program_id(0) == 0)
    def _():
        st_ref[...] = jnp.zeros_like(st_ref)
        r = lax.broadcasted_iota(jnp.int32, (TB, TB), 0)
        c = lax.broadcasted_iota(jnp.int32, (TB, TB), 1)
        same = (r // SUB) == (c // SUB)
        sel_ref[...] = jnp.where(same & (c <= r), 1.0, 0.0).astype(BF16)

    lbt = lbt_ref[...]
    mx = jnp.maximum(lbt[0:1], lbt[1:2])
    e0 = jnp.exp(lbt[0:1] - mx)
    e1 = jnp.exp(lbt[1:2] - mx)
    lb = e0 / (e0 + e1)

    f = lb + (1.0 - lb) * jax.nn.sigmoid(af_ref[...])
    g = jnp.log(f)
    sel = sel_ref[...]
    g1, g2, g3 = _split3(g)
    cum = (jnp.dot(sel, g1, preferred_element_type=F32) + jnp.dot(sel, g2, preferred_element_type=F32)
           + jnp.dot(sel, g3, preferred_element_type=F32))
    cum3 = cum.reshape(TB // SUB, SUB, A_WIDTH)
    mid = jnp.broadcast_to(cum3[:, MID - 1:MID, :], cum3.shape).reshape(TB, A_WIDTH)
    tot = jnp.broadcast_to(cum3[:, SUB - 1:SUB, :], cum3.shape).reshape(TB, A_WIDTH)
    aq = aq_ref[...]
    q = aq * jax.nn.sigmoid(aq) * (A_HEAD_DIM ** -0.5)
    kk = 1.0 - f
    a_ref[...] = (q * jnp.exp(cum)).astype(BF16)
    a2_ref[...] = (q * jnp.exp(cum - mid)).astype(BF16)
    b2_ref[...] = (kk * jnp.exp(mid - cum)).astype(BF16)
    kd_ref[...] = (kk * jnp.exp(tot - cum)).astype(BF16)
    v_ref[...] = ai_ref[...].astype(BF16)
    e_ref[...] = jnp.exp(tot)

    tr = lax.broadcasted_iota(jnp.int32, (SUB, SUB), 0)
    tc = lax.broadcasted_iota(jnp.int32, (SUB, SUB), 1)
    causal = tc <= tr

    def chunk_body(c, carry):
        r0 = pl.multiple_of(c * SUB, SUB)
        heads = [slice(h * A_HEAD_DIM, (h + 1) * A_HEAD_DIM) for h in range(A_HEADS)]
        ah = [a_ref[pl.ds(r0, SUB), cs] for cs in heads]
        vh = [v_ref[pl.ds(r0, SUB), cs] for cs in heads]
        sc = [lax.dot_general(a2_ref[pl.ds(r0, SUB), heads[h]], b2_ref[pl.ds(r0, SUB), heads[h]], NT_DIMS,
                              preferred_element_type=F32) for h in range(A_HEADS)]
        inter = []
        for h in range(A_HEADS):
            st = st_ref[h]
            inter.append(lax.dot_general(ah[h], st.astype(BF16), NT_DIMS, preferred_element_type=F32))
            dst = lax.dot_general(vh[h], kd_ref[pl.ds(r0, SUB), heads[h]], TN_DIMS, preferred_element_type=F32)
            st_ref[h] = st * e_ref[pl.ds(r0, 1), heads[h]] + dst
        for h in range(A_HEADS):
            pm = jnp.where(causal, sc[h], 0.0).astype(BF16)
            oacc_ref[pl.ds(r0, SUB), heads[h]] = inter[h] + jnp.dot(pm, vh[h], preferred_element_type=F32)
        return carry

    lax.fori_loop(0, TB // SUB, chunk_body, 0, unroll=True)

    gn = gn_ref[...]
    ag = ag_ref[...]
    gate = ag * jax.nn.sigmoid(ag)
    for h in range(A_HEADS):
        cs = slice(h * A_HEAD_DIM, (h + 1) * A_HEAD_DIM)
        o = oacc_ref[:, cs]
        on = o * lax.rsqrt(jnp.mean(o * o, axis=-1, keepdims=True) + EPS) * gn
        o_ref[:, cs] = (on * gate[:, cs]).astype(BF16)


def _hgrn2(p, lb_table, gnorm):
    blk = lambda col: pl.BlockSpec((TB, A_WIDTH), lambda i: (i, col // A_WIDTH))
    return pl.pallas_call(
        _hgrn2_kernel,
        grid=(SEQ // TB,),
        in_specs=[blk(P_AQ), blk(P_AF), blk(P_AI), blk(P_AG),
                  pl.BlockSpec((2, A_WIDTH), lambda i: (0, 0)),
                  pl.BlockSpec((1, A_HEAD_DIM), lambda i: (0, 0))],
        out_specs=pl.BlockSpec((TB, A_WIDTH), lambda i: (i, 0)),
        out_shape=jax.ShapeDtypeStruct((SEQ, A_WIDTH), BF16),
        scratch_shapes=[
            pltpu.VMEM((A_HEADS, A_HEAD_DIM, A_HEAD_DIM), F32),
            pltpu.VMEM((TB, TB), BF16),
            pltpu.VMEM((TB, A_WIDTH), BF16),
            pltpu.VMEM((TB, A_WIDTH), BF16),
            pltpu.VMEM((TB, A_WIDTH), BF16),
            pltpu.VMEM((TB, A_WIDTH), BF16),
            pltpu.VMEM((TB, A_WIDTH), BF16),
            pltpu.VMEM((TB, A_WIDTH), F32),
            pltpu.VMEM((TB, A_WIDTH), F32),
        ],
        compiler_params=pltpu.CompilerParams(
            dimension_semantics=("arbitrary",), vmem_limit_bytes=VMEM_LIMIT),
        name="hgrn2",
    )(p, p, p, p, lb_table, gnorm)


def _t5_bucket(rel):
    n = jnp.abs(rel)
    large = jnp.full(rel.shape, T5_MAX_EXACT, jnp.int32)
    for t in T5_THR:
        large = large + jnp.where(n >= t, 1, 0)
    return jnp.where(rel > 0, T5_HALF, 0) + jnp.where(n < T5_MAX_EXACT, n, large)


def _dsa_kernel(relb_ref, qi_ref, w_ref, q_ref, bg_ref, kidx_ref, k_ref, v_ref, o_ref,
                sc_ref, hi_ref, b0_ref, b1_ref, thr_ref, m_ref, l_ref, acc_ref, s_ref):
    qb = pl.program_id(0)

    @pl.when(qb == 0)
    def _():
        def bias_tile(rel, h):
            bucket = _t5_bucket(rel)
            val = jnp.zeros(rel.shape, F32)
            for b in range(REL_BUCKETS):
                val = jnp.where(bucket == b, relb_ref[b, h], val)
            return (val - relb_ref[T5_HALF - 1, h]) * LOG2E

        r0 = lax.broadcasted_iota(jnp.int32, (KT, QB), 0)
        c0 = lax.broadcasted_iota(jnp.int32, (KT, QB), 1)
        r1 = lax.broadcasted_iota(jnp.int32, (128, 128), 0)
        c1 = lax.broadcasted_iota(jnp.int32, (128, 128), 1)
        for h in range(B_HEADS):
            b0_ref[h] = bias_tile(r0 - c0, h)
            b1_ref[h] = bias_tile(r1 - 128 - c1, h)

    def idx_tile(j, diag):
        for ks in range(KT // KSUB):
            kt = kidx_ref[pl.ds(pl.multiple_of(j * KT + ks * KSUB, KSUB), KSUB), :]
            acc = jnp.zeros((KSUB, QB), F32)
            for h in range(IDX_HEADS):
                r = jnp.dot(kt, qi_ref[h], preferred_element_type=F32)
                acc = acc + w_ref[h:h + 1, :] * jnp.maximum(r, 0.0)
            if diag:
                rk = ks * KSUB + lax.broadcasted_iota(jnp.int32, (KSUB, QB), 0)
                ct = lax.broadcasted_iota(jnp.int32, (KSUB, QB), 1)
                acc = jnp.where((rk // CHUNK) <= (ct // CHUNK), acc, NEG_INF)
            sc_ref[j, ks * KSUB:(ks + 1) * KSUB, :] = acc
            hi_ref[j, ks * KSUB:(ks + 1) * KSUB, :] = acc.astype(BF16)

    def idx_body(i, carry):
        idx_tile(2 * i, False)
        idx_tile(2 * i + 1, False)
        return carry

    lax.fori_loop(0, qb // 2, idx_body, 0)

    @pl.when(qb % 2 == 1)
    def _():
        idx_tile(qb - 1, False)
        idx_tile(qb, True)

    @pl.when(qb % 2 == 0)
    def _():
        idx_tile(qb, True)

    npairs = (qb + 2) // 2

    @pl.when(qb % 2 == 0)
    def _():
        sc_ref[qb + 1] = jnp.full((KT, QB), NEG_INF, F32)
        hi_ref[qb + 1] = jnp.full((KT, QB), NEG_INF, BF16)

    def count(ref, thr):
        one = jnp.ones((KT, QB), ref.dtype)
        zero = jnp.zeros((KT, QB), ref.dtype)

        def body(jp, cnt):
            for u in range(2):
                sel = jnp.where(ref[2 * jp + u] >= thr, one, zero)
                for r in range(KT // CNT_ROWS):
                    cnt = cnt + sel[r * CNT_ROWS:(r + 1) * CNT_ROWS]
            return cnt

        cnt = lax.fori_loop(0, npairs, body, jnp.zeros((CNT_ROWS, QB), ref.dtype))
        return jnp.sum(cnt.astype(F32), axis=0, keepdims=True)

    def key16_to_bf16(k):
        bits = jnp.where(k >= 0, k, k ^ 0x7FFF).astype(jnp.int16)
        return lax.bitcast_convert_type(bits, BF16)

    def key32_to_f32(k):
        return lax.bitcast_convert_type(jnp.where(k >= 0, k, k ^ 0x7FFFFFFF), F32)

    c0 = count(hi_ref, jnp.zeros((1, QB), BF16))
    k16 = jnp.where(c0 >= float(TOPK), 0, I16_MIN).astype(jnp.int32)

    def coarse_body(i, k):
        cand = k | jnp.left_shift(jnp.int32(1), 14 - i)
        return jnp.where(count(hi_ref, key16_to_bf16(cand)) >= float(TOPK), cand, k)

    k16 = jnp.maximum(lax.fori_loop(0, 15, coarse_body, k16), KEY16_NEG_INF)
    k_mid = jnp.where(k16 >= 0, k16 << 16, (k16 << 16) | 0xFFFF)
    k_base = jnp.maximum(k_mid, I32_MIN + FINE_BELOW) - FINE_BELOW

    def fine_body(i, off):
        cand = off | jnp.left_shift(jnp.int32(1), FINE_BITS - 1 - i)
        return jnp.where(count(sc_ref, key32_to_f32(k_base + cand)) >= float(TOPK), cand, off)

    off = lax.fori_loop(0, FINE_BITS, fine_body, jnp.zeros((1, QB), jnp.int32))
    thr_ref[...] = key32_to_f32(k_base + off)

    m_ref[...] = jnp.full(m_ref.shape, NEG_INF, F32)
    l_ref[...] = jnp.zeros_like(l_ref)
    acc_ref[...] = jnp.zeros_like(acc_ref)
    ones_rows = jnp.ones((SUM_ROWS, KT), BF16)

    def att_step(tiles):
        masks = []
        for j, _ in tiles:
            s_t = sc_ref[j]
            masks.append(jnp.where((s_t >= thr_ref[...]) & (s_t > NEG_INF), 0.0, NEG_INF))

        def qk(h, j):
            hs = slice(h * B_HEAD_DIM, (h + 1) * B_HEAD_DIM)
            return jnp.dot(k_ref[pl.ds(pl.multiple_of(j * KT, KT), KT), hs], q_ref[hs, :],
                           preferred_element_type=F32)

        m_safe, alpha = [], []
        for h in range(B_HEADS):
            m_prev = m_ref[h]
            m_next = m_prev
            for ti, ((j, kind), mask) in enumerate(zip(tiles, masks)):
                s = qk(h, j)
                if kind == "diag":
                    s = s + b0_ref[h]
                elif kind == "prev":
                    lo = jnp.concatenate([s[128:, :128] + b1_ref[h], s[128:, 128:]], axis=1)
                    s = jnp.concatenate([s[:128, :], lo], axis=0)
                s = s + mask
                s_ref[h, ti * KT:(ti + 1) * KT, :] = s
                m_next = jnp.maximum(m_next, jnp.max(s, axis=0, keepdims=True))
            m_safe.append(jnp.where(m_next == NEG_INF, 0.0, m_next))
            alpha.append(jnp.exp2(m_prev - m_safe[h]))
            m_ref[h] = m_next

        def update(h, pv):
            l_ref[h] = alpha[h] * l_ref[h] + pv[B_HEAD_DIM:B_HEAD_DIM + 1, :]
            acc_ref[h] = alpha[h] * acc_ref[h] + pv[:B_HEAD_DIM, :]

        pvs = []
        for h in range(B_HEADS):
            hs = slice(h * B_HEAD_DIM, (h + 1) * B_HEAD_DIM)
            pv = None
            for ti, (j, _) in enumerate(tiles):
                p = jnp.exp2(s_ref[h, ti * KT:(ti + 1) * KT, :] - m_safe[h]).astype(BF16)
                d = jnp.dot(jnp.concatenate([v_ref[j, hs, :], ones_rows], axis=0), p,
                            preferred_element_type=F32)
                pv = d if pv is None else pv + d
            pvs.append(pv)
            if h >= UPDATE_LAG:
                update(h - UPDATE_LAG, pvs[h - UPDATE_LAG])
        for h in range(B_HEADS - UPDATE_LAG, B_HEADS):
            update(h, pvs[h])

    nfar = jnp.maximum(qb - 1, 0)

    def att_body(i, carry):
        att_step([(2 * i, "far"), (2 * i + 1, "far")])
        return carry

    lax.fori_loop(0, nfar // 2, att_body, 0)

    @pl.when(nfar % 2 == 1)
    def _():
        att_step([(qb - 2, "far")])

    @pl.when(qb == 0)
    def _():
        att_step([(qb, "diag")])

    @pl.when(qb >= 1)
    def _():
        att_step([(qb - 1, "prev"), (qb, "diag")])

    for h in range(B_HEADS):
        hs = slice(h * B_HEAD_DIM, (h + 1) * B_HEAD_DIM)
        bg = bg_ref[:, hs]
        o_ref[:, hs] = ((acc_ref[h] / l_ref[h]).T * (bg * jax.nn.sigmoid(bg))).astype(BF16)


def _dsa(rel_bias, q_idx_t, w_idx_t, q_t, p, k_idx, k, v_t):
    single = pl.Buffered(1)
    return pl.pallas_call(
        _dsa_kernel,
        grid=(SEQ // QB,),
        in_specs=[
            pl.BlockSpec(memory_space=pltpu.SMEM),
            pl.BlockSpec((IDX_HEADS, IDX_DIM, QB), lambda i: (0, 0, i)),
            pl.BlockSpec((128, QB), lambda i: (0, i)),
            pl.BlockSpec((B_WIDTH, QB), lambda i: (0, i)),
            pl.BlockSpec((QB, B_WIDTH), lambda i: (i, P_BG // B_WIDTH)),
            pl.BlockSpec((SEQ, IDX_DIM), lambda i: (0, 0), pipeline_mode=single),
            pl.BlockSpec((SEQ, B_WIDTH), lambda i: (0, 0), pipeline_mode=single),
            pl.BlockSpec((NKT, B_WIDTH, KT), lambda i: (0, 0, 0), pipeline_mode=single),
        ],
        out_specs=pl.BlockSpec((QB, B_WIDTH), lambda i: (i, 0)),
        out_shape=jax.ShapeDtypeStruct((SEQ, B_WIDTH), BF16),
        scratch_shapes=[
            pltpu.VMEM((NKT, KT, QB), F32),
            pltpu.VMEM((NKT, KT, QB), BF16),
            pltpu.VMEM((B_HEADS, KT, QB), F32),
            pltpu.VMEM((B_HEADS, 128, 128), F32),
            pltpu.VMEM((1, QB), F32),
            pltpu.VMEM((B_HEADS, 1, QB), F32),
            pltpu.VMEM((B_HEADS, 1, QB), F32),
            pltpu.VMEM((B_HEADS, B_HEAD_DIM, QB), F32),
            pltpu.VMEM((B_HEADS, ATT_TILES * KT, QB), F32),
        ],
        compiler_params=pltpu.CompilerParams(
            dimension_semantics=("arbitrary",), vmem_limit_bytes=VMEM_LIMIT),
        name="dsa",
    )(rel_bias, q_idx_t, w_idx_t, q_t, p, k_idx, k, v_t)


def _out_proj_kernel(ya_ref, yb_ref, ma_ref, mb_ref, x_ref, wpa_ref, wpb_ref, wo_ref, fnw_ref, o_ref):
    pa = jnp.dot(ya_ref[...], wpa_ref[...], preferred_element_type=F32)
    pb = jnp.dot(yb_ref[...], wpb_ref[...], preferred_element_type=F32)
    merged = jax.nn.sigmoid(ma_ref[...]) * pa + jax.nn.sigmoid(mb_ref[...]) * pb
    y = x_ref[...] + jnp.dot(merged.astype(BF16), wo_ref[...], preferred_element_type=F32)
    o_ref[...] = _rms(y, fnw_ref[...])


def _out_proj(y_a, y_b, p, x2, w_pa, w_pb, w_out, final_norm_w):
    tm = 256
    const = lambda i: (0, 0)
    single = pl.Buffered(1)
    return pl.pallas_call(
        _out_proj_kernel,
        grid=(SEQ // tm,),
        in_specs=[
            pl.BlockSpec((tm, A_WIDTH), lambda i: (i, 0)),
            pl.BlockSpec((tm, B_WIDTH), lambda i: (i, 0)),
            pl.BlockSpec((tm, D_MODEL), lambda i: (i, P_MA // D_MODEL)),
            pl.BlockSpec((tm, D_MODEL), lambda i: (i, P_MB // D_MODEL)),
            pl.BlockSpec((tm, D_MODEL), lambda i: (i, 0)),
            pl.BlockSpec((A_WIDTH, D_MODEL), const, pipeline_mode=single),
            pl.BlockSpec((B_WIDTH, D_MODEL), const, pipeline_mode=single),
            pl.BlockSpec((D_MODEL, D_MODEL), const, pipeline_mode=single),
            pl.BlockSpec((1, D_MODEL), const),
        ],
        out_specs=pl.BlockSpec((tm, D_MODEL), lambda i: (i, 0)),
        out_shape=jax.ShapeDtypeStruct((SEQ, D_MODEL), F32),
        compiler_params=pltpu.CompilerParams(
            dimension_semantics=("arbitrary",), vmem_limit_bytes=VMEM_LIMIT),
        name="out_proj",
    )(y_a, y_b, p, p, x2, w_pa, w_pb, w_out, final_norm_w)


def kernel(x, norm_w, w_in, lb_table, gnorm_a, q_norm_w, kv_norm_w, w_uq, w_qidx, w_ukv, kidx_norm_w,
           kidx_norm_b, w_pa, w_pb, w_out, rel_bias, final_norm_w):
    assert x.shape == (1, SEQ, D_MODEL) and w_in.shape[0] == 1
    x2 = x.reshape(SEQ, D_MODEL)
    p = _in_proj(x2, norm_w, w_in[0].astype(BF16))
    w_kv = w_ukv[0].astype(BF16).reshape(KV_RANK, B_HEADS, 2, B_HEAD_DIM)
    w_k = w_kv[:, :, 0, :].reshape(KV_RANK, B_WIDTH)
    w_v_t = w_kv[:, :, 1, :].reshape(KV_RANK, B_WIDTH).T
    q_t, q_idx_t, k, v_t, k_idx, w_idx_t = _proj2(
        p, q_norm_w, kv_norm_w, w_uq[0].astype(BF16).T, w_qidx[0].astype(BF16).T, w_k, w_v_t,
        kidx_norm_w, kidx_norm_b)
    y_a = _hgrn2(p, lb_table, gnorm_a)
    y_b = _dsa(rel_bias, q_idx_t, w_idx_t, q_t, p, k_idx, k, v_t)
    out = _out_proj(y_a, y_b, p, x2, w_pa[0].astype(BF16), w_pb[0].astype(BF16), w_out[0].astype(BF16),
                    final_norm_w.reshape(1, D_MODEL))
    return out.reshape(1, SEQ, D_MODEL)
```

```python
import math

import jax
import jax.numpy as jnp
import numpy as np
from jax import lax
from jax.experimental import pallas as pl
from jax.experimental.pallas import tpu as pltpu

F32 = jnp.float32
BF16 = jnp.bfloat16

D_MODEL = 2048
SEQ = 8192
EPS = 1e-6
A_HEADS = 8
A_HEAD_DIM = 128
A_WIDTH = 1024
B_HEADS = 8
B_HEAD_DIM = 128
B_WIDTH = 1024
Q_RANK = 512
KV_RANK = 256
IDX_HEADS = 16
IDX_DIM = 128
TOPK = 256
CHUNK = 64
REL_BUCKETS = 32
REL_MAX_DIST = 128

P_AQ, P_AF, P_AI, P_AG = 0, 1024, 2048, 3072
P_CQ = 4096
P_MISC = 4608
P_BG = 5120
P_MA = 6144
P_MB = 8192
P_WIDTH = 10240
IN_WIDTH = 10128
IN_TN = 1024
IN_SHIFT = 112

VMEM_LIMIT = 63 * 1024 * 1024

QB = 256
KT = 256
NKT = SEQ // KT
KSUB = 128
CNT_ROWS = 64
SUM_ROWS = 16
UPDATE_LAG = 2

SUB = 32
MID = SUB // 2
TB = 256

NEG_INF = float("-inf")
LOG2E = math.log2(math.e)
I16_MIN = -32768
I32_MIN = -2147483648
KEY16_NEG_INF = -32641
FINE_BELOW = 1 << 16
FINE_BITS = 17

NT_DIMS = (((1,), (1,)), ((), ()))
TN_DIMS = (((0,), (0,)), ((), ()))


def _t5_thresholds():
    half = REL_BUCKETS // 2
    max_exact = half // 2
    n = np.arange(1, 4 * REL_MAX_DIST, dtype=np.int64)
    large = max_exact + (np.log(np.maximum(n, 1).astype(np.float64) / max_exact)
                         / math.log(REL_MAX_DIST / max_exact) * (half - max_exact)).astype(np.int32)
    large = np.minimum(large, half - 1)
    b = np.where(n < max_exact, n, large)
    thr = [int(n[i + 1]) for i in np.nonzero(np.diff(b))[0] if n[i + 1] > max_exact]
    assert len(thr) == half - 1 - max_exact and b[-1] == half - 1 and thr[-1] < REL_MAX_DIST
    return max_exact, half, thr


T5_MAX_EXACT, T5_HALF, T5_THR = _t5_thresholds()


def _in_proj_kernel(x_ref, nw_ref, w_ref, o_ref, h_ref):
    @pl.when(pl.program_id(1) == 0)
    def _():
        x = x_ref[...]
        ms = jnp.mean(x * x, axis=-1, keepdims=True)
        h_ref[...] = (x * lax.rsqrt(ms + EPS) * nw_ref[...]).astype(BF16)

    j = pl.program_id(1)

    @pl.when(j < P_BG // IN_TN)
    def _():
        o_ref[...] = jnp.dot(h_ref[...], w_ref[:, :IN_TN], preferred_element_type=F32)

    @pl.when(j >= P_BG // IN_TN)
    def _():
        w = pltpu.roll(w_ref[...], IN_TN + 128 - (128 - IN_SHIFT), axis=1)[:, :IN_TN]
        o_ref[...] = jnp.dot(h_ref[...], w, preferred_element_type=F32)


def _in_proj(x2, norm_w, w_in_b):
    tm, tn = 1024, IN_TN
    n_plain = P_BG // tn

    def w_index(i, j):
        return 0, pl.multiple_of(jnp.where(j < n_plain, j * tn, j * tn - 128), 128)

    return pl.pallas_call(
        _in_proj_kernel,
        grid=(SEQ // tm, P_WIDTH // tn),
        in_specs=[
            pl.BlockSpec((tm, D_MODEL), lambda i, j: (i, 0)),
            pl.BlockSpec((1, D_MODEL), lambda i, j: (0, 0)),
            pl.BlockSpec((pl.Element(D_MODEL), pl.Element(tn + 128, padding=(0, P_WIDTH - IN_WIDTH))), w_index),
        ],
        out_specs=pl.BlockSpec((tm, tn), lambda i, j: (i, j)),
        out_shape=jax.ShapeDtypeStruct((SEQ, P_WIDTH), F32),
        scratch_shapes=[pltpu.VMEM((tm, D_MODEL), BF16)],
        compiler_params=pltpu.CompilerParams(
            dimension_semantics=("arbitrary", "arbitrary"), vmem_limit_bytes=VMEM_LIMIT),
        name="in_proj",
    )(x2, norm_w, w_in_b)


def _rms(x, w):
    return x * lax.rsqrt(jnp.mean(x * x, axis=-1, keepdims=True) + EPS) * w


def _proj2_kernel(cq_ref, misc_ref, qnw_ref, kvnw_ref, wuqt_ref, wqit_ref, wk_ref, wvt_ref, lnw_ref, lnb_ref,
                  qt_ref, qit_ref, k_ref, vt_ref, kidx_ref, wt_ref):
    cqn = _rms(cq_ref[...], qnw_ref[...]).astype(BF16)
    qt = lax.dot_general(wuqt_ref[...], cqn, NT_DIMS, preferred_element_type=F32)
    qt_ref[...] = (qt * (B_HEAD_DIM ** -0.5 * LOG2E)).astype(BF16)
    qit = lax.dot_general(wqit_ref[...], cqn, NT_DIMS, preferred_element_type=F32)
    for h in range(IDX_HEADS):
        qit_ref[h] = qit[h * IDX_DIM:(h + 1) * IDX_DIM, :].astype(BF16)
    misc = misc_ref[...]
    ckvn = _rms(misc[:, :KV_RANK], kvnw_ref[...]).astype(BF16)
    k_ref[...] = jnp.dot(ckvn, wk_ref[...], preferred_element_type=F32).astype(BF16)
    vt = lax.dot_general(wvt_ref[...], ckvn, NT_DIMS, preferred_element_type=F32).astype(BF16)
    for t in range(vt_ref.shape[0]):
        vt_ref[t] = vt[:, t * KT:(t + 1) * KT]
    kr = misc[:, KV_RANK:KV_RANK + IDX_DIM]
    mu = jnp.mean(kr, axis=-1, keepdims=True)
    var = jnp.mean(jnp.square(kr - mu), axis=-1, keepdims=True)
    kidx_ref[...] = ((kr - mu) * lax.rsqrt(var + EPS) * lnw_ref[...] + lnb_ref[...]).astype(BF16)
    wt_ref[...] = (misc[:, KV_RANK + IDX_DIM:] * (IDX_HEADS ** -0.5 * IDX_DIM ** -0.5)).T


def _proj2(p, q_norm_w, kv_norm_w, w_uq_t, w_qidx_t, w_k, w_v_t, ln_w, ln_b):
    tm = 2 * KT
    const = lambda i: (0, 0)
    return pl.pallas_call(
        _proj2_kernel,
        grid=(SEQ // tm,),
        in_specs=[
            pl.BlockSpec((tm, 512), lambda i: (i, P_CQ // 512)),
            pl.BlockSpec((tm, 512), lambda i: (i, P_MISC // 512)),
            pl.BlockSpec((1, Q_RANK), const),
            pl.BlockSpec((1, KV_RANK), const),
            pl.BlockSpec((B_WIDTH, Q_RANK), const),
            pl.BlockSpec((IDX_HEADS * IDX_DIM, Q_RANK), const),
            pl.BlockSpec((KV_RANK, B_WIDTH), const),
            pl.BlockSpec((B_WIDTH, KV_RANK), const),
            pl.BlockSpec((1, IDX_DIM), const),
            pl.BlockSpec((1, IDX_DIM), const),
        ],
        out_specs=[
            pl.BlockSpec((B_WIDTH, tm), lambda i: (0, i)),
            pl.BlockSpec((IDX_HEADS, IDX_DIM, tm), lambda i: (0, 0, i)),
            pl.BlockSpec((tm, B_WIDTH), lambda i: (i, 0)),
            pl.BlockSpec((tm // KT, B_WIDTH, KT), lambda i: (i, 0, 0)),
            pl.BlockSpec((tm, IDX_DIM), lambda i: (i, 0)),
            pl.BlockSpec((128, tm), lambda i: (0, i)),
        ],
        out_shape=[
            jax.ShapeDtypeStruct((B_WIDTH, SEQ), BF16),
            jax.ShapeDtypeStruct((IDX_HEADS, IDX_DIM, SEQ), BF16),
            jax.ShapeDtypeStruct((SEQ, B_WIDTH), BF16),
            jax.ShapeDtypeStruct((NKT, B_WIDTH, KT), BF16),
            jax.ShapeDtypeStruct((SEQ, IDX_DIM), BF16),
            jax.ShapeDtypeStruct((128, SEQ), F32),
        ],
        compiler_params=pltpu.CompilerParams(
            dimension_semantics=("arbitrary",), vmem_limit_bytes=VMEM_LIMIT),
        name="proj2",
    )(p, p, q_norm_w, kv_norm_w, w_uq_t, w_qidx_t, w_k, w_v_t, ln_w, ln_b)


def _split3(x):
    x1 = x.astype(BF16)
    r1 = x - x1.astype(F32)
    x2 = r1.astype(BF16)
    x3 = (r1 - x2.astype(F32)).astype(BF16)
    return x1, x2, x3


def _hgrn2_kernel(aq_ref, af_ref, ai_ref, ag_ref, lbt_ref, gn_ref, o_ref,
                  st_ref, sel_ref, a_ref, a2_ref, b2_ref, kd_ref, v_ref, e_ref, oacc_ref):
    @pl.when(pl.program_id(0) == 0)
    def _():
        st_ref[...] = jnp.zeros_like(st_ref)
        r = lax.broadcasted_iota(jnp.int32, (TB, TB), 0)
        c = lax.broadcasted_iota(jnp.int32, (TB, TB), 1)
        same = (r // SUB) == (c // SUB)
        sel_ref[...] = jnp.where(same & (c <= r), 1.0, 0.0).astype(BF16)

    lbt = lbt_ref[...]
    mx = jnp.maximum(lbt[0:1], lbt[1:2])
    e0 = jnp.exp(lbt[0:1] - mx)
    e1 = jnp.exp(lbt[1:2] - mx)
    lb = e0 / (e0 + e1)

    f = lb + (1.0 - lb) * jax.nn.sigmoid(af_ref[...])
    g = jnp.log(f)
    sel = sel_ref[...]
    g1, g2, g3 = _split3(g)
    cum = (jnp.dot(sel, g1, preferred_element_type=F32) + jnp.dot(sel, g2, preferred_element_type=F32)
           + jnp.dot(sel, g3, preferred_element_type=F32))
    cum3 = cum.reshape(TB // SUB, SUB, A_WIDTH)
    mid = jnp.broadcast_to(cum3[:, MID - 1:MID, :], cum3.shape).reshape(TB, A_WIDTH)
    tot = jnp.broadcast_to(cum3[:, SUB - 1:SUB, :], cum3.shape).reshape(TB, A_WIDTH)
    aq = aq_ref[...]
    q = aq * jax.nn.sigmoid(aq) * (A_HEAD_DIM ** -0.5)
    kk = 1.0 - f
    a_ref[...] = (q * jnp.exp(cum)).astype(BF16)
    a2_ref[...] = (q * jnp.exp(cum - mid)).astype(BF16)
    b2_ref[...] = (kk * jnp.exp(mid - cum)).astype(BF16)
    kd_ref[...] = (kk * jnp.exp(tot - cum)).astype(BF16)
    v_ref[...] = ai_ref[...].astype(BF16)
    e_ref[...] = jnp.exp(tot)

    tr = lax.broadcasted_iota(jnp.int32, (SUB, SUB), 0)
    tc = lax.broadcasted_iota(jnp.int32, (SUB, SUB), 1)
    causal = tc <= tr

    def chunk_body(c, carry):
        r0 = pl.multiple_of(c * SUB, SUB)
        heads = [slice(h * A_HEAD_DIM, (h + 1) * A_HEAD_DIM) for h in range(A_HEADS)]
        ah = [a_ref[pl.ds(r0, SUB), cs] for cs in heads]
        vh = [v_ref[pl.ds(r0, SUB), cs] for cs in heads]
        sc = [lax.dot_general(a2_ref[pl.ds(r0, SUB), heads[h]], b2_ref[pl.ds(r0, SUB), heads[h]], NT_DIMS,
                              preferred_element_type=F32) for h in range(A_HEADS)]
        inter = []
        for h in range(A_HEADS):
            st = st_ref[h]
            inter.append(lax.dot_general(ah[h], st.astype(BF16), NT_DIMS, preferred_element_type=F32))
            dst = lax.dot_general(vh[h], kd_ref[pl.ds(r0, SUB), heads[h]], TN_DIMS, preferred_element_type=F32)
            st_ref[h] = st * e_ref[pl.ds(r0, 1), heads[h]] + dst
        for h in range(A_HEADS):
            pm = jnp.where(causal, sc[h], 0.0).astype(BF16)
            oacc_ref[pl.ds(r0, SUB), heads[h]] = inter[h] + jnp.dot(pm, vh[h], preferred_element_type=F32)
        return carry

    lax.fori_loop(0, TB // SUB, chunk_body, 0, unroll=True)

    gn = gn_ref[...]
    ag = ag_ref[...]
    gate = ag * jax.nn.sigmoid(ag)
    for h in range(A_HEADS):
        cs = slice(h * A_HEAD_DIM, (h + 1) * A_HEAD_DIM)
        o = oacc_ref[:, cs]
        on = o * lax.rsqrt(jnp.mean(o * o, axis=-1, keepdims=True) + EPS) * gn
        o_ref[:, cs] = (on * gate[:, cs]).astype(BF16)


def _hgrn2(p, lb_table, gnorm):
    blk = lambda col: pl.BlockSpec((TB, A_WIDTH), lambda i: (i, col // A_WIDTH))
    return pl.pallas_call(
        _hgrn2_kernel,
        grid=(SEQ // TB,),
        in_specs=[blk(P_AQ), blk(P_AF), blk(P_AI), blk(P_AG),
                  pl.BlockSpec((2, A_WIDTH), lambda i: (0, 0)),
                  pl.BlockSpec((1, A_HEAD_DIM), lambda i: (0, 0))],
        out_specs=pl.BlockSpec((TB, A_WIDTH), lambda i: (i, 0)),
        out_shape=jax.ShapeDtypeStruct((SEQ, A_WIDTH), BF16),
        scratch_shapes=[
            pltpu.VMEM((A_HEADS, A_HEAD_DIM, A_HEAD_DIM), F32),
            pltpu.VMEM((TB, TB), BF16),
            pltpu.VMEM((TB, A_WIDTH), BF16),
            pltpu.VMEM((TB, A_WIDTH), BF16),
            pltpu.VMEM((TB, A_WIDTH), BF16),
            pltpu.VMEM((TB, A_WIDTH), BF16),
            pltpu.VMEM((TB, A_WIDTH), BF16),
            pltpu.VMEM((TB, A_WIDTH), F32),
            pltpu.VMEM((TB, A_WIDTH), F32),
        ],
        compiler_params=pltpu.CompilerParams(
            dimension_semantics=("arbitrary",), vmem_limit_bytes=VMEM_LIMIT),
        name="hgrn2",
    )(p, p, p, p, lb_table, gnorm)


def _t5_bucket(rel):
    n = jnp.abs(rel)
    large = jnp.full(rel.shape, T5_MAX_EXACT, jnp.int32)
    for t in T5_THR:
        large = large + jnp.where(n >= t, 1, 0)
    return jnp.where(rel > 0, T5_HALF, 0) + jnp.where(n < T5_MAX_EXACT, n, large)


def _dsa_kernel(relb_ref, qi_ref, w_ref, q_ref, bg_ref, kidx_ref, k_ref, v_ref, o_ref,
                sc_ref, hi_ref, b0_ref, b1_ref, thr_ref, m_ref, l_ref, acc_ref, s_ref, ms_ref, al_ref):
    qb = pl.program_id(0)

    @pl.when(qb == 0)
    def _():
        def bias_tile(rel, h):
            bucket = _t5_bucket(rel)
            val = jnp.zeros(rel.shape, F32)
            for b in range(REL_BUCKETS):
                val = jnp.where(bucket == b, relb_ref[b, h], val)
            return (val - relb_ref[T5_HALF - 1, h]) * LOG2E

        r0 = lax.broadcasted_iota(jnp.int32, (KT, QB), 0)
        c0 = lax.broadcasted_iota(jnp.int32, (KT, QB), 1)
        r1 = lax.broadcasted_iota(jnp.int32, (128, 128), 0)
        c1 = lax.broadcasted_iota(jnp.int32, (128, 128), 1)
        for h in range(B_HEADS):
            b0_ref[h] = bias_tile(r0 - c0, h)
            b1_ref[h] = bias_tile(r1 - 128 - c1, h)

    def idx_tile(j, diag):
        for ks in range(KT // KSUB):
            kt = kidx_ref[pl.ds(pl.multiple_of(j * KT + ks * KSUB, KSUB), KSUB), :]
            acc = jnp.zeros((KSUB, QB), F32)
            for h in range(IDX_HEADS):
                r = jnp.dot(kt, qi_ref[h], preferred_element_type=F32)
                acc = acc + w_ref[h:h + 1, :] * jnp.maximum(r, 0.0)
            if diag:
                rk = ks * KSUB + lax.broadcasted_iota(jnp.int32, (KSUB, QB), 0)
                ct = lax.broadcasted_iota(jnp.int32, (KSUB, QB), 1)
                acc = jnp.where((rk // CHUNK) <= (ct // CHUNK), acc, NEG_INF)
            sc_ref[j, ks * KSUB:(ks + 1) * KSUB, :] = acc
            hi_ref[j, ks * KSUB:(ks + 1) * KSUB, :] = acc.astype(BF16)

    def idx_body(i, carry):
        idx_tile(2 * i, False)
        idx_tile(2 * i + 1, False)
        return carry

    lax.fori_loop(0, qb // 2, idx_body, 0)

    @pl.when(qb % 2 == 1)
    def _():
        idx_tile(qb - 1, False)
        idx_tile(qb, True)

    @pl.when(qb % 2 == 0)
    def _():
        idx_tile(qb, True)

    npairs = (qb + 2) // 2

    @pl.when(qb % 2 == 0)
    def _():
        sc_ref[qb + 1] = jnp.full((KT, QB), NEG_INF, F32)
        hi_ref[qb + 1] = jnp.full((KT, QB), NEG_INF, BF16)

    def count(ref, thr):
        one = jnp.ones((KT, QB), ref.dtype)
        zero = jnp.zeros((KT, QB), ref.dtype)

        def body(jp, cnt):
            for u in range(2):
                sel = jnp.where(ref[2 * jp + u] >= thr, one, zero)
                for r in range(KT // CNT_ROWS):
                    cnt = cnt + sel[r * CNT_ROWS:(r + 1) * CNT_ROWS]
            return cnt

        cnt = lax.fori_loop(0, npairs, body, jnp.zeros((CNT_ROWS, QB), ref.dtype))
        return jnp.sum(cnt.astype(F32), axis=0, keepdims=True)

    def key16_to_bf16(k):
        bits = jnp.where(k >= 0, k, k ^ 0x7FFF).astype(jnp.int16)
        return lax.bitcast_convert_type(bits, BF16)

    def key32_to_f32(k):
        return lax.bitcast_convert_type(jnp.where(k >= 0, k, k ^ 0x7FFFFFFF), F32)

    c0 = count(hi_ref, jnp.zeros((1, QB), BF16))
    k16 = jnp.where(c0 >= float(TOPK), 0, I16_MIN).astype(jnp.int32)

    def coarse_body(i, k):
        cand = k | jnp.left_shift(jnp.int32(1), 14 - i)
        return jnp.where(count(hi_ref, key16_to_bf16(cand)) >= float(TOPK), cand, k)

    k16 = jnp.maximum(lax.fori_loop(0, 15, coarse_body, k16), KEY16_NEG_INF)
    k_mid = jnp.where(k16 >= 0, k16 << 16, (k16 << 16) | 0xFFFF)
    k_base = jnp.maximum(k_mid, I32_MIN + FINE_BELOW) - FINE_BELOW

    def fine_body(i, off):
        cand = off | jnp.left_shift(jnp.int32(1), FINE_BITS - 1 - i)
        return jnp.where(count(sc_ref, key32_to_f32(k_base + cand)) >= float(TOPK), cand, off)

    off = lax.fori_loop(0, FINE_BITS, fine_body, jnp.zeros((1, QB), jnp.int32))
    thr_ref[...] = key32_to_f32(k_base + off)

    m_ref[...] = jnp.full(m_ref.shape, NEG_INF, F32)
    l_ref[...] = jnp.zeros_like(l_ref)
    acc_ref[...] = jnp.zeros_like(acc_ref)
    ones_rows = jnp.ones((SUM_ROWS, KT), BF16)

    def att_step(logits=None, values=None):
        if logits is not None:
            j1, kind, slot1 = logits
            s_t = sc_ref[j1]
            mask = jnp.where((s_t >= thr_ref[...]) & (s_t > NEG_INF), 0.0, NEG_INF)
            k0 = pl.multiple_of(j1 * KT, KT)
        if values is not None:
            j2, slot2 = values
        pvs = []

        def update(h):
            a = al_ref[slot2, h]
            l_ref[h] = a * l_ref[h] + pvs[h][B_HEAD_DIM:B_HEAD_DIM + 1, :]
            acc_ref[h] = a * acc_ref[h] + pvs[h][:B_HEAD_DIM, :]

        for h in range(B_HEADS):
            hs = slice(h * B_HEAD_DIM, (h + 1) * B_HEAD_DIM)
            if logits is not None:
                s = jnp.dot(k_ref[pl.ds(k0, KT), hs], q_ref[hs, :], preferred_element_type=F32)
                if kind == "diag":
                    s = s + b0_ref[h]
                elif kind == "prev":
                    lo = jnp.concatenate([s[128:, :128] + b1_ref[h], s[128:, 128:]], axis=1)
                    s = jnp.concatenate([s[:128, :], lo], axis=0)
                s = s + mask
                s_ref[slot1, h] = s
                m_prev = m_ref[h]
                m_next = jnp.maximum(m_prev, jnp.max(s, axis=0, keepdims=True))
                m_safe = jnp.where(m_next == NEG_INF, 0.0, m_next)
                ms_ref[slot1, h] = m_safe
                al_ref[slot1, h] = jnp.exp2(m_prev - m_safe)
                m_ref[h] = m_next
            if values is not None:
                p = jnp.exp2(s_ref[slot2, h] - ms_ref[slot2, h]).astype(BF16)
                pvs.append(jnp.dot(jnp.concatenate([v_ref[j2, hs, :], ones_rows], axis=0), p,
                                   preferred_element_type=F32))
                if h >= UPDATE_LAG:
                    update(h - UPDATE_LAG)
        if values is not None:
            for h in range(B_HEADS - UPDATE_LAG, B_HEADS):
                update(h)

    odd = qb % 2

    @pl.when(qb == 0)
    def _():
        att_step(logits=(0, "diag", 0))

    @pl.when(qb == 1)
    def _():
        att_step(logits=(0, "prev", 1))

    @pl.when(qb >= 2)
    def _():
        @pl.when(odd == 0)
        def _():
            att_step(logits=(0, "far", 0))

        @pl.when(odd == 1)
        def _():
            att_step(logits=(0, "far", 1))
            att_step(logits=(1, "far", 0), values=(0, 1))

        def att_body(i, carry):
            j = odd + 2 * i
            att_step(logits=(j + 1, "far", 1), values=(j, 0))
            att_step(logits=(j + 2, "far", 0), values=(j + 1, 1))
            return carry

        lax.fori_loop(0, (qb - 2) // 2, att_body, 0)
        att_step(logits=(qb - 1, "prev", 1), values=(qb - 2, 0))

    @pl.when(qb >= 1)
    def _():
        att_step(logits=(qb, "diag", 0), values=(qb - 1, 1))

    att_step(values=(qb, 0))

    for h in range(B_HEADS):
        hs = slice(h * B_HEAD_DIM, (h + 1) * B_HEAD_DIM)
        bg = bg_ref[:, hs]
        o_ref[:, hs] = ((acc_ref[h] / l_ref[h]).T * (bg * jax.nn.sigmoid(bg))).astype(BF16)


def _dsa(rel_bias, q_idx_t, w_idx_t, q_t, p, k_idx, k, v_t):
    single = pl.Buffered(1)
    return pl.pallas_call(
        _dsa_kernel,
        grid=(SEQ // QB,),
        in_specs=[
            pl.BlockSpec(memory_space=pltpu.SMEM),
            pl.BlockSpec((IDX_HEADS, IDX_DIM, QB), lambda i: (0, 0, i)),
            pl.BlockSpec((128, QB), lambda i: (0, i)),
            pl.BlockSpec((B_WIDTH, QB), lambda i: (0, i)),
            pl.BlockSpec((QB, B_WIDTH), lambda i: (i, P_BG // B_WIDTH)),
            pl.BlockSpec((SEQ, IDX_DIM), lambda i: (0, 0), pipeline_mode=single),
            pl.BlockSpec((SEQ, B_WIDTH), lambda i: (0, 0), pipeline_mode=single),
            pl.BlockSpec((NKT, B_WIDTH, KT), lambda i: (0, 0, 0), pipeline_mode=single),
        ],
        out_specs=pl.BlockSpec((QB, B_WIDTH), lambda i: (i, 0)),
        out_shape=jax.ShapeDtypeStruct((SEQ, B_WIDTH), BF16),
        scratch_shapes=[
            pltpu.VMEM((NKT, KT, QB), F32),
            pltpu.VMEM((NKT, KT, QB), BF16),
            pltpu.VMEM((B_HEADS, KT, QB), F32),
            pltpu.VMEM((B_HEADS, 128, 128), F32),
            pltpu.VMEM((1, QB), F32),
            pltpu.VMEM((B_HEADS, 1, QB), F32),
            pltpu.VMEM((B_HEADS, 1, QB), F32),
            pltpu.VMEM((B_HEADS, B_HEAD_DIM, QB), F32),
            pltpu.VMEM((2, B_HEADS, KT, QB), F32),
            pltpu.VMEM((2, B_HEADS, 1, QB), F32),
            pltpu.VMEM((2, B_HEADS, 1, QB), F32),
        ],
        compiler_params=pltpu.CompilerParams(
            dimension_semantics=("arbitrary",), vmem_limit_bytes=VMEM_LIMIT),
        name="dsa",
    )(rel_bias, q_idx_t, w_idx_t, q_t, p, k_idx, k, v_t)


def _out_proj_kernel(ya_ref, yb_ref, ma_ref, mb_ref, x_ref, wpa_ref, wpb_ref, wo_ref, fnw_ref, o_ref):
    pa = jnp.dot(ya_ref[...], wpa_ref[...], preferred_element_type=F32)
    pb = jnp.dot(yb_ref[...], wpb_ref[...], preferred_element_type=F32)
    merged = jax.nn.sigmoid(ma_ref[...]) * pa + jax.nn.sigmoid(mb_ref[...]) * pb
    y = x_ref[...] + jnp.dot(merged.astype(BF16), wo_ref[...], preferred_element_type=F32)
    o_ref[...] = _rms(y, fnw_ref[...])


def _out_proj(y_a, y_b, p, x2, w_pa, w_pb, w_out, final_norm_w):
    tm = 512
    const = lambda i: (0, 0)
    single = pl.Buffered(1)
    return pl.pallas_call(
        _out_proj_kernel,
        grid=(SEQ // tm,),
        in_specs=[
            pl.BlockSpec((tm, A_WIDTH), lambda i: (i, 0)),
            pl.BlockSpec((tm, B_WIDTH), lambda i: (i, 0)),
            pl.BlockSpec((tm, D_MODEL), lambda i: (i, P_MA // D_MODEL)),
            pl.BlockSpec((tm, D_MODEL), lambda i: (i, P_MB // D_MODEL)),
            pl.BlockSpec((tm, D_MODEL), lambda i: (i, 0)),
            pl.BlockSpec((A_WIDTH, D_MODEL), const, pipeline_mode=single),
            pl.BlockSpec((B_WIDTH, D_MODEL), const, pipeline_mode=single),
            pl.BlockSpec((D_MODEL, D_MODEL), const, pipeline_mode=single),
            pl.BlockSpec((1, D_MODEL), const),
        ],
        out_specs=pl.BlockSpec((tm, D_MODEL), lambda i: (i, 0)),
        out_shape=jax.ShapeDtypeStruct((SEQ, D_MODEL), F32),
        compiler_params=pltpu.CompilerParams(
            dimension_semantics=("arbitrary",), vmem_limit_bytes=VMEM_LIMIT),
        name="out_proj",
    )(y_a, y_b, p, p, x2, w_pa, w_pb, w_out, final_norm_w)


def kernel(x, norm_w, w_in, lb_table, gnorm_a, q_norm_w, kv_norm_w, w_uq, w_qidx, w_ukv, kidx_norm_w,
           kidx_norm_b, w_pa, w_pb, w_out, rel_bias, final_norm_w):
    assert x.shape == (1, SEQ, D_MODEL) and w_in.shape[0] == 1
    x2 = x.reshape(SEQ, D_MODEL)
    p = _in_proj(x2, norm_w, w_in[0].astype(BF16))
    w_kv = w_ukv[0].astype(BF16).reshape(KV_RANK, B_HEADS, 2, B_HEAD_DIM)
    w_k = w_kv[:, :, 0, :].reshape(KV_RANK, B_WIDTH)
    w_v_t = w_kv[:, :, 1, :].reshape(KV_RANK, B_WIDTH).T
    q_t, q_idx_t, k, v_t, k_idx, w_idx_t = _proj2(
        p, q_norm_w, kv_norm_w, w_uq[0].astype(BF16).T, w_qidx[0].astype(BF16).T, w_k, w_v_t,
        kidx_norm_w, kidx_norm_b)
    y_a = _hgrn2(p, lb_table, gnorm_a)
    y_b = _dsa(rel_bias, q_idx_t, w_idx_t, q_t, p, k_idx, k, v_t)
    out = _out_proj(y_a, y_b, p, x2, w_pa[0].astype(BF16), w_pb[0].astype(BF16), w_out[0].astype(BF16),
                    final_norm_w.reshape(1, D_MODEL))
    return out.reshape(1, SEQ, D_MODEL)
```

```python
import math

import jax
import jax.numpy as jnp
import numpy as np
from jax import lax
from jax.experimental import pallas as pl
from jax.experimental.pallas import tpu as pltpu

F32 = jnp.float32
BF16 = jnp.bfloat16

D_MODEL = 2048
SEQ = 8192
EPS = 1e-6
A_HEADS = 8
A_HEAD_DIM = 128
A_WIDTH = 1024
B_HEADS = 8
B_HEAD_DIM = 128
B_WIDTH = 1024
Q_RANK = 512
KV_RANK = 256
IDX_HEADS = 16
IDX_DIM = 128
TOPK = 256
CHUNK = 64
REL_BUCKETS = 32
REL_MAX_DIST = 128

P_AQ, P_AF, P_AI, P_AG = 0, 1024, 2048, 3072
P_CQ = 4096
P_MISC = 4608
P_BG = 5120
P_MA = 6144
P_MB = 8192
P_WIDTH = 10240
IN_WIDTH = 10128
IN_TN = 1024
IN_SHIFT = 112

VMEM_LIMIT = 63 * 1024 * 1024

QB = 256
KT = 256
NKT = SEQ // KT
KSUB = 128
CNT_ROWS = 64
SUM_ROWS = 16
UPDATE_LAG = 2

SUB = 32
MID = SUB // 2
TB = 256

NEG_INF = float("-inf")
LOG2E = math.log2(math.e)
I16_MIN = -32768
I32_MIN = -2147483648
KEY16_NEG_INF = -32641
SEQ_BITS = 13
assert SEQ == 1 << SEQ_BITS
FINE_BELOW = 1 << 16
FINE_BITS = 17

NT_DIMS = (((1,), (1,)), ((), ()))
TN_DIMS = (((0,), (0,)), ((), ()))


def _t5_thresholds():
    half = REL_BUCKETS // 2
    max_exact = half // 2
    n = np.arange(1, 4 * REL_MAX_DIST, dtype=np.int64)
    large = max_exact + (np.log(np.maximum(n, 1).astype(np.float64) / max_exact)
                         / math.log(REL_MAX_DIST / max_exact) * (half - max_exact)).astype(np.int32)
    large = np.minimum(large, half - 1)
    b = np.where(n < max_exact, n, large)
    thr = [int(n[i + 1]) for i in np.nonzero(np.diff(b))[0] if n[i + 1] > max_exact]
    assert len(thr) == half - 1 - max_exact and b[-1] == half - 1 and thr[-1] < REL_MAX_DIST
    return max_exact, half, thr


T5_MAX_EXACT, T5_HALF, T5_THR = _t5_thresholds()


def _in_proj_kernel(x_ref, nw_ref, w_ref, o_ref, h_ref):
    @pl.when(pl.program_id(1) == 0)
    def _():
        x = x_ref[...]
        ms = jnp.mean(x * x, axis=-1, keepdims=True)
        h_ref[...] = (x * lax.rsqrt(ms + EPS) * nw_ref[...]).astype(BF16)

    j = pl.program_id(1)

    @pl.when(j < P_BG // IN_TN)
    def _():
        o_ref[...] = jnp.dot(h_ref[...], w_ref[:, :IN_TN], preferred_element_type=F32)

    @pl.when(j >= P_BG // IN_TN)
    def _():
        w = pltpu.roll(w_ref[...], IN_TN + 128 - (128 - IN_SHIFT), axis=1)[:, :IN_TN]
        o_ref[...] = jnp.dot(h_ref[...], w, preferred_element_type=F32)


def _in_proj(x2, norm_w, w_in_b):
    tm, tn = 1024, IN_TN
    n_plain = P_BG // tn

    def w_index(i, j):
        return 0, pl.multiple_of(jnp.where(j < n_plain, j * tn, j * tn - 128), 128)

    return pl.pallas_call(
        _in_proj_kernel,
        grid=(SEQ // tm, P_WIDTH // tn),
        in_specs=[
            pl.BlockSpec((tm, D_MODEL), lambda i, j: (i, 0)),
            pl.BlockSpec((1, D_MODEL), lambda i, j: (0, 0)),
            pl.BlockSpec((pl.Element(D_MODEL), pl.Element(tn + 128, padding=(0, P_WIDTH - IN_WIDTH))), w_index),
        ],
        out_specs=pl.BlockSpec((tm, tn), lambda i, j: (i, j)),
        out_shape=jax.ShapeDtypeStruct((SEQ, P_WIDTH), F32),
        scratch_shapes=[pltpu.VMEM((tm, D_MODEL), BF16)],
        compiler_params=pltpu.CompilerParams(
            dimension_semantics=("arbitrary", "arbitrary"), vmem_limit_bytes=VMEM_LIMIT),
        name="in_proj",
    )(x2, norm_w, w_in_b)


def _rms(x, w):
    return x * lax.rsqrt(jnp.mean(x * x, axis=-1, keepdims=True) + EPS) * w


def _proj2_kernel(cq_ref, misc_ref, qnw_ref, kvnw_ref, wuqt_ref, wqit_ref, wk_ref, wvt_ref, lnw_ref, lnb_ref,
                  qt_ref, qit_ref, k_ref, vt_ref, kidx_ref, wt_ref):
    cqn = _rms(cq_ref[...], qnw_ref[...]).astype(BF16)
    qt = lax.dot_general(wuqt_ref[...], cqn, NT_DIMS, preferred_element_type=F32)
    qt_ref[...] = (qt * (B_HEAD_DIM ** -0.5 * LOG2E)).astype(BF16)
    qit = lax.dot_general(wqit_ref[...], cqn, NT_DIMS, preferred_element_type=F32)
    for h in range(IDX_HEADS):
        qit_ref[h] = qit[h * IDX_DIM:(h + 1) * IDX_DIM, :].astype(BF16)
    misc = misc_ref[...]
    ckvn = _rms(misc[:, :KV_RANK], kvnw_ref[...]).astype(BF16)
    k_ref[...] = jnp.dot(ckvn, wk_ref[...], preferred_element_type=F32).astype(BF16)
    vt = lax.dot_general(wvt_ref[...], ckvn, NT_DIMS, preferred_element_type=F32).astype(BF16)
    for t in range(vt_ref.shape[0]):
        vt_ref[t] = vt[:, t * KT:(t + 1) * KT]
    kr = misc[:, KV_RANK:KV_RANK + IDX_DIM]
    mu = jnp.mean(kr, axis=-1, keepdims=True)
    var = jnp.mean(jnp.square(kr - mu), axis=-1, keepdims=True)
    kidx_ref[...] = ((kr - mu) * lax.rsqrt(var + EPS) * lnw_ref[...] + lnb_ref[...]).astype(BF16)
    wt_ref[...] = (misc[:, KV_RANK + IDX_DIM:] * (IDX_HEADS ** -0.5 * IDX_DIM ** -0.5)).T


def _proj2(p, q_norm_w, kv_norm_w, w_uq_t, w_qidx_t, w_k, w_v_t, ln_w, ln_b):
    tm = 2 * KT
    const = lambda i: (0, 0)
    return pl.pallas_call(
        _proj2_kernel,
        grid=(SEQ // tm,),
        in_specs=[
            pl.BlockSpec((tm, 512), lambda i: (i, P_CQ // 512)),
            pl.BlockSpec((tm, 512), lambda i: (i, P_MISC // 512)),
            pl.BlockSpec((1, Q_RANK), const),
            pl.BlockSpec((1, KV_RANK), const),
            pl.BlockSpec((B_WIDTH, Q_RANK), const),
            pl.BlockSpec((IDX_HEADS * IDX_DIM, Q_RANK), const),
            pl.BlockSpec((KV_RANK, B_WIDTH), const),
            pl.BlockSpec((B_WIDTH, KV_RANK), const),
            pl.BlockSpec((1, IDX_DIM), const),
            pl.BlockSpec((1, IDX_DIM), const),
        ],
        out_specs=[
            pl.BlockSpec((B_WIDTH, tm), lambda i: (0, i)),
            pl.BlockSpec((IDX_HEADS, IDX_DIM, tm), lambda i: (0, 0, i)),
            pl.BlockSpec((tm, B_WIDTH), lambda i: (i, 0)),
            pl.BlockSpec((tm // KT, B_WIDTH, KT), lambda i: (i, 0, 0)),
            pl.BlockSpec((tm, IDX_DIM), lambda i: (i, 0)),
            pl.BlockSpec((128, tm), lambda i: (0, i)),
        ],
        out_shape=[
            jax.ShapeDtypeStruct((B_WIDTH, SEQ), BF16),
            jax.ShapeDtypeStruct((IDX_HEADS, IDX_DIM, SEQ), BF16),
            jax.ShapeDtypeStruct((SEQ, B_WIDTH), BF16),
            jax.ShapeDtypeStruct((NKT, B_WIDTH, KT), BF16),
            jax.ShapeDtypeStruct((SEQ, IDX_DIM), BF16),
            jax.ShapeDtypeStruct((128, SEQ), F32),
        ],
        compiler_params=pltpu.CompilerParams(
            dimension_semantics=("arbitrary",), vmem_limit_bytes=VMEM_LIMIT),
        name="proj2",
    )(p, p, q_norm_w, kv_norm_w, w_uq_t, w_qidx_t, w_k, w_v_t, ln_w, ln_b)


def _split3(x):
    x1 = x.astype(BF16)
    r1 = x - x1.astype(F32)
    x2 = r1.astype(BF16)
    x3 = (r1 - x2.astype(F32)).astype(BF16)
    return x1, x2, x3


def _hgrn2_kernel(aq_ref, af_ref, ai_ref, ag_ref, lbt_ref, gn_ref, o_ref,
                  st_ref, sel_ref, a_ref, a2_ref, b2_ref, kd_ref, v_ref, e_ref, oacc_ref):
    @pl.when(pl.program_id(0) == 0)
    def _():
        st_ref[...] = jnp.zeros_like(st_ref)
        r = lax.broadcasted_iota(jnp.int32, (TB, TB), 0)
        c = lax.broadcasted_iota(jnp.int32, (TB, TB), 1)
        same = (r // SUB) == (c // SUB)
        sel_ref[...] = jnp.where(same & (c <= r), 1.0, 0.0).astype(BF16)

    lbt = lbt_ref[...]
    mx = jnp.maximum(lbt[0:1], lbt[1:2])
    e0 = jnp.exp(lbt[0:1] - mx)
    e1 = jnp.exp(lbt[1:2] - mx)
    lb = e0 / (e0 + e1)

    f = lb + (1.0 - lb) * jax.nn.sigmoid(af_ref[...])
    g = jnp.log(f)
    sel = sel_ref[...]
    g1, g2, g3 = _split3(g)
    cum = (jnp.dot(sel, g1, preferred_element_type=F32) + jnp.dot(sel, g2, preferred_element_type=F32)
           + jnp.dot(sel, g3, preferred_element_type=F32))
    cum3 = cum.reshape(TB // SUB, SUB, A_WIDTH)
    mid = jnp.broadcast_to(cum3[:, MID - 1:MID, :], cum3.shape).reshape(TB, A_WIDTH)
    tot = jnp.broadcast_to(cum3[:, SUB - 1:SUB, :], cum3.shape).reshape(TB, A_WIDTH)
    aq = aq_ref[...]
    q = aq * jax.nn.sigmoid(aq) * (A_HEAD_DIM ** -0.5)
    kk = 1.0 - f
    a_ref[...] = (q * jnp.exp(cum)).astype(BF16)
    a2_ref[...] = (q * jnp.exp(cum - mid)).astype(BF16)
    b2_ref[...] = (kk * jnp.exp(mid - cum)).astype(BF16)
    kd_ref[...] = (kk * jnp.exp(tot - cum)).astype(BF16)
    v_ref[...] = ai_ref[...].astype(BF16)
    e_ref[...] = jnp.exp(tot)

    tr = lax.broadcasted_iota(jnp.int32, (SUB, SUB), 0)
    tc = lax.broadcasted_iota(jnp.int32, (SUB, SUB), 1)
    causal = tc <= tr

    def chunk_body(c, carry):
        r0 = pl.multiple_of(c * SUB, SUB)
        heads = [slice(h * A_HEAD_DIM, (h + 1) * A_HEAD_DIM) for h in range(A_HEADS)]
        ah = [a_ref[pl.ds(r0, SUB), cs] for cs in heads]
        vh = [v_ref[pl.ds(r0, SUB), cs] for cs in heads]
        sc = [lax.dot_general(a2_ref[pl.ds(r0, SUB), heads[h]], b2_ref[pl.ds(r0, SUB), heads[h]], NT_DIMS,
                              preferred_element_type=F32) for h in range(A_HEADS)]
        inter = []
        for h in range(A_HEADS):
            st = st_ref[h]
            inter.append(lax.dot_general(ah[h], st.astype(BF16), NT_DIMS, preferred_element_type=F32))
            dst = lax.dot_general(vh[h], kd_ref[pl.ds(r0, SUB), heads[h]], TN_DIMS, preferred_element_type=F32)
            st_ref[h] = st * e_ref[pl.ds(r0, 1), heads[h]] + dst
        for h in range(A_HEADS):
            pm = jnp.where(causal, sc[h], 0.0).astype(BF16)
            oacc_ref[pl.ds(r0, SUB), heads[h]] = inter[h] + jnp.dot(pm, vh[h], preferred_element_type=F32)
        return carry

    lax.fori_loop(0, TB // SUB, chunk_body, 0, unroll=True)

    gn = gn_ref[...]
    ag = ag_ref[...]
    gate = ag * jax.nn.sigmoid(ag)
    for h in range(A_HEADS):
        cs = slice(h * A_HEAD_DIM, (h + 1) * A_HEAD_DIM)
        o = oacc_ref[:, cs]
        on = o * lax.rsqrt(jnp.mean(o * o, axis=-1, keepdims=True) + EPS) * gn
        o_ref[:, cs] = (on * gate[:, cs]).astype(BF16)


def _hgrn2(p, lb_table, gnorm):
    blk = lambda col: pl.BlockSpec((TB, A_WIDTH), lambda i: (i, col // A_WIDTH))
    return pl.pallas_call(
        _hgrn2_kernel,
        grid=(SEQ // TB,),
        in_specs=[blk(P_AQ), blk(P_AF), blk(P_AI), blk(P_AG),
                  pl.BlockSpec((2, A_WIDTH), lambda i: (0, 0)),
                  pl.BlockSpec((1, A_HEAD_DIM), lambda i: (0, 0))],
        out_specs=pl.BlockSpec((TB, A_WIDTH), lambda i: (i, 0)),
        out_shape=jax.ShapeDtypeStruct((SEQ, A_WIDTH), BF16),
        scratch_shapes=[
            pltpu.VMEM((A_HEADS, A_HEAD_DIM, A_HEAD_DIM), F32),
            pltpu.VMEM((TB, TB), BF16),
            pltpu.VMEM((TB, A_WIDTH), BF16),
            pltpu.VMEM((TB, A_WIDTH), BF16),
            pltpu.VMEM((TB, A_WIDTH), BF16),
            pltpu.VMEM((TB, A_WIDTH), BF16),
            pltpu.VMEM((TB, A_WIDTH), BF16),
            pltpu.VMEM((TB, A_WIDTH), F32),
            pltpu.VMEM((TB, A_WIDTH), F32),
        ],
        compiler_params=pltpu.CompilerParams(
            dimension_semantics=("arbitrary",), vmem_limit_bytes=VMEM_LIMIT),
        name="hgrn2",
    )(p, p, p, p, lb_table, gnorm)


def _t5_bucket(rel):
    n = jnp.abs(rel)
    large = jnp.full(rel.shape, T5_MAX_EXACT, jnp.int32)
    for t in T5_THR:
        large = large + jnp.where(n >= t, 1, 0)
    return jnp.where(rel > 0, T5_HALF, 0) + jnp.where(n < T5_MAX_EXACT, n, large)


def _dsa_kernel(relb_ref, qi_ref, w_ref, q_ref, bg_ref, kidx_ref, k_ref, v_ref, o_ref,
                sc_ref, hi_ref, b0_ref, b1_ref, thr_ref, m_ref, l_ref, acc_ref, s_ref, ms_ref, al_ref):
    qb = pl.program_id(0)

    @pl.when(qb == 0)
    def _():
        def bias_tile(rel, h):
            bucket = _t5_bucket(rel)
            val = jnp.zeros(rel.shape, F32)
            for b in range(REL_BUCKETS):
                val = jnp.where(bucket == b, relb_ref[b, h], val)
            return (val - relb_ref[T5_HALF - 1, h]) * LOG2E

        r0 = lax.broadcasted_iota(jnp.int32, (KT, QB), 0)
        c0 = lax.broadcasted_iota(jnp.int32, (KT, QB), 1)
        r1 = lax.broadcasted_iota(jnp.int32, (128, 128), 0)
        c1 = lax.broadcasted_iota(jnp.int32, (128, 128), 1)
        for h in range(B_HEADS):
            b0_ref[h] = bias_tile(r0 - c0, h)
            b1_ref[h] = bias_tile(r1 - 128 - c1, h)

    def idx_tile(j, diag):
        for ks in range(KT // KSUB):
            kt = kidx_ref[pl.ds(pl.multiple_of(j * KT + ks * KSUB, KSUB), KSUB), :]
            acc = jnp.zeros((KSUB, QB), F32)
            for h in range(IDX_HEADS):
                r = jnp.dot(kt, qi_ref[h], preferred_element_type=F32)
                acc = acc + w_ref[h:h + 1, :] * jnp.maximum(r, 0.0)
            if diag:
                rk = ks * KSUB + lax.broadcasted_iota(jnp.int32, (KSUB, QB), 0)
                ct = lax.broadcasted_iota(jnp.int32, (KSUB, QB), 1)
                acc = jnp.where((rk // CHUNK) <= (ct // CHUNK), acc, NEG_INF)
            sc_ref[j, ks * KSUB:(ks + 1) * KSUB, :] = acc
            hi_ref[j, ks * KSUB:(ks + 1) * KSUB, :] = acc.astype(BF16)

    def idx_body(i, carry):
        idx_tile(2 * i, False)
        idx_tile(2 * i + 1, False)
        return carry

    lax.fori_loop(0, qb // 2, idx_body, 0)

    @pl.when(qb % 2 == 1)
    def _():
        idx_tile(qb - 1, False)
        idx_tile(qb, True)

    @pl.when(qb % 2 == 0)
    def _():
        idx_tile(qb, True)

    npairs = (qb + 2) // 2

    @pl.when(qb % 2 == 0)
    def _():
        sc_ref[qb + 1] = jnp.full((KT, QB), NEG_INF, F32)
        hi_ref[qb + 1] = jnp.full((KT, QB), NEG_INF, BF16)

    def count(ref, thr):
        one = jnp.ones((KT, QB), ref.dtype)
        zero = jnp.zeros((KT, QB), ref.dtype)

        def body(jp, cnt):
            for u in range(2):
                sel = jnp.where(ref[2 * jp + u] >= thr, one, zero)
                for r in range(KT // CNT_ROWS):
                    cnt = cnt + sel[r * CNT_ROWS:(r + 1) * CNT_ROWS]
            return cnt

        cnt = lax.fori_loop(0, npairs, body, jnp.zeros((CNT_ROWS, QB), ref.dtype))
        return jnp.sum(cnt.astype(F32), axis=0, keepdims=True)

    def key16_to_bf16(k):
        bits = jnp.where(k >= 0, k, k ^ 0x7FFF).astype(jnp.int16)
        return lax.bitcast_convert_type(bits, BF16)

    def key32_to_f32(k):
        return lax.bitcast_convert_type(jnp.where(k >= 0, k, k ^ 0x7FFFFFFF), F32)

    c0 = count(hi_ref, jnp.zeros((1, QB), BF16))
    k16 = jnp.where(c0 >= float(TOPK), 0, I16_MIN).astype(jnp.int32)

    def coarse_body(i, k):
        cand = k | jnp.left_shift(jnp.int32(1), 14 - i)
        return jnp.where(count(hi_ref, key16_to_bf16(cand)) >= float(TOPK), cand, k)

    k16 = jnp.maximum(lax.fori_loop(0, 15, coarse_body, k16), KEY16_NEG_INF)
    k_mid = jnp.where(k16 >= 0, k16 << 16, (k16 << 16) | 0xFFFF)
    k_base = jnp.maximum(k_mid, I32_MIN + FINE_BELOW) - FINE_BELOW

    def fine_body(i, carry):
        off, n_ge = carry
        cand = off | jnp.left_shift(jnp.int32(1), FINE_BITS - 1 - i)
        c = count(sc_ref, key32_to_f32(k_base + cand))
        ok = c >= float(TOPK)
        return jnp.where(ok, cand, off), jnp.where(ok, c, n_ge)

    off, n_ge = lax.fori_loop(0, FINE_BITS, fine_body,
                              (jnp.zeros((1, QB), jnp.int32), jnp.full((1, QB), TOPK + 1.0, F32)))
    thr = key32_to_f32(k_base + off)
    thr_ref[...] = thr

    @pl.when(jnp.max(n_ge) > float(TOPK))
    def _():
        rows = lax.broadcasted_iota(jnp.int32, (KT, QB), 0)

        def tile_sum(fn):
            def body(jp, cnt):
                for u in range(2):
                    sel = fn(2 * jp + u)
                    for r in range(KT // CNT_ROWS):
                        cnt = cnt + sel[r * CNT_ROWS:(r + 1) * CNT_ROWS]
                return cnt
            cnt = lax.fori_loop(0, npairs, body, jnp.zeros((CNT_ROWS, QB), F32))
            return jnp.sum(cnt, axis=0, keepdims=True)

        n_gt = tile_sum(lambda j: jnp.where(sc_ref[j] > thr, 1.0, 0.0))
        ties_kept = float(TOPK) - n_gt

        def ties_before(p):
            return tile_sum(lambda j: jnp.where((sc_ref[j] == thr) & (rows < p - j * KT), 1.0, 0.0))

        def idx_body(i, p):
            cand = p | jnp.left_shift(jnp.int32(1), SEQ_BITS - 1 - i)
            return jnp.where(ties_before(cand) < ties_kept, cand, p)

        last = lax.fori_loop(0, SEQ_BITS, idx_body, jnp.zeros((1, QB), jnp.int32))

        def drop_body(j, carry):
            x = sc_ref[j]
            sc_ref[j] = jnp.where((x == thr) & (rows > last - j * KT), NEG_INF, x)
            return carry

        lax.fori_loop(0, 2 * npairs, drop_body, 0)

    m_ref[...] = jnp.full(m_ref.shape, NEG_INF, F32)
    l_ref[...] = jnp.zeros_like(l_ref)
    acc_ref[...] = jnp.zeros_like(acc_ref)
    ones_rows = jnp.ones((SUM_ROWS, KT), BF16)

    def att_step(logits=None, values=None):
        if logits is not None:
            j1, kind, slot1 = logits
            s_t = sc_ref[j1]
            mask = jnp.where((s_t >= thr_ref[...]) & (s_t > NEG_INF), 0.0, NEG_INF)
            k0 = pl.multiple_of(j1 * KT, KT)
        if values is not None:
            j2, slot2 = values
        pvs = []

        def update(h):
            a = al_ref[slot2, h]
            l_ref[h] = a * l_ref[h] + pvs[h][B_HEAD_DIM:B_HEAD_DIM + 1, :]
            acc_ref[h] = a * acc_ref[h] + pvs[h][:B_HEAD_DIM, :]

        for h in range(B_HEADS):
            hs = slice(h * B_HEAD_DIM, (h + 1) * B_HEAD_DIM)
            if logits is not None:
                s = jnp.dot(k_ref[pl.ds(k0, KT), hs], q_ref[hs, :], preferred_element_type=F32)
                if kind == "diag":
                    s = s + b0_ref[h]
                elif kind == "prev":
                    lo = jnp.concatenate([s[128:, :128] + b1_ref[h], s[128:, 128:]], axis=1)
                    s = jnp.concatenate([s[:128, :], lo], axis=0)
                s = s + mask
                s_ref[slot1, h] = s
                m_prev = m_ref[h]
                m_next = jnp.maximum(m_prev, jnp.max(s, axis=0, keepdims=True))
                m_safe = jnp.where(m_next == NEG_INF, 0.0, m_next)
                ms_ref[slot1, h] = m_safe
                al_ref[slot1, h] = jnp.exp2(m_prev - m_safe)
                m_ref[h] = m_next
            if values is not None:
                p = jnp.exp2(s_ref[slot2, h] - ms_ref[slot2, h]).astype(BF16)
                pvs.append(jnp.dot(jnp.concatenate([v_ref[j2, hs, :], ones_rows], axis=0), p,
                                   preferred_element_type=F32))
                if h >= UPDATE_LAG:
                    update(h - UPDATE_LAG)
        if values is not None:
            for h in range(B_HEADS - UPDATE_LAG, B_HEADS):
                update(h)

    odd = qb % 2

    @pl.when(qb == 0)
    def _():
        att_step(logits=(0, "diag", 0))

    @pl.when(qb == 1)
    def _():
        att_step(logits=(0, "prev", 1))

    @pl.when(qb >= 2)
    def _():
        @pl.when(odd == 0)
        def _():
            att_step(logits=(0, "far", 0))

        @pl.when(odd == 1)
        def _():
            att_step(logits=(0, "far", 1))
            att_step(logits=(1, "far", 0), values=(0, 1))

        def att_body(i, carry):
            j = odd + 2 * i
            att_step(logits=(j + 1, "far", 1), values=(j, 0))
            att_step(logits=(j + 2, "far", 0), values=(j + 1, 1))
            return carry

        lax.fori_loop(0, (qb - 2) // 2, att_body, 0)
        att_step(logits=(qb - 1, "prev", 1), values=(qb - 2, 0))

    @pl.when(qb >= 1)
    def _():
        att_step(logits=(qb, "diag", 0), values=(qb - 1, 1))

    att_step(values=(qb, 0))

    for h in range(B_HEADS):
        hs = slice(h * B_HEAD_DIM, (h + 1) * B_HEAD_DIM)
        bg = bg_ref[:, hs]
        o_ref[:, hs] = ((acc_ref[h] / l_ref[h]).T * (bg * jax.nn.sigmoid(bg))).astype(BF16)


def _dsa(rel_bias, q_idx_t, w_idx_t, q_t, p, k_idx, k, v_t):
    single = pl.Buffered(1)
    return pl.pallas_call(
        _dsa_kernel,
        grid=(SEQ // QB,),
        in_specs=[
            pl.BlockSpec(memory_space=pltpu.SMEM),
            pl.BlockSpec((IDX_HEADS, IDX_DIM, QB), lambda i: (0, 0, i)),
            pl.BlockSpec((128, QB), lambda i: (0, i)),
            pl.BlockSpec((B_WIDTH, QB), lambda i: (0, i)),
            pl.BlockSpec((QB, B_WIDTH), lambda i: (i, P_BG // B_WIDTH)),
            pl.BlockSpec((SEQ, IDX_DIM), lambda i: (0, 0), pipeline_mode=single),
            pl.BlockSpec((SEQ, B_WIDTH), lambda i: (0, 0), pipeline_mode=single),
            pl.BlockSpec((NKT, B_WIDTH, KT), lambda i: (0, 0, 0), pipeline_mode=single),
        ],
        out_specs=pl.BlockSpec((QB, B_WIDTH), lambda i: (i, 0)),
        out_shape=jax.ShapeDtypeStruct((SEQ, B_WIDTH), BF16),
        scratch_shapes=[
            pltpu.VMEM((NKT, KT, QB), F32),
            pltpu.VMEM((NKT, KT, QB), BF16),
            pltpu.VMEM((B_HEADS, KT, QB), F32),
            pltpu.VMEM((B_HEADS, 128, 128), F32),
            pltpu.VMEM((1, QB), F32),
            pltpu.VMEM((B_HEADS, 1, QB), F32),
            pltpu.VMEM((B_HEADS, 1, QB), F32),
            pltpu.VMEM((B_HEADS, B_HEAD_DIM, QB), F32),
            pltpu.VMEM((2, B_HEADS, KT, QB), F32),
            pltpu.VMEM((2, B_HEADS, 1, QB), F32),
            pltpu.VMEM((2, B_HEADS, 1, QB), F32),
        ],
        compiler_params=pltpu.CompilerParams(
            dimension_semantics=("arbitrary",), vmem_limit_bytes=VMEM_LIMIT),
        name="dsa",
    )(rel_bias, q_idx_t, w_idx_t, q_t, p, k_idx, k, v_t)


def _out_proj_kernel(ya_ref, yb_ref, ma_ref, mb_ref, x_ref, wpa_ref, wpb_ref, wo_ref, fnw_ref, o_ref):
    pa = jnp.dot(ya_ref[...], wpa_ref[...], preferred_element_type=F32)
    pb = jnp.dot(yb_ref[...], wpb_ref[...], preferred_element_type=F32)
    merged = jax.nn.sigmoid(ma_ref[...]) * pa + jax.nn.sigmoid(mb_ref[...]) * pb
    y = x_ref[...] + jnp.dot(merged.astype(BF16), wo_ref[...], preferred_element_type=F32)
    o_ref[...] = _rms(y, fnw_ref[...])


def _out_proj(y_a, y_b, p, x2, w_pa, w_pb, w_out, final_norm_w):
    tm = 512
    const = lambda i: (0, 0)
    single = pl.Buffered(1)
    return pl.pallas_call(
        _out_proj_kernel,
        grid=(SEQ // tm,),
        in_specs=[
            pl.BlockSpec((tm, A_WIDTH), lambda i: (i, 0)),
            pl.BlockSpec((tm, B_WIDTH), lambda i: (i, 0)),
            pl.BlockSpec((tm, D_MODEL), lambda i: (i, P_MA // D_MODEL)),
            pl.BlockSpec((tm, D_MODEL), lambda i: (i, P_MB // D_MODEL)),
            pl.BlockSpec((tm, D_MODEL), lambda i: (i, 0)),
            pl.BlockSpec((A_WIDTH, D_MODEL), const, pipeline_mode=single),
            pl.BlockSpec((B_WIDTH, D_MODEL), const, pipeline_mode=single),
            pl.BlockSpec((D_MODEL, D_MODEL), const, pipeline_mode=single),
            pl.BlockSpec((1, D_MODEL), const),
        ],
        out_specs=pl.BlockSpec((tm, D_MODEL), lambda i: (i, 0)),
        out_shape=jax.ShapeDtypeStruct((SEQ, D_MODEL), F32),
        compiler_params=pltpu.CompilerParams(
            dimension_semantics=("arbitrary",), vmem_limit_bytes=VMEM_LIMIT),
        name="out_proj",
    )(y_a, y_b, p, p, x2, w_pa, w_pb, w_out, final_norm_w)


def kernel(x, norm_w, w_in, lb_table, gnorm_a, q_norm_w, kv_norm_w, w_uq, w_qidx, w_ukv, kidx_norm_w,
           kidx_norm_b, w_pa, w_pb, w_out, rel_bias, final_norm_w):
    assert x.shape == (1, SEQ, D_MODEL) and w_in.shape[0] == 1
    x2 = x.reshape(SEQ, D_MODEL)
    p = _in_proj(x2, norm_w, w_in[0].astype(BF16))
    w_kv = w_ukv[0].astype(BF16).reshape(KV_RANK, B_HEADS, 2, B_HEAD_DIM)
    w_k = w_kv[:, :, 0, :].reshape(KV_RANK, B_WIDTH)
    w_v_t = w_kv[:, :, 1, :].reshape(KV_RANK, B_WIDTH).T
    q_t, q_idx_t, k, v_t, k_idx, w_idx_t = _proj2(
        p, q_norm_w, kv_norm_w, w_uq[0].astype(BF16).T, w_qidx[0].astype(BF16).T, w_k, w_v_t,
        kidx_norm_w, kidx_norm_b)
    y_a = _hgrn2(p, lb_table, gnorm_a)
    y_b = _dsa(rel_bias, q_idx_t, w_idx_t, q_t, p, k_idx, k, v_t)
    out = _out_proj(y_a, y_b, p, x2, w_pa[0].astype(BF16), w_pb[0].astype(BF16), w_out[0].astype(BF16),
                    final_norm_w.reshape(1, D_MODEL))
    return out.reshape(1, SEQ, D_MODEL)
```

```python
import math

import jax
import jax.numpy as jnp
import numpy as np
from jax import lax
from jax.experimental import pallas as pl
from jax.experimental.pallas import tpu as pltpu

F32 = jnp.float32
BF16 = jnp.bfloat16

D_MODEL = 2048
SEQ = 8192
EPS = 1e-6
A_HEADS = 8
A_HEAD_DIM = 128
A_WIDTH = 1024
B_HEADS = 8
B_HEAD_DIM = 128
B_WIDTH = 1024
Q_RANK = 512
KV_RANK = 256
IDX_HEADS = 16
IDX_DIM = 128
TOPK = 256
CHUNK = 64
REL_BUCKETS = 32
REL_MAX_DIST = 128

P_AQ, P_AF, P_AI, P_AG = 0, 1024, 2048, 3072
P_CQ = 4096
P_MISC = 4608
P_BG = 5120
P_MA = 6144
P_MB = 8192
P_WIDTH = 10240
IN_WIDTH = 10128
IN_TN = 1024
IN_SHIFT = 112

VMEM_LIMIT = 63 * 1024 * 1024

QB = 256
KT = 256
NKT = SEQ // KT
KSUB = 128
CNT_ROWS = 64
SUM_ROWS = 16
UPDATE_LAG = 2

SUB = 32
MID = SUB // 2
TB = 256

NEG_INF = float("-inf")
LOG2E = math.log2(math.e)
I16_MIN = -32768
I32_MIN = -2147483648
KEY16_NEG_INF = -32641
SEQ_BITS = 13
assert SEQ == 1 << SEQ_BITS
FINE_BELOW = 1 << 16
FINE_BITS = 17

NT_DIMS = (((1,), (1,)), ((), ()))
TN_DIMS = (((0,), (0,)), ((), ()))


def _t5_thresholds():
    half = REL_BUCKETS // 2
    max_exact = half // 2
    n = np.arange(1, 4 * REL_MAX_DIST, dtype=np.int64)
    large = max_exact + (np.log(np.maximum(n, 1).astype(np.float64) / max_exact)
                         / math.log(REL_MAX_DIST / max_exact) * (half - max_exact)).astype(np.int32)
    large = np.minimum(large, half - 1)
    b = np.where(n < max_exact, n, large)
    thr = [int(n[i + 1]) for i in np.nonzero(np.diff(b))[0] if n[i + 1] > max_exact]
    assert len(thr) == half - 1 - max_exact and b[-1] == half - 1 and thr[-1] < REL_MAX_DIST
    return max_exact, half, thr


T5_MAX_EXACT, T5_HALF, T5_THR = _t5_thresholds()


def _rms_h_kernel(x_ref, nw_ref, h_ref):
    h_ref[...] = _rms(x_ref[...], nw_ref[...]).astype(BF16)


def _rms_h(x2, norm_w):
    tm = 512
    return pl.pallas_call(
        _rms_h_kernel,
        grid=(SEQ // tm,),
        in_specs=[pl.BlockSpec((tm, D_MODEL), lambda i: (i, 0)),
                  pl.BlockSpec((1, D_MODEL), lambda i: (0, 0))],
        out_specs=pl.BlockSpec((tm, D_MODEL), lambda i: (i, 0)),
        out_shape=jax.ShapeDtypeStruct((SEQ, D_MODEL), BF16),
        compiler_params=pltpu.CompilerParams(
            dimension_semantics=("arbitrary",), vmem_limit_bytes=VMEM_LIMIT),
        name="rms_h",
    )(x2, norm_w)


def _in_proj_kernel(h_ref, w_ref, o_ref, wb_ref):
    j = pl.program_id(0)

    @pl.when(pl.program_id(1) == 0)
    def _():
        @pl.when(j < P_BG // IN_TN)
        def _():
            wb_ref[...] = w_ref[:, :IN_TN].astype(BF16)

        @pl.when(j >= P_BG // IN_TN)
        def _():
            w = w_ref[...].astype(BF16)
            wb_ref[...] = pltpu.roll(w, IN_TN + 128 - (128 - IN_SHIFT), axis=1)[:, :IN_TN]

    o_ref[...] = jnp.dot(h_ref[...], wb_ref[...], preferred_element_type=F32)


def _in_proj(h, w_in):
    tm, tn = 1024, IN_TN
    n_plain = P_BG // tn

    def w_index(j, i):
        return 0, pl.multiple_of(jnp.where(j < n_plain, j * tn, j * tn - 128), 128)

    return pl.pallas_call(
        _in_proj_kernel,
        grid=(P_WIDTH // tn, SEQ // tm),
        in_specs=[
            pl.BlockSpec((tm, D_MODEL), lambda j, i: (i, 0)),
            pl.BlockSpec((pl.Element(D_MODEL), pl.Element(tn + 128, padding=(0, P_WIDTH - IN_WIDTH))), w_index),
        ],
        out_specs=pl.BlockSpec((tm, tn), lambda j, i: (i, j)),
        out_shape=jax.ShapeDtypeStruct((SEQ, P_WIDTH), F32),
        scratch_shapes=[pltpu.VMEM((D_MODEL, tn), BF16)],
        compiler_params=pltpu.CompilerParams(
            dimension_semantics=("arbitrary", "arbitrary"), vmem_limit_bytes=VMEM_LIMIT),
        name="in_proj",
    )(h, w_in)


def _rms(x, w):
    return x * lax.rsqrt(jnp.mean(x * x, axis=-1, keepdims=True) + EPS) * w


def _proj2_kernel(cq_ref, misc_ref, qnw_ref, kvnw_ref, wuqt_ref, wqit_ref, wk_ref, wvt_ref, lnw_ref, lnb_ref,
                  qt_ref, qit_ref, k_ref, vt_ref, kidx_ref, wt_ref):
    cqn = _rms(cq_ref[...], qnw_ref[...]).astype(BF16)
    qt = lax.dot_general(wuqt_ref[...], cqn, NT_DIMS, preferred_element_type=F32)
    qt_ref[...] = (qt * (B_HEAD_DIM ** -0.5 * LOG2E)).astype(BF16)
    qit = lax.dot_general(wqit_ref[...], cqn, NT_DIMS, preferred_element_type=F32)
    for h in range(IDX_HEADS):
        qit_ref[h] = qit[h * IDX_DIM:(h + 1) * IDX_DIM, :].astype(BF16)
    misc = misc_ref[...]
    ckvn = _rms(misc[:, :KV_RANK], kvnw_ref[...]).astype(BF16)
    k_ref[...] = jnp.dot(ckvn, wk_ref[...], preferred_element_type=F32).astype(BF16)
    vt = lax.dot_general(wvt_ref[...], ckvn, NT_DIMS, preferred_element_type=F32).astype(BF16)
    for t in range(vt_ref.shape[0]):
        vt_ref[t] = vt[:, t * KT:(t + 1) * KT]
    kr = misc[:, KV_RANK:KV_RANK + IDX_DIM]
    mu = jnp.mean(kr, axis=-1, keepdims=True)
    var = jnp.mean(jnp.square(kr - mu), axis=-1, keepdims=True)
    kidx_ref[...] = ((kr - mu) * lax.rsqrt(var + EPS) * lnw_ref[...] + lnb_ref[...]).astype(BF16)
    wt_ref[...] = (misc[:, KV_RANK + IDX_DIM:] * (IDX_HEADS ** -0.5 * IDX_DIM ** -0.5)).T


def _proj2(p, q_norm_w, kv_norm_w, w_uq_t, w_qidx_t, w_k, w_v_t, ln_w, ln_b):
    tm = 2 * KT
    const = lambda i: (0, 0)
    return pl.pallas_call(
        _proj2_kernel,
        grid=(SEQ // tm,),
        in_specs=[
            pl.BlockSpec((tm, 512), lambda i: (i, P_CQ // 512)),
            pl.BlockSpec((tm, 512), lambda i: (i, P_MISC // 512)),
            pl.BlockSpec((1, Q_RANK), const),
            pl.BlockSpec((1, KV_RANK), const),
            pl.BlockSpec((B_WIDTH, Q_RANK), const),
            pl.BlockSpec((IDX_HEADS * IDX_DIM, Q_RANK), const),
            pl.BlockSpec((KV_RANK, B_WIDTH), const),
            pl.BlockSpec((B_WIDTH, KV_RANK), const),
            pl.BlockSpec((1, IDX_DIM), const),
            pl.BlockSpec((1, IDX_DIM), const),
        ],
        out_specs=[
            pl.BlockSpec((B_WIDTH, tm), lambda i: (0, i)),
            pl.BlockSpec((IDX_HEADS, IDX_DIM, tm), lambda i: (0, 0, i)),
            pl.BlockSpec((tm, B_WIDTH), lambda i: (i, 0)),
            pl.BlockSpec((tm // KT, B_WIDTH, KT), lambda i: (i, 0, 0)),
            pl.BlockSpec((tm, IDX_DIM), lambda i: (i, 0)),
            pl.BlockSpec((128, tm), lambda i: (0, i)),
        ],
        out_shape=[
            jax.ShapeDtypeStruct((B_WIDTH, SEQ), BF16),
            jax.ShapeDtypeStruct((IDX_HEADS, IDX_DIM, SEQ), BF16),
            jax.ShapeDtypeStruct((SEQ, B_WIDTH), BF16),
            jax.ShapeDtypeStruct((NKT, B_WIDTH, KT), BF16),
            jax.ShapeDtypeStruct((SEQ, IDX_DIM), BF16),
            jax.ShapeDtypeStruct((128, SEQ), F32),
        ],
        compiler_params=pltpu.CompilerParams(
            dimension_semantics=("arbitrary",), vmem_limit_bytes=VMEM_LIMIT),
        name="proj2",
    )(p, p, q_norm_w, kv_norm_w, w_uq_t, w_qidx_t, w_k, w_v_t, ln_w, ln_b)


def _split3(x):
    x1 = x.astype(BF16)
    r1 = x - x1.astype(F32)
    x2 = r1.astype(BF16)
    x3 = (r1 - x2.astype(F32)).astype(BF16)
    return x1, x2, x3


def _hgrn2_kernel(aq_ref, af_ref, ai_ref, ag_ref, lbt_ref, gn_ref, o_ref,
                  st_ref, sel_ref, a_ref, a2_ref, b2_ref, kd_ref, v_ref, e_ref, oacc_ref):
    @pl.when(pl.program_id(0) == 0)
    def _():
        st_ref[...] = jnp.zeros_like(st_ref)
        r = lax.broadcasted_iota(jnp.int32, (TB, TB), 0)
        c = lax.broadcasted_iota(jnp.int32, (TB, TB), 1)
        same = (r // SUB) == (c // SUB)
        sel_ref[...] = jnp.where(same & (c <= r), 1.0, 0.0).astype(BF16)

    lbt = lbt_ref[...]
    mx = jnp.maximum(lbt[0:1], lbt[1:2])
    e0 = jnp.exp(lbt[0:1] - mx)
    e1 = jnp.exp(lbt[1:2] - mx)
    lb = e0 / (e0 + e1)

    f = lb + (1.0 - lb) * jax.nn.sigmoid(af_ref[...])
    g = jnp.log(f)
    sel = sel_ref[...]
    g1, g2, g3 = _split3(g)
    cum = (jnp.dot(sel, g1, preferred_element_type=F32) + jnp.dot(sel, g2, preferred_element_type=F32)
           + jnp.dot(sel, g3, preferred_element_type=F32))
    cum3 = cum.reshape(TB // SUB, SUB, A_WIDTH)
    mid = jnp.broadcast_to(cum3[:, MID - 1:MID, :], cum3.shape).reshape(TB, A_WIDTH)
    tot = jnp.broadcast_to(cum3[:, SUB - 1:SUB, :], cum3.shape).reshape(TB, A_WIDTH)
    aq = aq_ref[...]
    q = aq * jax.nn.sigmoid(aq) * (A_HEAD_DIM ** -0.5)
    kk = 1.0 - f
    a_ref[...] = (q * jnp.exp(cum)).astype(BF16)
    a2_ref[...] = (q * jnp.exp(cum - mid)).astype(BF16)
    b2_ref[...] = (kk * jnp.exp(mid - cum)).astype(BF16)
    kd_ref[...] = (kk * jnp.exp(tot - cum)).astype(BF16)
    v_ref[...] = ai_ref[...].astype(BF16)
    e_ref[...] = jnp.exp(tot)

    tr = lax.broadcasted_iota(jnp.int32, (SUB, SUB), 0)
    tc = lax.broadcasted_iota(jnp.int32, (SUB, SUB), 1)
    causal = tc <= tr

    def chunk_body(c, carry):
        r0 = pl.multiple_of(c * SUB, SUB)
        heads = [slice(h * A_HEAD_DIM, (h + 1) * A_HEAD_DIM) for h in range(A_HEADS)]
        ah = [a_ref[pl.ds(r0, SUB), cs] for cs in heads]
        vh = [v_ref[pl.ds(r0, SUB), cs] for cs in heads]
        sc = [lax.dot_general(a2_ref[pl.ds(r0, SUB), heads[h]], b2_ref[pl.ds(r0, SUB), heads[h]], NT_DIMS,
                              preferred_element_type=F32) for h in range(A_HEADS)]
        inter = []
        for h in range(A_HEADS):
            st = st_ref[h]
            inter.append(lax.dot_general(ah[h], st.astype(BF16), NT_DIMS, preferred_element_type=F32))
            dst = lax.dot_general(vh[h], kd_ref[pl.ds(r0, SUB), heads[h]], TN_DIMS, preferred_element_type=F32)
            st_ref[h] = st * e_ref[pl.ds(r0, 1), heads[h]] + dst
        for h in range(A_HEADS):
            pm = jnp.where(causal, sc[h], 0.0).astype(BF16)
            oacc_ref[pl.ds(r0, SUB), heads[h]] = inter[h] + jnp.dot(pm, vh[h], preferred_element_type=F32)
        return carry

    lax.fori_loop(0, TB // SUB, chunk_body, 0, unroll=True)

    gn = gn_ref[...]
    ag = ag_ref[...]
    gate = ag * jax.nn.sigmoid(ag)
    for h in range(A_HEADS):
        cs = slice(h * A_HEAD_DIM, (h + 1) * A_HEAD_DIM)
        o = oacc_ref[:, cs]
        on = o * lax.rsqrt(jnp.mean(o * o, axis=-1, keepdims=True) + EPS) * gn
        o_ref[:, cs] = (on * gate[:, cs]).astype(BF16)


def _hgrn2(p, lb_table, gnorm):
    blk = lambda col: pl.BlockSpec((TB, A_WIDTH), lambda i: (i, col // A_WIDTH))
    return pl.pallas_call(
        _hgrn2_kernel,
        grid=(SEQ // TB,),
        in_specs=[blk(P_AQ), blk(P_AF), blk(P_AI), blk(P_AG),
                  pl.BlockSpec((2, A_WIDTH), lambda i: (0, 0)),
                  pl.BlockSpec((1, A_HEAD_DIM), lambda i: (0, 0))],
        out_specs=pl.BlockSpec((TB, A_WIDTH), lambda i: (i, 0)),
        out_shape=jax.ShapeDtypeStruct((SEQ, A_WIDTH), BF16),
        scratch_shapes=[
            pltpu.VMEM((A_HEADS, A_HEAD_DIM, A_HEAD_DIM), F32),
            pltpu.VMEM((TB, TB), BF16),
            pltpu.VMEM((TB, A_WIDTH), BF16),
            pltpu.VMEM((TB, A_WIDTH), BF16),
            pltpu.VMEM((TB, A_WIDTH), BF16),
            pltpu.VMEM((TB, A_WIDTH), BF16),
            pltpu.VMEM((TB, A_WIDTH), BF16),
            pltpu.VMEM((TB, A_WIDTH), F32),
            pltpu.VMEM((TB, A_WIDTH), F32),
        ],
        compiler_params=pltpu.CompilerParams(
            dimension_semantics=("arbitrary",), vmem_limit_bytes=VMEM_LIMIT),
        name="hgrn2",
    )(p, p, p, p, lb_table, gnorm)


def _t5_bucket(rel):
    n = jnp.abs(rel)
    large = jnp.full(rel.shape, T5_MAX_EXACT, jnp.int32)
    for t in T5_THR:
        large = large + jnp.where(n >= t, 1, 0)
    return jnp.where(rel > 0, T5_HALF, 0) + jnp.where(n < T5_MAX_EXACT, n, large)


def _dsa_kernel(relb_ref, qi_ref, w_ref, q_ref, bg_ref, kidx_ref, k_ref, v_ref, o_ref,
                sc_ref, hi_ref, b0_ref, b1_ref, thr_ref, m_ref, l_ref, acc_ref, s_ref, ms_ref, al_ref):
    qb = pl.program_id(0)

    @pl.when(qb == 0)
    def _():
        def bias_tile(rel, h):
            bucket = _t5_bucket(rel)
            val = jnp.zeros(rel.shape, F32)
            for b in range(REL_BUCKETS):
                val = jnp.where(bucket == b, relb_ref[b, h], val)
            return (val - relb_ref[T5_HALF - 1, h]) * LOG2E

        r0 = lax.broadcasted_iota(jnp.int32, (KT, QB), 0)
        c0 = lax.broadcasted_iota(jnp.int32, (KT, QB), 1)
        r1 = lax.broadcasted_iota(jnp.int32, (128, 128), 0)
        c1 = lax.broadcasted_iota(jnp.int32, (128, 128), 1)
        for h in range(B_HEADS):
            b0_ref[h] = bias_tile(r0 - c0, h)
            b1_ref[h] = bias_tile(r1 - 128 - c1, h)

    def idx_tile(j, diag):
        for ks in range(KT // KSUB):
            kt = kidx_ref[pl.ds(pl.multiple_of(j * KT + ks * KSUB, KSUB), KSUB), :]
            acc = jnp.zeros((KSUB, QB), F32)
            for h in range(IDX_HEADS):
                r = jnp.dot(kt, qi_ref[h], preferred_element_type=F32)
                acc = acc + w_ref[h:h + 1, :] * jnp.maximum(r, 0.0)
            if diag:
                rk = ks * KSUB + lax.broadcasted_iota(jnp.int32, (KSUB, QB), 0)
                ct = lax.broadcasted_iota(jnp.int32, (KSUB, QB), 1)
                acc = jnp.where((rk // CHUNK) <= (ct // CHUNK), acc, NEG_INF)
            sc_ref[j, ks * KSUB:(ks + 1) * KSUB, :] = acc
            hi_ref[j, ks * KSUB:(ks + 1) * KSUB, :] = acc.astype(BF16)

    def idx_body(i, carry):
        idx_tile(2 * i, False)
        idx_tile(2 * i + 1, False)
        return carry

    lax.fori_loop(0, qb // 2, idx_body, 0)

    @pl.when(qb % 2 == 1)
    def _():
        idx_tile(qb - 1, False)
        idx_tile(qb, True)

    @pl.when(qb % 2 == 0)
    def _():
        idx_tile(qb, True)

    npairs = (qb + 2) // 2

    @pl.when(qb % 2 == 0)
    def _():
        sc_ref[qb + 1] = jnp.full((KT, QB), NEG_INF, F32)
        hi_ref[qb + 1] = jnp.full((KT, QB), NEG_INF, BF16)

    def count(ref, thr):
        one = jnp.ones((KT, QB), ref.dtype)
        zero = jnp.zeros((KT, QB), ref.dtype)

        def body(jp, cnt):
            for u in range(2):
                sel = jnp.where(ref[2 * jp + u] >= thr, one, zero)
                for r in range(KT // CNT_ROWS):
                    cnt = cnt + sel[r * CNT_ROWS:(r + 1) * CNT_ROWS]
            return cnt

        cnt = lax.fori_loop(0, npairs, body, jnp.zeros((CNT_ROWS, QB), ref.dtype))
        return jnp.sum(cnt.astype(F32), axis=0, keepdims=True)

    def key16_to_bf16(k):
        bits = jnp.where(k >= 0, k, k ^ 0x7FFF).astype(jnp.int16)
        return lax.bitcast_convert_type(bits, BF16)

    def key32_to_f32(k):
        return lax.bitcast_convert_type(jnp.where(k >= 0, k, k ^ 0x7FFFFFFF), F32)

    c0 = count(hi_ref, jnp.zeros((1, QB), BF16))
    k16 = jnp.where(c0 >= float(TOPK), 0, I16_MIN).astype(jnp.int32)

    def coarse_body(i, k):
        cand = k | jnp.left_shift(jnp.int32(1), 14 - i)
        return jnp.where(count(hi_ref, key16_to_bf16(cand)) >= float(TOPK), cand, k)

    k16 = jnp.maximum(lax.fori_loop(0, 15, coarse_body, k16), KEY16_NEG_INF)
    k_mid = jnp.where(k16 >= 0, k16 << 16, (k16 << 16) | 0xFFFF)
    k_base = jnp.maximum(k_mid, I32_MIN + FINE_BELOW) - FINE_BELOW

    def fine_body(i, carry):
        off, n_ge = carry
        cand = off | jnp.left_shift(jnp.int32(1), FINE_BITS - 1 - i)
        c = count(sc_ref, key32_to_f32(k_base + cand))
        ok = c >= float(TOPK)
        return jnp.where(ok, cand, off), jnp.where(ok, c, n_ge)

    off, n_ge = lax.fori_loop(0, FINE_BITS, fine_body,
                              (jnp.zeros((1, QB), jnp.int32), jnp.full((1, QB), TOPK + 1.0, F32)))
    thr = key32_to_f32(k_base + off)
    thr_ref[...] = thr

    @pl.when(jnp.max(n_ge) > float(TOPK))
    def _():
        rows = lax.broadcasted_iota(jnp.int32, (KT, QB), 0)

        def tile_sum(fn):
            def body(jp, cnt):
                for u in range(2):
                    sel = fn(2 * jp + u)
                    for r in range(KT // CNT_ROWS):
                        cnt = cnt + sel[r * CNT_ROWS:(r + 1) * CNT_ROWS]
                return cnt
            cnt = lax.fori_loop(0, npairs, body, jnp.zeros((CNT_ROWS, QB), F32))
            return jnp.sum(cnt, axis=0, keepdims=True)

        n_gt = tile_sum(lambda j: jnp.where(sc_ref[j] > thr, 1.0, 0.0))
        ties_kept = float(TOPK) - n_gt

        def ties_before(p):
            return tile_sum(lambda j: jnp.where((sc_ref[j] == thr) & (rows < p - j * KT), 1.0, 0.0))

        def idx_body(i, p):
            cand = p | jnp.left_shift(jnp.int32(1), SEQ_BITS - 1 - i)
            return jnp.where(ties_before(cand) < ties_kept, cand, p)

        last = lax.fori_loop(0, SEQ_BITS, idx_body, jnp.zeros((1, QB), jnp.int32))

        def drop_body(j, carry):
            x = sc_ref[j]
            sc_ref[j] = jnp.where((x == thr) & (rows > last - j * KT), NEG_INF, x)
            return carry

        lax.fori_loop(0, 2 * npairs, drop_body, 0)

    m_ref[...] = jnp.full(m_ref.shape, NEG_INF, F32)
    l_ref[...] = jnp.zeros_like(l_ref)
    acc_ref[...] = jnp.zeros_like(acc_ref)
    ones_rows = jnp.ones((SUM_ROWS, KT), BF16)

    def att_step(logits=None, values=None):
        if logits is not None:
            j1, kind, slot1 = logits
            s_t = sc_ref[j1]
            mask = jnp.where((s_t >= thr_ref[...]) & (s_t > NEG_INF), 0.0, NEG_INF)
            k0 = pl.multiple_of(j1 * KT, KT)
        if values is not None:
            j2, slot2 = values
        pvs = []

        def update(h):
            a = al_ref[slot2, h]
            l_ref[h] = a * l_ref[h] + pvs[h][B_HEAD_DIM:B_HEAD_DIM + 1, :]
            acc_ref[h] = a * acc_ref[h] + pvs[h][:B_HEAD_DIM, :]

        for h in range(B_HEADS):
            hs = slice(h * B_HEAD_DIM, (h + 1) * B_HEAD_DIM)
            if logits is not None:
                s = jnp.dot(k_ref[pl.ds(k0, KT), hs], q_ref[hs, :], preferred_element_type=F32)
                if kind == "diag":
                    s = s + b0_ref[h]
                elif kind == "prev":
                    lo = jnp.concatenate([s[128:, :128] + b1_ref[h], s[128:, 128:]], axis=1)
                    s = jnp.concatenate([s[:128, :], lo], axis=0)
                s = s + mask
                s_ref[slot1, h] = s
                m_prev = m_ref[h]
                m_next = jnp.maximum(m_prev, jnp.max(s, axis=0, keepdims=True))
                m_safe = jnp.where(m_next == NEG_INF, 0.0, m_next)
                ms_ref[slot1, h] = m_safe
                al_ref[slot1, h] = jnp.exp2(m_prev - m_safe)
                m_ref[h] = m_next
            if values is not None:
                p = jnp.exp2(s_ref[slot2, h] - ms_ref[slot2, h]).astype(BF16)
                pvs.append(jnp.dot(jnp.concatenate([v_ref[j2, hs, :], ones_rows], axis=0), p,
                                   preferred_element_type=F32))
                if h >= UPDATE_LAG:
                    update(h - UPDATE_LAG)
        if values is not None:
            for h in range(B_HEADS - UPDATE_LAG, B_HEADS):
                update(h)

    odd = qb % 2

    @pl.when(qb == 0)
    def _():
        att_step(logits=(0, "diag", 0))

    @pl.when(qb == 1)
    def _():
        att_step(logits=(0, "prev", 1))

    @pl.when(qb >= 2)
    def _():
        @pl.when(odd == 0)
        def _():
            att_step(logits=(0, "far", 0))

        @pl.when(odd == 1)
        def _():
            att_step(logits=(0, "far", 1))
            att_step(logits=(1, "far", 0), values=(0, 1))

        def att_body(i, carry):
            j = odd + 2 * i
            att_step(logits=(j + 1, "far", 1), values=(j, 0))
            att_step(logits=(j + 2, "far", 0), values=(j + 1, 1))
            return carry

        lax.fori_loop(0, (qb - 2) // 2, att_body, 0)
        att_step(logits=(qb - 1, "prev", 1), values=(qb - 2, 0))

    @pl.when(qb >= 1)
    def _():
        att_step(logits=(qb, "diag", 0), values=(qb - 1, 1))

    att_step(values=(qb, 0))

    for h in range(B_HEADS):
        hs = slice(h * B_HEAD_DIM, (h + 1) * B_HEAD_DIM)
        bg = bg_ref[:, hs]
        o_ref[:, hs] = ((acc_ref[h] / l_ref[h]).T * (bg * jax.nn.sigmoid(bg))).astype(BF16)


def _dsa(rel_bias, q_idx_t, w_idx_t, q_t, p, k_idx, k, v_t):
    single = pl.Buffered(1)
    return pl.pallas_call(
        _dsa_kernel,
        grid=(SEQ // QB,),
        in_specs=[
            pl.BlockSpec(memory_space=pltpu.SMEM),
            pl.BlockSpec((IDX_HEADS, IDX_DIM, QB), lambda i: (0, 0, i)),
            pl.BlockSpec((128, QB), lambda i: (0, i)),
            pl.BlockSpec((B_WIDTH, QB), lambda i: (0, i)),
            pl.BlockSpec((QB, B_WIDTH), lambda i: (i, P_BG // B_WIDTH)),
            pl.BlockSpec((SEQ, IDX_DIM), lambda i: (0, 0), pipeline_mode=single),
            pl.BlockSpec((SEQ, B_WIDTH), lambda i: (0, 0), pipeline_mode=single),
            pl.BlockSpec((NKT, B_WIDTH, KT), lambda i: (0, 0, 0), pipeline_mode=single),
        ],
        out_specs=pl.BlockSpec((QB, B_WIDTH), lambda i: (i, 0)),
        out_shape=jax.ShapeDtypeStruct((SEQ, B_WIDTH), BF16),
        scratch_shapes=[
            pltpu.VMEM((NKT, KT, QB), F32),
            pltpu.VMEM((NKT, KT, QB), BF16),
            pltpu.VMEM((B_HEADS, KT, QB), F32),
            pltpu.VMEM((B_HEADS, 128, 128), F32),
            pltpu.VMEM((1, QB), F32),
            pltpu.VMEM((B_HEADS, 1, QB), F32),
            pltpu.VMEM((B_HEADS, 1, QB), F32),
            pltpu.VMEM((B_HEADS, B_HEAD_DIM, QB), F32),
            pltpu.VMEM((2, B_HEADS, KT, QB), F32),
            pltpu.VMEM((2, B_HEADS, 1, QB), F32),
            pltpu.VMEM((2, B_HEADS, 1, QB), F32),
        ],
        compiler_params=pltpu.CompilerParams(
            dimension_semantics=("arbitrary",), vmem_limit_bytes=VMEM_LIMIT),
        name="dsa",
    )(rel_bias, q_idx_t, w_idx_t, q_t, p, k_idx, k, v_t)


def _out_proj_kernel(ya_ref, yb_ref, ma_ref, mb_ref, x_ref, wpa_ref, wpb_ref, wo_ref, fnw_ref, o_ref):
    pa = jnp.dot(ya_ref[...], wpa_ref[...], preferred_element_type=F32)
    pb = jnp.dot(yb_ref[...], wpb_ref[...], preferred_element_type=F32)
    merged = jax.nn.sigmoid(ma_ref[...]) * pa + jax.nn.sigmoid(mb_ref[...]) * pb
    y = x_ref[...] + jnp.dot(merged.astype(BF16), wo_ref[...], preferred_element_type=F32)
    o_ref[...] = _rms(y, fnw_ref[...])


def _out_proj(y_a, y_b, p, x2, w_pa, w_pb, w_out, final_norm_w):
    tm = 512
    const = lambda i: (0, 0)
    single = pl.Buffered(1)
    return pl.pallas_call(
        _out_proj_kernel,
        grid=(SEQ // tm,),
        in_specs=[
            pl.BlockSpec((tm, A_WIDTH), lambda i: (i, 0)),
            pl.BlockSpec((tm, B_WIDTH), lambda i: (i, 0)),
            pl.BlockSpec((tm, D_MODEL), lambda i: (i, P_MA // D_MODEL)),
            pl.BlockSpec((tm, D_MODEL), lambda i: (i, P_MB // D_MODEL)),
            pl.BlockSpec((tm, D_MODEL), lambda i: (i, 0)),
            pl.BlockSpec((A_WIDTH, D_MODEL), const, pipeline_mode=single),
            pl.BlockSpec((B_WIDTH, D_MODEL), const, pipeline_mode=single),
            pl.BlockSpec((D_MODEL, D_MODEL), const, pipeline_mode=single),
            pl.BlockSpec((1, D_MODEL), const),
        ],
        out_specs=pl.BlockSpec((tm, D_MODEL), lambda i: (i, 0)),
        out_shape=jax.ShapeDtypeStruct((SEQ, D_MODEL), F32),
        compiler_params=pltpu.CompilerParams(
            dimension_semantics=("arbitrary",), vmem_limit_bytes=VMEM_LIMIT),
        name="out_proj",
    )(y_a, y_b, p, p, x2, w_pa, w_pb, w_out, final_norm_w)


def kernel(x, norm_w, w_in, lb_table, gnorm_a, q_norm_w, kv_norm_w, w_uq, w_qidx, w_ukv, kidx_norm_w,
           kidx_norm_b, w_pa, w_pb, w_out, rel_bias, final_norm_w):
    assert x.shape == (1, SEQ, D_MODEL) and w_in.shape[0] == 1
    x2 = x.reshape(SEQ, D_MODEL)
    p = _in_proj(_rms_h(x2, norm_w), w_in[0])
    w_kv = w_ukv[0].astype(BF16).reshape(KV_RANK, B_HEADS, 2, B_HEAD_DIM)
    w_k = w_kv[:, :, 0, :].reshape(KV_RANK, B_WIDTH)
    w_v_t = w_kv[:, :, 1, :].reshape(KV_RANK, B_WIDTH).T
    q_t, q_idx_t, k, v_t, k_idx, w_idx_t = _proj2(
        p, q_norm_w, kv_norm_w, w_uq[0].astype(BF16).T, w_qidx[0].astype(BF16).T, w_k, w_v_t,
        kidx_norm_w, kidx_norm_b)
    y_a = _hgrn2(p, lb_table, gnorm_a)
    y_b = _dsa(rel_bias, q_idx_t, w_idx_t, q_t, p, k_idx, k, v_t)
    out = _out_proj(y_a, y_b, p, x2, w_pa[0].astype(BF16), w_pb[0].astype(BF16), w_out[0].astype(BF16),
                    final_norm_w.reshape(1, D_MODEL))
    return out.reshape(1, SEQ, D_MODEL)
```

```python
import math

import jax
import jax.numpy as jnp
import numpy as np
from jax import lax
from jax.experimental import pallas as pl
from jax.experimental.pallas import tpu as pltpu

F32 = jnp.float32
BF16 = jnp.bfloat16

D_MODEL = 2048
SEQ = 8192
EPS = 1e-6
A_HEADS = 8
A_HEAD_DIM = 128
A_WIDTH = 1024
B_HEADS = 8
B_HEAD_DIM = 128
B_WIDTH = 1024
Q_RANK = 512
KV_RANK = 256
IDX_HEADS = 16
IDX_DIM = 128
TOPK = 256
CHUNK = 64
REL_BUCKETS = 32
REL_MAX_DIST = 128

P_AQ, P_AF, P_AI, P_AG = 0, 1024, 2048, 3072
P_CQ = 4096
P_MISC = 4608
P_BG = 5120
P_MA = 6144
P_MB = 8192
P_WIDTH = 10240
IN_WIDTH = 10128
IN_TN = 1024
IN_SHIFT = 112

VMEM_LIMIT = 63 * 1024 * 1024

QB = 256
KT = 256
NKT = SEQ // KT
KSUB = 128
CNT_ROWS = 64
SUM_ROWS = 16
UPDATE_LAG = 2

SUB = 32
MID = SUB // 2
TB = 256

NEG_INF = float("-inf")
LOG2E = math.log2(math.e)
I16_MIN = -32768
I32_MIN = -2147483648
KEY16_NEG_INF = -32641
SEQ_BITS = 13
assert SEQ == 1 << SEQ_BITS
FINE_BELOW = 1 << 16
FINE_BITS = 17

NT_DIMS = (((1,), (1,)), ((), ()))
TN_DIMS = (((0,), (0,)), ((), ()))


def _t5_thresholds():
    half = REL_BUCKETS // 2
    max_exact = half // 2
    n = np.arange(1, 4 * REL_MAX_DIST, dtype=np.int64)
    large = max_exact + (np.log(np.maximum(n, 1).astype(np.float64) / max_exact)
                         / math.log(REL_MAX_DIST / max_exact) * (half - max_exact)).astype(np.int32)
    large = np.minimum(large, half - 1)
    b = np.where(n < max_exact, n, large)
    thr = [int(n[i + 1]) for i in np.nonzero(np.diff(b))[0] if n[i + 1] > max_exact]
    assert len(thr) == half - 1 - max_exact and b[-1] == half - 1 and thr[-1] < REL_MAX_DIST
    return max_exact, half, thr


T5_MAX_EXACT, T5_HALF, T5_THR = _t5_thresholds()


def _in_proj_kernel(x_ref, nw_ref, w_ref, o_ref, h_ref):
    @pl.when(pl.program_id(1) == 0)
    def _():
        x = x_ref[...]
        ms = jnp.mean(x * x, axis=-1, keepdims=True)
        h_ref[...] = (x * lax.rsqrt(ms + EPS) * nw_ref[...]).astype(BF16)

    j = pl.program_id(1)

    @pl.when(j < P_BG // IN_TN)
    def _():
        o_ref[...] = jnp.dot(h_ref[...], w_ref[:, :IN_TN], preferred_element_type=F32)

    @pl.when(j >= P_BG // IN_TN)
    def _():
        w = pltpu.roll(w_ref[...], IN_TN + 128 - (128 - IN_SHIFT), axis=1)[:, :IN_TN]
        o_ref[...] = jnp.dot(h_ref[...], w, preferred_element_type=F32)


def _in_proj(x2, norm_w, w_in_b):
    tm, tn = 1024, IN_TN
    n_plain = P_BG // tn

    def w_index(i, j):
        return 0, pl.multiple_of(jnp.where(j < n_plain, j * tn, j * tn - 128), 128)

    return pl.pallas_call(
        _in_proj_kernel,
        grid=(SEQ // tm, P_WIDTH // tn),
        in_specs=[
            pl.BlockSpec((tm, D_MODEL), lambda i, j: (i, 0)),
            pl.BlockSpec((1, D_MODEL), lambda i, j: (0, 0)),
            pl.BlockSpec((pl.Element(D_MODEL), pl.Element(tn + 128, padding=(0, P_WIDTH - IN_WIDTH))), w_index),
        ],
        out_specs=pl.BlockSpec((tm, tn), lambda i, j: (i, j)),
        out_shape=jax.ShapeDtypeStruct((SEQ, P_WIDTH), F32),
        scratch_shapes=[pltpu.VMEM((tm, D_MODEL), BF16)],
        compiler_params=pltpu.CompilerParams(
            dimension_semantics=("arbitrary", "arbitrary"), vmem_limit_bytes=VMEM_LIMIT),
        name="in_proj",
    )(x2, norm_w, w_in_b)


def _rms(x, w):
    return x * lax.rsqrt(jnp.mean(x * x, axis=-1, keepdims=True) + EPS) * w


def _proj2_kernel(cq_ref, misc_ref, qnw_ref, kvnw_ref, wuqt_ref, wqit_ref, wk_ref, wvt_ref, lnw_ref, lnb_ref,
                  qt_ref, qit_ref, k_ref, vt_ref, kidx_ref, wt_ref):
    cqn = _rms(cq_ref[...], qnw_ref[...]).astype(BF16)
    qt = lax.dot_general(wuqt_ref[...], cqn, NT_DIMS, preferred_element_type=F32)
    qt_ref[...] = (qt * (B_HEAD_DIM ** -0.5 * LOG2E)).astype(BF16)
    qit = lax.dot_general(wqit_ref[...], cqn, NT_DIMS, preferred_element_type=F32)
    for h in range(IDX_HEADS):
        qit_ref[h] = qit[h * IDX_DIM:(h + 1) * IDX_DIM, :].astype(BF16)
    misc = misc_ref[...]
    ckvn = _rms(misc[:, :KV_RANK], kvnw_ref[...]).astype(BF16)
    k_ref[...] = jnp.dot(ckvn, wk_ref[...], preferred_element_type=F32).astype(BF16)
    vt = lax.dot_general(wvt_ref[...], ckvn, NT_DIMS, preferred_element_type=F32).astype(BF16)
    for t in range(vt_ref.shape[0]):
        vt_ref[t] = vt[:, t * KT:(t + 1) * KT]
    kr = misc[:, KV_RANK:KV_RANK + IDX_DIM]
    mu = jnp.mean(kr, axis=-1, keepdims=True)
    var = jnp.mean(jnp.square(kr - mu), axis=-1, keepdims=True)
    kidx_ref[...] = ((kr - mu) * lax.rsqrt(var + EPS) * lnw_ref[...] + lnb_ref[...]).astype(BF16)
    wt_ref[...] = (misc[:, KV_RANK + IDX_DIM:] * (IDX_HEADS ** -0.5 * IDX_DIM ** -0.5)).T


def _proj2(p, q_norm_w, kv_norm_w, w_uq_t, w_qidx_t, w_k, w_v_t, ln_w, ln_b):
    tm = 4 * KT
    const = lambda i: (0, 0)
    return pl.pallas_call(
        _proj2_kernel,
        grid=(SEQ // tm,),
        in_specs=[
            pl.BlockSpec((tm, 512), lambda i: (i, P_CQ // 512)),
            pl.BlockSpec((tm, 512), lambda i: (i, P_MISC // 512)),
            pl.BlockSpec((1, Q_RANK), const),
            pl.BlockSpec((1, KV_RANK), const),
            pl.BlockSpec((B_WIDTH, Q_RANK), const),
            pl.BlockSpec((IDX_HEADS * IDX_DIM, Q_RANK), const),
            pl.BlockSpec((KV_RANK, B_WIDTH), const),
            pl.BlockSpec((B_WIDTH, KV_RANK), const),
            pl.BlockSpec((1, IDX_DIM), const),
            pl.BlockSpec((1, IDX_DIM), const),
        ],
        out_specs=[
            pl.BlockSpec((B_WIDTH, tm), lambda i: (0, i)),
            pl.BlockSpec((IDX_HEADS, IDX_DIM, tm), lambda i: (0, 0, i)),
            pl.BlockSpec((tm, B_WIDTH), lambda i: (i, 0)),
            pl.BlockSpec((tm // KT, B_WIDTH, KT), lambda i: (i, 0, 0)),
            pl.BlockSpec((tm, IDX_DIM), lambda i: (i, 0)),
            pl.BlockSpec((128, tm), lambda i: (0, i)),
        ],
        out_shape=[
            jax.ShapeDtypeStruct((B_WIDTH, SEQ), BF16),
            jax.ShapeDtypeStruct((IDX_HEADS, IDX_DIM, SEQ), BF16),
            jax.ShapeDtypeStruct((SEQ, B_WIDTH), BF16),
            jax.ShapeDtypeStruct((NKT, B_WIDTH, KT), BF16),
            jax.ShapeDtypeStruct((SEQ, IDX_DIM), BF16),
            jax.ShapeDtypeStruct((128, SEQ), F32),
        ],
        compiler_params=pltpu.CompilerParams(
            dimension_semantics=("arbitrary",), vmem_limit_bytes=VMEM_LIMIT),
        name="proj2",
    )(p, p, q_norm_w, kv_norm_w, w_uq_t, w_qidx_t, w_k, w_v_t, ln_w, ln_b)


def _split3(x):
    x1 = x.astype(BF16)
    r1 = x - x1.astype(F32)
    x2 = r1.astype(BF16)
    x3 = (r1 - x2.astype(F32)).astype(BF16)
    return x1, x2, x3


def _hgrn2_kernel(aq_ref, af_ref, ai_ref, ag_ref, lbt_ref, gn_ref, o_ref,
                  st_ref, sel_ref, a_ref, a2_ref, b2_ref, kd_ref, v_ref, e_ref, oacc_ref):
    @pl.when(pl.program_id(0) == 0)
    def _():
        st_ref[...] = jnp.zeros_like(st_ref)
        r = lax.broadcasted_iota(jnp.int32, (TB, TB), 0)
        c = lax.broadcasted_iota(jnp.int32, (TB, TB), 1)
        same = (r // SUB) == (c // SUB)
        sel_ref[...] = jnp.where(same & (c <= r), 1.0, 0.0).astype(BF16)

    lbt = lbt_ref[...]
    mx = jnp.maximum(lbt[0:1], lbt[1:2])
    e0 = jnp.exp(lbt[0:1] - mx)
    e1 = jnp.exp(lbt[1:2] - mx)
    lb = e0 / (e0 + e1)

    f = lb + (1.0 - lb) * jax.nn.sigmoid(af_ref[...])
    g = jnp.log(f)
    sel = sel_ref[...]
    g1, g2, g3 = _split3(g)
    cum = (jnp.dot(sel, g1, preferred_element_type=F32) + jnp.dot(sel, g2, preferred_element_type=F32)
           + jnp.dot(sel, g3, preferred_element_type=F32))
    cum3 = cum.reshape(TB // SUB, SUB, A_WIDTH)
    mid = jnp.broadcast_to(cum3[:, MID - 1:MID, :], cum3.shape).reshape(TB, A_WIDTH)
    tot = jnp.broadcast_to(cum3[:, SUB - 1:SUB, :], cum3.shape).reshape(TB, A_WIDTH)
    aq = aq_ref[...]
    q = aq * jax.nn.sigmoid(aq) * (A_HEAD_DIM ** -0.5)
    kk = 1.0 - f
    a_ref[...] = (q * jnp.exp(cum)).astype(BF16)
    a2_ref[...] = (q * jnp.exp(cum - mid)).astype(BF16)
    b2_ref[...] = (kk * jnp.exp(mid - cum)).astype(BF16)
    kd_ref[...] = (kk * jnp.exp(tot - cum)).astype(BF16)
    v_ref[...] = ai_ref[...].astype(BF16)
    e_ref[...] = jnp.exp(tot)

    tr = lax.broadcasted_iota(jnp.int32, (SUB, SUB), 0)
    tc = lax.broadcasted_iota(jnp.int32, (SUB, SUB), 1)
    causal = tc <= tr

    def chunk_body(c, carry):
        r0 = pl.multiple_of(c * SUB, SUB)
        heads = [slice(h * A_HEAD_DIM, (h + 1) * A_HEAD_DIM) for h in range(A_HEADS)]
        ah = [a_ref[pl.ds(r0, SUB), cs] for cs in heads]
        vh = [v_ref[pl.ds(r0, SUB), cs] for cs in heads]
        sc = [lax.dot_general(a2_ref[pl.ds(r0, SUB), heads[h]], b2_ref[pl.ds(r0, SUB), heads[h]], NT_DIMS,
                              preferred_element_type=F32) for h in range(A_HEADS)]
        inter = []
        for h in range(A_HEADS):
            st = st_ref[h]
            inter.append(lax.dot_general(ah[h], st.astype(BF16), NT_DIMS, preferred_element_type=F32))
            dst = lax.dot_general(vh[h], kd_ref[pl.ds(r0, SUB), heads[h]], TN_DIMS, preferred_element_type=F32)
            st_ref[h] = st * e_ref[pl.ds(r0, 1), heads[h]] + dst
        for h in range(A_HEADS):
            pm = jnp.where(causal, sc[h], 0.0).astype(BF16)
            oacc_ref[pl.ds(r0, SUB), heads[h]] = inter[h] + jnp.dot(pm, vh[h], preferred_element_type=F32)
        return carry

    lax.fori_loop(0, TB // SUB, chunk_body, 0, unroll=True)

    gn = gn_ref[...]
    ag = ag_ref[...]
    gate = ag * jax.nn.sigmoid(ag)
    for h in range(A_HEADS):
        cs = slice(h * A_HEAD_DIM, (h + 1) * A_HEAD_DIM)
        o = oacc_ref[:, cs]
        on = o * lax.rsqrt(jnp.mean(o * o, axis=-1, keepdims=True) + EPS) * gn
        o_ref[:, cs] = (on * gate[:, cs]).astype(BF16)


def _hgrn2(p, lb_table, gnorm):
    blk = lambda col: pl.BlockSpec((TB, A_WIDTH), lambda i: (i, col // A_WIDTH))
    return pl.pallas_call(
        _hgrn2_kernel,
        grid=(SEQ // TB,),
        in_specs=[blk(P_AQ), blk(P_AF), blk(P_AI), blk(P_AG),
                  pl.BlockSpec((2, A_WIDTH), lambda i: (0, 0)),
                  pl.BlockSpec((1, A_HEAD_DIM), lambda i: (0, 0))],
        out_specs=pl.BlockSpec((TB, A_WIDTH), lambda i: (i, 0)),
        out_shape=jax.ShapeDtypeStruct((SEQ, A_WIDTH), BF16),
        scratch_shapes=[
            pltpu.VMEM((A_HEADS, A_HEAD_DIM, A_HEAD_DIM), F32),
            pltpu.VMEM((TB, TB), BF16),
            pltpu.VMEM((TB, A_WIDTH), BF16),
            pltpu.VMEM((TB, A_WIDTH), BF16),
            pltpu.VMEM((TB, A_WIDTH), BF16),
            pltpu.VMEM((TB, A_WIDTH), BF16),
            pltpu.VMEM((TB, A_WIDTH), BF16),
            pltpu.VMEM((TB, A_WIDTH), F32),
            pltpu.VMEM((TB, A_WIDTH), F32),
        ],
        compiler_params=pltpu.CompilerParams(
            dimension_semantics=("arbitrary",), vmem_limit_bytes=VMEM_LIMIT),
        name="hgrn2",
    )(p, p, p, p, lb_table, gnorm)


def _t5_bucket(rel):
    n = jnp.abs(rel)
    large = jnp.full(rel.shape, T5_MAX_EXACT, jnp.int32)
    for t in T5_THR:
        large = large + jnp.where(n >= t, 1, 0)
    return jnp.where(rel > 0, T5_HALF, 0) + jnp.where(n < T5_MAX_EXACT, n, large)


def _dsa_kernel(relb_ref, qi_ref, w_ref, q_ref, bg_ref, kidx_ref, k_ref, v_ref, o_ref,
                sc_ref, hi_ref, b0_ref, b1_ref, thr_ref, m_ref, l_ref, acc_ref, s_ref, ms_ref, al_ref):
    qb = pl.program_id(0)

    @pl.when(qb == 0)
    def _():
        def bias_tile(rel, h):
            bucket = _t5_bucket(rel)
            val = jnp.zeros(rel.shape, F32)
            for b in range(REL_BUCKETS):
                val = jnp.where(bucket == b, relb_ref[b, h], val)
            return (val - relb_ref[T5_HALF - 1, h]) * LOG2E

        r0 = lax.broadcasted_iota(jnp.int32, (KT, QB), 0)
        c0 = lax.broadcasted_iota(jnp.int32, (KT, QB), 1)
        r1 = lax.broadcasted_iota(jnp.int32, (128, 128), 0)
        c1 = lax.broadcasted_iota(jnp.int32, (128, 128), 1)
        for h in range(B_HEADS):
            b0_ref[h] = bias_tile(r0 - c0, h)
            b1_ref[h] = bias_tile(r1 - 128 - c1, h)

    def idx_tile(j, diag):
        for ks in range(KT // KSUB):
            kt = kidx_ref[pl.ds(pl.multiple_of(j * KT + ks * KSUB, KSUB), KSUB), :]
            acc = jnp.zeros((KSUB, QB), F32)
            for h in range(IDX_HEADS):
                r = jnp.dot(kt, qi_ref[h], preferred_element_type=F32)
                acc = acc + w_ref[h:h + 1, :] * jnp.maximum(r, 0.0)
            if diag:
                rk = ks * KSUB + lax.broadcasted_iota(jnp.int32, (KSUB, QB), 0)
                ct = lax.broadcasted_iota(jnp.int32, (KSUB, QB), 1)
                acc = jnp.where((rk // CHUNK) <= (ct // CHUNK), acc, NEG_INF)
            sc_ref[j, ks * KSUB:(ks + 1) * KSUB, :] = acc
            hi_ref[j, ks * KSUB:(ks + 1) * KSUB, :] = acc.astype(BF16)

    def idx_body(i, carry):
        idx_tile(2 * i, False)
        idx_tile(2 * i + 1, False)
        return carry

    lax.fori_loop(0, qb // 2, idx_body, 0)

    @pl.when(qb % 2 == 1)
    def _():
        idx_tile(qb - 1, False)
        idx_tile(qb, True)

    @pl.when(qb % 2 == 0)
    def _():
        idx_tile(qb, True)

    npairs = (qb + 2) // 2

    @pl.when(qb % 2 == 0)
    def _():
        sc_ref[qb + 1] = jnp.full((KT, QB), NEG_INF, F32)
        hi_ref[qb + 1] = jnp.full((KT, QB), NEG_INF, BF16)

    def count(ref, thr):
        one = jnp.ones((KT, QB), ref.dtype)
        zero = jnp.zeros((KT, QB), ref.dtype)

        def tiles(first, n, cnt):
            for u in range(n):
                sel = jnp.where(ref[first + u] >= thr, one, zero)
                for r in range(KT // CNT_ROWS):
                    cnt = cnt + sel[r * CNT_ROWS:(r + 1) * CNT_ROWS]
            return cnt

        cnt = lax.fori_loop(0, npairs // 2, lambda i, c: tiles(4 * i, 4, c), jnp.zeros((CNT_ROWS, QB), ref.dtype))
        cnt = lax.fori_loop(0, npairs % 2, lambda i, c: tiles(2 * (npairs - 1), 2, c), cnt)
        return jnp.sum(cnt.astype(F32), axis=0, keepdims=True)

    def key16_to_bf16(k):
        bits = jnp.where(k >= 0, k, k ^ 0x7FFF).astype(jnp.int16)
        return lax.bitcast_convert_type(bits, BF16)

    def key32_to_f32(k):
        return lax.bitcast_convert_type(jnp.where(k >= 0, k, k ^ 0x7FFFFFFF), F32)

    c0 = count(hi_ref, jnp.zeros((1, QB), BF16))
    k16 = jnp.where(c0 >= float(TOPK), 0, I16_MIN).astype(jnp.int32)

    def coarse_body(i, k):
        cand = k | jnp.left_shift(jnp.int32(1), 14 - i)
        return jnp.where(count(hi_ref, key16_to_bf16(cand)) >= float(TOPK), cand, k)

    k16 = jnp.maximum(lax.fori_loop(0, 15, coarse_body, k16), KEY16_NEG_INF)
    k_mid = jnp.where(k16 >= 0, k16 << 16, (k16 << 16) | 0xFFFF)
    k_base = jnp.maximum(k_mid, I32_MIN + FINE_BELOW) - FINE_BELOW

    def fine_body(i, carry):
        off, n_ge = carry
        cand = off | jnp.left_shift(jnp.int32(1), FINE_BITS - 1 - i)
        c = count(sc_ref, key32_to_f32(k_base + cand))
        ok = c >= float(TOPK)
        return jnp.where(ok, cand, off), jnp.where(ok, c, n_ge)

    off, n_ge = lax.fori_loop(0, FINE_BITS, fine_body,
                              (jnp.zeros((1, QB), jnp.int32), jnp.full((1, QB), TOPK + 1.0, F32)))
    thr = key32_to_f32(k_base + off)
    thr_ref[...] = thr

    @pl.when(jnp.max(n_ge) > float(TOPK))
    def _():
        rows = lax.broadcasted_iota(jnp.int32, (KT, QB), 0)

        def tile_sum(fn):
            def body(jp, cnt):
                for u in range(2):
                    sel = fn(2 * jp + u)
                    for r in range(KT // CNT_ROWS):
                        cnt = cnt + sel[r * CNT_ROWS:(r + 1) * CNT_ROWS]
                return cnt
            cnt = lax.fori_loop(0, npairs, body, jnp.zeros((CNT_ROWS, QB), F32))
            return jnp.sum(cnt, axis=0, keepdims=True)

        n_gt = tile_sum(lambda j: jnp.where(sc_ref[j] > thr, 1.0, 0.0))
        ties_kept = float(TOPK) - n_gt

        def ties_before(p):
            return tile_sum(lambda j: jnp.where((sc_ref[j] == thr) & (rows < p - j * KT), 1.0, 0.0))

        def idx_body(i, p):
            cand = p | jnp.left_shift(jnp.int32(1), SEQ_BITS - 1 - i)
            return jnp.where(ties_before(cand) < ties_kept, cand, p)

        last = lax.fori_loop(0, SEQ_BITS, idx_body, jnp.zeros((1, QB), jnp.int32))

        def drop_body(j, carry):
            x = sc_ref[j]
            sc_ref[j] = jnp.where((x == thr) & (rows > last - j * KT), NEG_INF, x)
            return carry

        lax.fori_loop(0, 2 * npairs, drop_body, 0)

    m_ref[...] = jnp.full(m_ref.shape, NEG_INF, F32)
    l_ref[...] = jnp.zeros_like(l_ref)
    acc_ref[...] = jnp.zeros_like(acc_ref)
    ones_rows = jnp.ones((SUM_ROWS, KT), BF16)

    def att_step(logits=None, values=None):
        if logits is not None:
            j1, kind, slot1 = logits
            s_t = sc_ref[j1]
            mask = jnp.where((s_t >= thr_ref[...]) & (s_t > NEG_INF), 0.0, NEG_INF)
            k0 = pl.multiple_of(j1 * KT, KT)
        if values is not None:
            j2, slot2 = values
        pvs = []

        def update(h):
            a = al_ref[slot2, h]
            l_ref[h] = a * l_ref[h] + pvs[h][B_HEAD_DIM:B_HEAD_DIM + 1, :]
            acc_ref[h] = a * acc_ref[h] + pvs[h][:B_HEAD_DIM, :]

        for h in range(B_HEADS):
            hs = slice(h * B_HEAD_DIM, (h + 1) * B_HEAD_DIM)
            if logits is not None:
                s = jnp.dot(k_ref[pl.ds(k0, KT), hs], q_ref[hs, :], preferred_element_type=F32)
                if kind == "diag":
                    s = s + b0_ref[h]
                elif kind == "prev":
                    lo = jnp.concatenate([s[128:, :128] + b1_ref[h], s[128:, 128:]], axis=1)
                    s = jnp.concatenate([s[:128, :], lo], axis=0)
                s = s + mask
                s_ref[slot1, h] = s
                m_prev = m_ref[h]
                m_next = jnp.maximum(m_prev, jnp.max(s, axis=0, keepdims=True))
                m_safe = jnp.where(m_next == NEG_INF, 0.0, m_next)
                ms_ref[slot1, h] = m_safe
                al_ref[slot1, h] = jnp.exp2(m_prev - m_safe)
                m_ref[h] = m_next
            if values is not None:
                p = jnp.exp2(s_ref[slot2, h] - ms_ref[slot2, h]).astype(BF16)
                pvs.append(jnp.dot(jnp.concatenate([v_ref[j2, hs, :], ones_rows], axis=0), p,
                                   preferred_element_type=F32))
                if h >= UPDATE_LAG:
                    update(h - UPDATE_LAG)
        if values is not None:
            for h in range(B_HEADS - UPDATE_LAG, B_HEADS):
                update(h)

    odd = qb % 2

    @pl.when(qb == 0)
    def _():
        att_step(logits=(0, "diag", 0))

    @pl.when(qb == 1)
    def _():
        att_step(logits=(0, "prev", 1))

    @pl.when(qb >= 2)
    def _():
        @pl.when(odd == 0)
        def _():
            att_step(logits=(0, "far", 0))

        @pl.when(odd == 1)
        def _():
            att_step(logits=(0, "far", 1))
            att_step(logits=(1, "far", 0), values=(0, 1))

        def att_body(i, carry):
            j = odd + 2 * i
            att_step(logits=(j + 1, "far", 1), values=(j, 0))
            att_step(logits=(j + 2, "far", 0), values=(j + 1, 1))
            return carry

        lax.fori_loop(0, (qb - 2) // 2, att_body, 0)
        att_step(logits=(qb - 1, "prev", 1), values=(qb - 2, 0))

    @pl.when(qb >= 1)
    def _():
        att_step(logits=(qb, "diag", 0), values=(qb - 1, 1))

    att_step(values=(qb, 0))

    for h in range(B_HEADS):
        hs = slice(h * B_HEAD_DIM, (h + 1) * B_HEAD_DIM)
        bg = bg_ref[:, hs]
        o_ref[:, hs] = ((acc_ref[h] / l_ref[h]).T * (bg * jax.nn.sigmoid(bg))).astype(BF16)


def _dsa(rel_bias, q_idx_t, w_idx_t, q_t, p, k_idx, k, v_t):
    single = pl.Buffered(1)
    return pl.pallas_call(
        _dsa_kernel,
        grid=(SEQ // QB,),
        in_specs=[
            pl.BlockSpec(memory_space=pltpu.SMEM),
            pl.BlockSpec((IDX_HEADS, IDX_DIM, QB), lambda i: (0, 0, i)),
            pl.BlockSpec((128, QB), lambda i: (0, i)),
            pl.BlockSpec((B_WIDTH, QB), lambda i: (0, i)),
            pl.BlockSpec((QB, B_WIDTH), lambda i: (i, P_BG // B_WIDTH)),
            pl.BlockSpec((SEQ, IDX_DIM), lambda i: (0, 0), pipeline_mode=single),
            pl.BlockSpec((SEQ, B_WIDTH), lambda i: (0, 0), pipeline_mode=single),
            pl.BlockSpec((NKT, B_WIDTH, KT), lambda i: (0, 0, 0), pipeline_mode=single),
        ],
        out_specs=pl.BlockSpec((QB, B_WIDTH), lambda i: (i, 0)),
        out_shape=jax.ShapeDtypeStruct((SEQ, B_WIDTH), BF16),
        scratch_shapes=[
            pltpu.VMEM((NKT, KT, QB), F32),
            pltpu.VMEM((NKT, KT, QB), BF16),
            pltpu.VMEM((B_HEADS, KT, QB), F32),
            pltpu.VMEM((B_HEADS, 128, 128), F32),
            pltpu.VMEM((1, QB), F32),
            pltpu.VMEM((B_HEADS, 1, QB), F32),
            pltpu.VMEM((B_HEADS, 1, QB), F32),
            pltpu.VMEM((B_HEADS, B_HEAD_DIM, QB), F32),
            pltpu.VMEM((2, B_HEADS, KT, QB), F32),
            pltpu.VMEM((2, B_HEADS, 1, QB), F32),
            pltpu.VMEM((2, B_HEADS, 1, QB), F32),
        ],
        compiler_params=pltpu.CompilerParams(
            dimension_semantics=("arbitrary",), vmem_limit_bytes=VMEM_LIMIT),
        name="dsa",
    )(rel_bias, q_idx_t, w_idx_t, q_t, p, k_idx, k, v_t)


def _out_proj_kernel(ya_ref, yb_ref, ma_ref, mb_ref, x_ref, wpa_ref, wpb_ref, wo_ref, fnw_ref, o_ref):
    pa = jnp.dot(ya_ref[...], wpa_ref[...], preferred_element_type=F32)
    pb = jnp.dot(yb_ref[...], wpb_ref[...], preferred_element_type=F32)
    merged = jax.nn.sigmoid(ma_ref[...]) * pa + jax.nn.sigmoid(mb_ref[...]) * pb
    y = x_ref[...] + jnp.dot(merged.astype(BF16), wo_ref[...], preferred_element_type=F32)
    o_ref[...] = _rms(y, fnw_ref[...])


def _out_proj(y_a, y_b, p, x2, w_pa, w_pb, w_out, final_norm_w):
    tm = 512
    const = lambda i: (0, 0)
    single = pl.Buffered(1)
    return pl.pallas_call(
        _out_proj_kernel,
        grid=(SEQ // tm,),
        in_specs=[
            pl.BlockSpec((tm, A_WIDTH), lambda i: (i, 0)),
            pl.BlockSpec((tm, B_WIDTH), lambda i: (i, 0)),
            pl.BlockSpec((tm, D_MODEL), lambda i: (i, P_MA // D_MODEL)),
            pl.BlockSpec((tm, D_MODEL), lambda i: (i, P_MB // D_MODEL)),
            pl.BlockSpec((tm, D_MODEL), lambda i: (i, 0)),
            pl.BlockSpec((A_WIDTH, D_MODEL), const, pipeline_mode=single),
            pl.BlockSpec((B_WIDTH, D_MODEL), const, pipeline_mode=single),
            pl.BlockSpec((D_MODEL, D_MODEL), const, pipeline_mode=single),
            pl.BlockSpec((1, D_MODEL), const),
        ],
        out_specs=pl.BlockSpec((tm, D_MODEL), lambda i: (i, 0)),
        out_shape=jax.ShapeDtypeStruct((SEQ, D_MODEL), F32),
        compiler_params=pltpu.CompilerParams(
            dimension_semantics=("arbitrary",), vmem_limit_bytes=VMEM_LIMIT),
        name="out_proj",
    )(y_a, y_b, p, p, x2, w_pa, w_pb, w_out, final_norm_w)


def kernel(x, norm_w, w_in, lb_table, gnorm_a, q_norm_w, kv_norm_w, w_uq, w_qidx, w_ukv, kidx_norm_w,
           kidx_norm_b, w_pa, w_pb, w_out, rel_bias, final_norm_w):
    assert x.shape == (1, SEQ, D_MODEL) and w_in.shape[0] == 1
    x2 = x.reshape(SEQ, D_MODEL)
    p = _in_proj(x2, norm_w, w_in[0].astype(BF16))
    w_kv = w_ukv[0].astype(BF16).reshape(KV_RANK, B_HEADS, 2, B_HEAD_DIM)
    w_k = w_kv[:, :, 0, :].reshape(KV_RANK, B_WIDTH)
    w_v_t = w_kv[:, :, 1, :].reshape(KV_RANK, B_WIDTH).T
    q_t, q_idx_t, k, v_t, k_idx, w_idx_t = _proj2(
        p, q_norm_w, kv_norm_w, w_uq[0].astype(BF16).T, w_qidx[0].astype(BF16).T, w_k, w_v_t,
        kidx_norm_w, kidx_norm_b)
    y_a = _hgrn2(p, lb_table, gnorm_a)
    y_b = _dsa(rel_bias, q_idx_t, w_idx_t, q_t, p, k_idx, k, v_t)
    out = _out_proj(y_a, y_b, p, x2, w_pa[0].astype(BF16), w_pb[0].astype(BF16), w_out[0].astype(BF16),
                    final_norm_w.reshape(1, D_MODEL))
    return out.reshape(1, SEQ, D_MODEL)
```

```python
import math

import jax
import jax.numpy as jnp
import numpy as np
from jax import lax
from jax.experimental import pallas as pl
from jax.experimental.pallas import tpu as pltpu

F32 = jnp.float32
BF16 = jnp.bfloat16

D_MODEL = 2048
SEQ = 8192
EPS = 1e-6
A_HEADS = 8
A_HEAD_DIM = 128
A_WIDTH = 1024
B_HEADS = 8
B_HEAD_DIM = 128
B_WIDTH = 1024
Q_RANK = 512
KV_RANK = 256
IDX_HEADS = 16
IDX_DIM = 128
TOPK = 256
CHUNK = 64
REL_BUCKETS = 32
REL_MAX_DIST = 128

P_AQ, P_AF, P_AI, P_AG = 0, 1024, 2048, 3072
P_CQ = 4096
P_MISC = 4608
P_BG = 5120
P_MA = 6144
P_MB = 8192
P_WIDTH = 10240
IN_WIDTH = 10128
IN_TN = 1024
IN_SHIFT = 112

VMEM_LIMIT = 63 * 1024 * 1024

QB = 256
KT = 256
NKT = SEQ // KT
KSUB = 128
IDX_UNROLL = 4
CNT_ROWS = 64
SUM_ROWS = 16
UPDATE_LAG = 2

SUB = 32
MID = SUB // 2
TB = 256

NEG_INF = float("-inf")
LOG2E = math.log2(math.e)
I16_MIN = -32768
I32_MIN = -2147483648
KEY16_NEG_INF = -32641
SEQ_BITS = 13
assert SEQ == 1 << SEQ_BITS
FINE_BELOW = 1 << 16
FINE_BITS = 17

NT_DIMS = (((1,), (1,)), ((), ()))
TN_DIMS = (((0,), (0,)), ((), ()))


def _t5_thresholds():
    half = REL_BUCKETS // 2
    max_exact = half // 2
    n = np.arange(1, 4 * REL_MAX_DIST, dtype=np.int64)
    large = max_exact + (np.log(np.maximum(n, 1).astype(np.float64) / max_exact)
                         / math.log(REL_MAX_DIST / max_exact) * (half - max_exact)).astype(np.int32)
    large = np.minimum(large, half - 1)
    b = np.where(n < max_exact, n, large)
    thr = [int(n[i + 1]) for i in np.nonzero(np.diff(b))[0] if n[i + 1] > max_exact]
    assert len(thr) == half - 1 - max_exact and b[-1] == half - 1 and thr[-1] < REL_MAX_DIST
    return max_exact, half, thr


T5_MAX_EXACT, T5_HALF, T5_THR = _t5_thresholds()


def _in_proj_kernel(x_ref, nw_ref, w_ref, o_ref, h_ref):
    @pl.when(pl.program_id(1) == 0)
    def _():
        x = x_ref[...]
        ms = jnp.mean(x * x, axis=-1, keepdims=True)
        h_ref[...] = (x * lax.rsqrt(ms + EPS) * nw_ref[...]).astype(BF16)

    j = pl.program_id(1)

    @pl.when(j < P_BG // IN_TN)
    def _():
        o_ref[...] = jnp.dot(h_ref[...], w_ref[:, :IN_TN], preferred_element_type=F32)

    @pl.when(j >= P_BG // IN_TN)
    def _():
        w = pltpu.roll(w_ref[...], IN_TN + 128 - (128 - IN_SHIFT), axis=1)[:, :IN_TN]
        o_ref[...] = jnp.dot(h_ref[...], w, preferred_element_type=F32)


def _in_proj(x2, norm_w, w_in_b):
    tm, tn = 1024, IN_TN
    n_plain = P_BG // tn

    def w_index(i, j):
        return 0, pl.multiple_of(jnp.where(j < n_plain, j * tn, j * tn - 128), 128)

    return pl.pallas_call(
        _in_proj_kernel,
        grid=(SEQ // tm, P_WIDTH // tn),
        in_specs=[
            pl.BlockSpec((tm, D_MODEL), lambda i, j: (i, 0)),
            pl.BlockSpec((1, D_MODEL), lambda i, j: (0, 0)),
            pl.BlockSpec((pl.Element(D_MODEL), pl.Element(tn + 128, padding=(0, P_WIDTH - IN_WIDTH))), w_index),
        ],
        out_specs=pl.BlockSpec((tm, tn), lambda i, j: (i, j)),
        out_shape=jax.ShapeDtypeStruct((SEQ, P_WIDTH), F32),
        scratch_shapes=[pltpu.VMEM((tm, D_MODEL), BF16)],
        compiler_params=pltpu.CompilerParams(
            dimension_semantics=("arbitrary", "arbitrary"), vmem_limit_bytes=VMEM_LIMIT),
        name="in_proj",
    )(x2, norm_w, w_in_b)


def _rms(x, w):
    return x * lax.rsqrt(jnp.mean(x * x, axis=-1, keepdims=True) + EPS) * w


def _proj2_kernel(cq_ref, misc_ref, qnw_ref, kvnw_ref, wuqt_ref, wqit_ref, wk_ref, wvt_ref, lnw_ref, lnb_ref,
                  qt_ref, qit_ref, k_ref, vt_ref, kidx_ref, wt_ref):
    cqn = _rms(cq_ref[...], qnw_ref[...]).astype(BF16)
    qt = lax.dot_general(wuqt_ref[...], cqn, NT_DIMS, preferred_element_type=F32)
    qt_ref[...] = (qt * (B_HEAD_DIM ** -0.5 * LOG2E)).astype(BF16)
    qit = lax.dot_general(wqit_ref[...], cqn, NT_DIMS, preferred_element_type=F32)
    for h in range(IDX_HEADS):
        qit_ref[h] = qit[h * IDX_DIM:(h + 1) * IDX_DIM, :].astype(BF16)
    misc = misc_ref[...]
    ckvn = _rms(misc[:, :KV_RANK], kvnw_ref[...]).astype(BF16)
    k_ref[...] = jnp.dot(ckvn, wk_ref[...], preferred_element_type=F32).astype(BF16)
    vt = lax.dot_general(wvt_ref[...], ckvn, NT_DIMS, preferred_element_type=F32).astype(BF16)
    for t in range(vt_ref.shape[0]):
        vt_ref[t] = vt[:, t * KT:(t + 1) * KT]
    kr = misc[:, KV_RANK:KV_RANK + IDX_DIM]
    mu = jnp.mean(kr, axis=-1, keepdims=True)
    var = jnp.mean(jnp.square(kr - mu), axis=-1, keepdims=True)
    kidx_ref[...] = ((kr - mu) * lax.rsqrt(var + EPS) * lnw_ref[...] + lnb_ref[...]).astype(BF16)
    wt_ref[...] = (misc[:, KV_RANK + IDX_DIM:] * (IDX_HEADS ** -0.5 * IDX_DIM ** -0.5)).T


def _proj2(p, q_norm_w, kv_norm_w, w_uq_t, w_qidx_t, w_k, w_v_t, ln_w, ln_b):
    tm = 4 * KT
    const = lambda i: (0, 0)
    return pl.pallas_call(
        _proj2_kernel,
        grid=(SEQ // tm,),
        in_specs=[
            pl.BlockSpec((tm, 512), lambda i: (i, P_CQ // 512)),
            pl.BlockSpec((tm, 512), lambda i: (i, P_MISC // 512)),
            pl.BlockSpec((1, Q_RANK), const),
            pl.BlockSpec((1, KV_RANK), const),
            pl.BlockSpec((B_WIDTH, Q_RANK), const),
            pl.BlockSpec((IDX_HEADS * IDX_DIM, Q_RANK), const),
            pl.BlockSpec((KV_RANK, B_WIDTH), const),
            pl.BlockSpec((B_WIDTH, KV_RANK), const),
            pl.BlockSpec((1, IDX_DIM), const),
            pl.BlockSpec((1, IDX_DIM), const),
        ],
        out_specs=[
            pl.BlockSpec((B_WIDTH, tm), lambda i: (0, i)),
            pl.BlockSpec((IDX_HEADS, IDX_DIM, tm), lambda i: (0, 0, i)),
            pl.BlockSpec((tm, B_WIDTH), lambda i: (i, 0)),
            pl.BlockSpec((tm // KT, B_WIDTH, KT), lambda i: (i, 0, 0)),
            pl.BlockSpec((tm, IDX_DIM), lambda i: (i, 0)),
            pl.BlockSpec((128, tm), lambda i: (0, i)),
        ],
        out_shape=[
            jax.ShapeDtypeStruct((B_WIDTH, SEQ), BF16),
            jax.ShapeDtypeStruct((IDX_HEADS, IDX_DIM, SEQ), BF16),
            jax.ShapeDtypeStruct((SEQ, B_WIDTH), BF16),
            jax.ShapeDtypeStruct((NKT, B_WIDTH, KT), BF16),
            jax.ShapeDtypeStruct((SEQ, IDX_DIM), BF16),
            jax.ShapeDtypeStruct((128, SEQ), F32),
        ],
        compiler_params=pltpu.CompilerParams(
            dimension_semantics=("arbitrary",), vmem_limit_bytes=VMEM_LIMIT),
        name="proj2",
    )(p, p, q_norm_w, kv_norm_w, w_uq_t, w_qidx_t, w_k, w_v_t, ln_w, ln_b)


def _sigmoid(x):
    return 0.5 * jnp.tanh(0.5 * x) + 0.5


def _split3(x):
    x1 = x.astype(BF16)
    r1 = x - x1.astype(F32)
    x2 = r1.astype(BF16)
    x3 = (r1 - x2.astype(F32)).astype(BF16)
    return x1, x2, x3


def _hgrn2_kernel(aq_ref, af_ref, ai_ref, ag_ref, lbt_ref, gn_ref, o_ref,
                  st_ref, sel_ref, a_ref, a2_ref, b2_ref, kd_ref, v_ref, e_ref, oacc_ref):
    @pl.when(pl.program_id(0) == 0)
    def _():
        st_ref[...] = jnp.zeros_like(st_ref)
        r = lax.broadcasted_iota(jnp.int32, (TB, TB), 0)
        c = lax.broadcasted_iota(jnp.int32, (TB, TB), 1)
        same = (r // SUB) == (c // SUB)
        sel_ref[...] = jnp.where(same & (c <= r), 1.0, 0.0).astype(BF16)

    lbt = lbt_ref[...]
    mx = jnp.maximum(lbt[0:1], lbt[1:2])
    e0 = jnp.exp(lbt[0:1] - mx)
    e1 = jnp.exp(lbt[1:2] - mx)
    lb = e0 / (e0 + e1)

    f = lb + (1.0 - lb) * _sigmoid(af_ref[...])
    g = jnp.log(f)
    sel = sel_ref[...]
    g1, g2, g3 = _split3(g)
    cum = (jnp.dot(sel, g1, preferred_element_type=F32) + jnp.dot(sel, g2, preferred_element_type=F32)
           + jnp.dot(sel, g3, preferred_element_type=F32))
    cum3 = cum.reshape(TB // SUB, SUB, A_WIDTH)
    mid = jnp.broadcast_to(cum3[:, MID - 1:MID, :], cum3.shape).reshape(TB, A_WIDTH)
    tot = jnp.broadcast_to(cum3[:, SUB - 1:SUB, :], cum3.shape).reshape(TB, A_WIDTH)
    aq = aq_ref[...]
    q = aq * _sigmoid(aq) * (A_HEAD_DIM ** -0.5)
    kk = 1.0 - f
    a_ref[...] = (q * jnp.exp(cum)).astype(BF16)
    a2_ref[...] = (q * jnp.exp(cum - mid)).astype(BF16)
    b2_ref[...] = (kk * jnp.exp(mid - cum)).astype(BF16)
    kd_ref[...] = (kk * jnp.exp(tot - cum)).astype(BF16)
    v_ref[...] = ai_ref[...].astype(BF16)
    e_ref[...] = jnp.exp(tot)

    tr = lax.broadcasted_iota(jnp.int32, (SUB, SUB), 0)
    tc = lax.broadcasted_iota(jnp.int32, (SUB, SUB), 1)
    causal = tc <= tr

    def chunk_body(c, carry):
        r0 = pl.multiple_of(c * SUB, SUB)
        heads = [slice(h * A_HEAD_DIM, (h + 1) * A_HEAD_DIM) for h in range(A_HEADS)]
        ah = [a_ref[pl.ds(r0, SUB), cs] for cs in heads]
        vh = [v_ref[pl.ds(r0, SUB), cs] for cs in heads]
        sc = [lax.dot_general(a2_ref[pl.ds(r0, SUB), heads[h]], b2_ref[pl.ds(r0, SUB), heads[h]], NT_DIMS,
                              preferred_element_type=F32) for h in range(A_HEADS)]
        inter = []
        for h in range(A_HEADS):
            st = st_ref[h]
            inter.append(lax.dot_general(ah[h], st.astype(BF16), NT_DIMS, preferred_element_type=F32))
            dst = lax.dot_general(vh[h], kd_ref[pl.ds(r0, SUB), heads[h]], TN_DIMS, preferred_element_type=F32)
            st_ref[h] = st * e_ref[pl.ds(r0, 1), heads[h]] + dst
        for h in range(A_HEADS):
            pm = jnp.where(causal, sc[h], 0.0).astype(BF16)
            oacc_ref[pl.ds(r0, SUB), heads[h]] = inter[h] + jnp.dot(pm, vh[h], preferred_element_type=F32)
        return carry

    lax.fori_loop(0, TB // SUB, chunk_body, 0, unroll=True)

    gn = gn_ref[...]
    ag = ag_ref[...]
    gate = ag * _sigmoid(ag)
    for h in range(A_HEADS):
        cs = slice(h * A_HEAD_DIM, (h + 1) * A_HEAD_DIM)
        o = oacc_ref[:, cs]
        on = o * lax.rsqrt(jnp.mean(o * o, axis=-1, keepdims=True) + EPS) * gn
        o_ref[:, cs] = (on * gate[:, cs]).astype(BF16)


def _hgrn2(p, lb_table, gnorm):
    blk = lambda col: pl.BlockSpec((TB, A_WIDTH), lambda i: (i, col // A_WIDTH))
    return pl.pallas_call(
        _hgrn2_kernel,
        grid=(SEQ // TB,),
        in_specs=[blk(P_AQ), blk(P_AF), blk(P_AI), blk(P_AG),
                  pl.BlockSpec((2, A_WIDTH), lambda i: (0, 0)),
                  pl.BlockSpec((1, A_HEAD_DIM), lambda i: (0, 0))],
        out_specs=pl.BlockSpec((TB, A_WIDTH), lambda i: (i, 0)),
        out_shape=jax.ShapeDtypeStruct((SEQ, A_WIDTH), BF16),
        scratch_shapes=[
            pltpu.VMEM((A_HEADS, A_HEAD_DIM, A_HEAD_DIM), F32),
            pltpu.VMEM((TB, TB), BF16),
            pltpu.VMEM((TB, A_WIDTH), BF16),
            pltpu.VMEM((TB, A_WIDTH), BF16),
            pltpu.VMEM((TB, A_WIDTH), BF16),
            pltpu.VMEM((TB, A_WIDTH), BF16),
            pltpu.VMEM((TB, A_WIDTH), BF16),
            pltpu.VMEM((TB, A_WIDTH), F32),
            pltpu.VMEM((TB, A_WIDTH), F32),
        ],
        compiler_params=pltpu.CompilerParams(
            dimension_semantics=("arbitrary",), vmem_limit_bytes=VMEM_LIMIT),
        name="hgrn2",
    )(p, p, p, p, lb_table, gnorm)


def _t5_bucket(rel):
    n = jnp.abs(rel)
    large = jnp.full(rel.shape, T5_MAX_EXACT, jnp.int32)
    for t in T5_THR:
        large = large + jnp.where(n >= t, 1, 0)
    return jnp.where(rel > 0, T5_HALF, 0) + jnp.where(n < T5_MAX_EXACT, n, large)


def _dsa_kernel(relb_ref, qi_ref, w_ref, q_ref, bg_ref, kidx_ref, k_ref, v_ref, o_ref,
                sc_ref, hi_ref, b0_ref, b1_ref, thr_ref, m_ref, l_ref, acc_ref, s_ref, ms_ref, al_ref):
    qb = pl.program_id(0)

    @pl.when(qb == 0)
    def _():
        def bias_tile(rel, h):
            bucket = _t5_bucket(rel)
            val = jnp.zeros(rel.shape, F32)
            for b in range(REL_BUCKETS):
                val = jnp.where(bucket == b, relb_ref[b, h], val)
            return (val - relb_ref[T5_HALF - 1, h]) * LOG2E

        r0 = lax.broadcasted_iota(jnp.int32, (KT, QB), 0)
        c0 = lax.broadcasted_iota(jnp.int32, (KT, QB), 1)
        r1 = lax.broadcasted_iota(jnp.int32, (128, 128), 0)
        c1 = lax.broadcasted_iota(jnp.int32, (128, 128), 1)
        for h in range(B_HEADS):
            b0_ref[h] = bias_tile(r0 - c0, h)
            b1_ref[h] = bias_tile(r1 - 128 - c1, h)

    def idx_tile(j, diag):
        for ks in range(KT // KSUB):
            kt = kidx_ref[pl.ds(pl.multiple_of(j * KT + ks * KSUB, KSUB), KSUB), :]
            acc = jnp.zeros((KSUB, QB), F32)
            for h in range(IDX_HEADS):
                r = jnp.dot(kt, qi_ref[h], preferred_element_type=F32)
                acc = acc + w_ref[h:h + 1, :] * jnp.maximum(r, 0.0)
            if diag:
                rk = ks * KSUB + lax.broadcasted_iota(jnp.int32, (KSUB, QB), 0)
                ct = lax.broadcasted_iota(jnp.int32, (KSUB, QB), 1)
                acc = jnp.where((rk // CHUNK) <= (ct // CHUNK), acc, NEG_INF)
            sc_ref[j, ks * KSUB:(ks + 1) * KSUB, :] = acc
            hi_ref[j, ks * KSUB:(ks + 1) * KSUB, :] = acc.astype(BF16)

    def idx_body(i, carry):
        for u in range(IDX_UNROLL):
            idx_tile(IDX_UNROLL * i + u, False)
        return carry

    lax.fori_loop(0, qb // IDX_UNROLL, idx_body, 0)
    for rem in range(IDX_UNROLL):
        @pl.when(qb % IDX_UNROLL == rem)
        def _():
            for u in range(rem):
                idx_tile(qb - rem + u, False)
            idx_tile(qb, True)

    npairs = (qb + 2) // 2

    @pl.when(qb % 2 == 0)
    def _():
        sc_ref[qb + 1] = jnp.full((KT, QB), NEG_INF, F32)
        hi_ref[qb + 1] = jnp.full((KT, QB), NEG_INF, BF16)

    def count(ref, thr):
        one = jnp.ones((KT, QB), ref.dtype)
        zero = jnp.zeros((KT, QB), ref.dtype)

        def tiles(first, n, cnt):
            for u in range(n):
                sel = jnp.where(ref[first + u] >= thr, one, zero)
                for r in range(KT // CNT_ROWS):
                    cnt = cnt + sel[r * CNT_ROWS:(r + 1) * CNT_ROWS]
            return cnt

        cnt = lax.fori_loop(0, npairs // 2, lambda i, c: tiles(4 * i, 4, c), jnp.zeros((CNT_ROWS, QB), ref.dtype))
        cnt = lax.fori_loop(0, npairs % 2, lambda i, c: tiles(2 * (npairs - 1), 2, c), cnt)
        return jnp.sum(cnt.astype(F32), axis=0, keepdims=True)

    def key16_to_bf16(k):
        bits = jnp.where(k >= 0, k, k ^ 0x7FFF).astype(jnp.int16)
        return lax.bitcast_convert_type(bits, BF16)

    def key32_to_f32(k):
        return lax.bitcast_convert_type(jnp.where(k >= 0, k, k ^ 0x7FFFFFFF), F32)

    c0 = count(hi_ref, jnp.zeros((1, QB), BF16))
    k16 = jnp.where(c0 >= float(TOPK), 0, I16_MIN).astype(jnp.int32)

    def coarse_body(i, k):
        cand = k | jnp.left_shift(jnp.int32(1), 14 - i)
        return jnp.where(count(hi_ref, key16_to_bf16(cand)) >= float(TOPK), cand, k)

    k16 = jnp.maximum(lax.fori_loop(0, 15, coarse_body, k16), KEY16_NEG_INF)
    k_mid = jnp.where(k16 >= 0, k16 << 16, (k16 << 16) | 0xFFFF)
    k_base = jnp.maximum(k_mid, I32_MIN + FINE_BELOW) - FINE_BELOW

    def fine_body(i, carry):
        off, n_ge = carry
        cand = off | jnp.left_shift(jnp.int32(1), FINE_BITS - 1 - i)
        c = count(sc_ref, key32_to_f32(k_base + cand))
        ok = c >= float(TOPK)
        return jnp.where(ok, cand, off), jnp.where(ok, c, n_ge)

    off, n_ge = lax.fori_loop(0, FINE_BITS, fine_body,
                              (jnp.zeros((1, QB), jnp.int32), jnp.full((1, QB), TOPK + 1.0, F32)))
    thr = key32_to_f32(k_base + off)
    thr_ref[...] = thr

    @pl.when(jnp.max(n_ge) > float(TOPK))
    def _():
        rows = lax.broadcasted_iota(jnp.int32, (KT, QB), 0)

        def tile_sum(fn):
            def body(jp, cnt):
                for u in range(2):
                    sel = fn(2 * jp + u)
                    for r in range(KT // CNT_ROWS):
                        cnt = cnt + sel[r * CNT_ROWS:(r + 1) * CNT_ROWS]
                return cnt
            cnt = lax.fori_loop(0, npairs, body, jnp.zeros((CNT_ROWS, QB), F32))
            return jnp.sum(cnt, axis=0, keepdims=True)

        n_gt = tile_sum(lambda j: jnp.where(sc_ref[j] > thr, 1.0, 0.0))
        ties_kept = float(TOPK) - n_gt

        def ties_before(p):
            return tile_sum(lambda j: jnp.where((sc_ref[j] == thr) & (rows < p - j * KT), 1.0, 0.0))

        def idx_body(i, p):
            cand = p | jnp.left_shift(jnp.int32(1), SEQ_BITS - 1 - i)
            return jnp.where(ties_before(cand) < ties_kept, cand, p)

        last = lax.fori_loop(0, SEQ_BITS, idx_body, jnp.zeros((1, QB), jnp.int32))

        def drop_body(j, carry):
            x = sc_ref[j]
            sc_ref[j] = jnp.where((x == thr) & (rows > last - j * KT), NEG_INF, x)
            return carry

        lax.fori_loop(0, 2 * npairs, drop_body, 0)

    m_ref[...] = jnp.full(m_ref.shape, NEG_INF, F32)
    l_ref[...] = jnp.zeros_like(l_ref)
    acc_ref[...] = jnp.zeros_like(acc_ref)
    ones_rows = jnp.ones((SUM_ROWS, KT), BF16)

    def att_step(logits=None, values=None):
        if logits is not None:
            j1, kind, slot1 = logits
            s_t = sc_ref[j1]
            mask = jnp.where((s_t >= thr_ref[...]) & (s_t > NEG_INF), 0.0, NEG_INF)
            k0 = pl.multiple_of(j1 * KT, KT)
        if values is not None:
            j2, slot2 = values
        pvs = []

        def update(h):
            a = al_ref[slot2, h]
            l_ref[h] = a * l_ref[h] + pvs[h][B_HEAD_DIM:B_HEAD_DIM + 1, :]
            acc_ref[h] = a * acc_ref[h] + pvs[h][:B_HEAD_DIM, :]

        for h in range(B_HEADS):
            hs = slice(h * B_HEAD_DIM, (h + 1) * B_HEAD_DIM)
            if logits is not None:
                s = jnp.dot(k_ref[pl.ds(k0, KT), hs], q_ref[hs, :], preferred_element_type=F32)
                if kind == "diag":
                    s = s + b0_ref[h]
                elif kind == "prev":
                    lo = jnp.concatenate([s[128:, :128] + b1_ref[h], s[128:, 128:]], axis=1)
                    s = jnp.concatenate([s[:128, :], lo], axis=0)
                s = s + mask
                s_ref[slot1, h] = s
                m_prev = m_ref[h]
                m_next = jnp.maximum(m_prev, jnp.max(s, axis=0, keepdims=True))
                m_safe = jnp.where(m_next == NEG_INF, 0.0, m_next)
                ms_ref[slot1, h] = m_safe
                al_ref[slot1, h] = jnp.exp2(m_prev - m_safe)
                m_ref[h] = m_next
            if values is not None:
                p = jnp.exp2(s_ref[slot2, h] - ms_ref[slot2, h]).astype(BF16)
                pvs.append(jnp.dot(jnp.concatenate([v_ref[j2, hs, :], ones_rows], axis=0), p,
                                   preferred_element_type=F32))
                if h >= UPDATE_LAG:
                    update(h - UPDATE_LAG)
        if values is not None:
            for h in range(B_HEADS - UPDATE_LAG, B_HEADS):
                update(h)

    odd = qb % 2

    @pl.when(qb == 0)
    def _():
        att_step(logits=(0, "diag", 0))

    @pl.when(qb == 1)
    def _():
        att_step(logits=(0, "prev", 1))

    @pl.when(qb >= 2)
    def _():
        @pl.when(odd == 0)
        def _():
            att_step(logits=(0, "far", 0))

        @pl.when(odd == 1)
        def _():
            att_step(logits=(0, "far", 1))
            att_step(logits=(1, "far", 0), values=(0, 1))

        def att_body(i, carry):
            j = odd + 2 * i
            att_step(logits=(j + 1, "far", 1), values=(j, 0))
            att_step(logits=(j + 2, "far", 0), values=(j + 1, 1))
            return carry

        lax.fori_loop(0, (qb - 2) // 2, att_body, 0)
        att_step(logits=(qb - 1, "prev", 1), values=(qb - 2, 0))

    @pl.when(qb >= 1)
    def _():
        att_step(logits=(qb, "diag", 0), values=(qb - 1, 1))

    att_step(values=(qb, 0))

    for h in range(B_HEADS):
        hs = slice(h * B_HEAD_DIM, (h + 1) * B_HEAD_DIM)
        bg = bg_ref[:, hs]
        o_ref[:, hs] = ((acc_ref[h] / l_ref[h]).T * (bg * jax.nn.sigmoid(bg))).astype(BF16)


def _dsa(rel_bias, q_idx_t, w_idx_t, q_t, p, k_idx, k, v_t):
    single = pl.Buffered(1)
    return pl.pallas_call(
        _dsa_kernel,
        grid=(SEQ // QB,),
        in_specs=[
            pl.BlockSpec(memory_space=pltpu.SMEM),
            pl.BlockSpec((IDX_HEADS, IDX_DIM, QB), lambda i: (0, 0, i)),
            pl.BlockSpec((128, QB), lambda i: (0, i)),
            pl.BlockSpec((B_WIDTH, QB), lambda i: (0, i)),
            pl.BlockSpec((QB, B_WIDTH), lambda i: (i, P_BG // B_WIDTH)),
            pl.BlockSpec((SEQ, IDX_DIM), lambda i: (0, 0), pipeline_mode=single),
            pl.BlockSpec((SEQ, B_WIDTH), lambda i: (0, 0), pipeline_mode=single),
            pl.BlockSpec((NKT, B_WIDTH, KT), lambda i: (0, 0, 0), pipeline_mode=single),
        ],
        out_specs=pl.BlockSpec((QB, B_WIDTH), lambda i: (i, 0)),
        out_shape=jax.ShapeDtypeStruct((SEQ, B_WIDTH), BF16),
        scratch_shapes=[
            pltpu.VMEM((NKT, KT, QB), F32),
            pltpu.VMEM((NKT, KT, QB), BF16),
            pltpu.VMEM((B_HEADS, KT, QB), F32),
            pltpu.VMEM((B_HEADS, 128, 128), F32),
            pltpu.VMEM((1, QB), F32),
            pltpu.VMEM((B_HEADS, 1, QB), F32),
            pltpu.VMEM((B_HEADS, 1, QB), F32),
            pltpu.VMEM((B_HEADS, B_HEAD_DIM, QB), F32),
            pltpu.VMEM((2, B_HEADS, KT, QB), F32),
            pltpu.VMEM((2, B_HEADS, 1, QB), F32),
            pltpu.VMEM((2, B_HEADS, 1, QB), F32),
        ],
        compiler_params=pltpu.CompilerParams(
            dimension_semantics=("arbitrary",), vmem_limit_bytes=VMEM_LIMIT),
        name="dsa",
    )(rel_bias, q_idx_t, w_idx_t, q_t, p, k_idx, k, v_t)


def _out_proj_kernel(ya_ref, yb_ref, ma_ref, mb_ref, x_ref, wpa_ref, wpb_ref, wo_ref, fnw_ref, o_ref):
    pa = jnp.dot(ya_ref[...], wpa_ref[...], preferred_element_type=F32)
    pb = jnp.dot(yb_ref[...], wpb_ref[...], preferred_element_type=F32)
    merged = jax.nn.sigmoid(ma_ref[...]) * pa + jax.nn.sigmoid(mb_ref[...]) * pb
    y = x_ref[...] + jnp.dot(merged.astype(BF16), wo_ref[...], preferred_element_type=F32)
    o_ref[...] = _rms(y, fnw_ref[...])


def _out_proj(y_a, y_b, p, x2, w_pa, w_pb, w_out, final_norm_w):
    tm = 512
    const = lambda i: (0, 0)
    single = pl.Buffered(1)
    return pl.pallas_call(
        _out_proj_kernel,
        grid=(SEQ // tm,),
        in_specs=[
            pl.BlockSpec((tm, A_WIDTH), lambda i: (i, 0)),
            pl.BlockSpec((tm, B_WIDTH), lambda i: (i, 0)),
            pl.BlockSpec((tm, D_MODEL), lambda i: (i, P_MA // D_MODEL)),
            pl.BlockSpec((tm, D_MODEL), lambda i: (i, P_MB // D_MODEL)),
            pl.BlockSpec((tm, D_MODEL), lambda i: (i, 0)),
            pl.BlockSpec((A_WIDTH, D_MODEL), const, pipeline_mode=single),
            pl.BlockSpec((B_WIDTH, D_MODEL), const, pipeline_mode=single),
            pl.BlockSpec((D_MODEL, D_MODEL), const, pipeline_mode=single),
            pl.BlockSpec((1, D_MODEL), const),
        ],
        out_specs=pl.BlockSpec((tm, D_MODEL), lambda i: (i, 0)),
        out_shape=jax.ShapeDtypeStruct((SEQ, D_MODEL), F32),
        compiler_params=pltpu.CompilerParams(
            dimension_semantics=("arbitrary",), vmem_limit_bytes=VMEM_LIMIT),
        name="out_proj",
    )(y_a, y_b, p, p, x2, w_pa, w_pb, w_out, final_norm_w)


def kernel(x, norm_w, w_in, lb_table, gnorm_a, q_norm_w, kv_norm_w, w_uq, w_qidx, w_ukv, kidx_norm_w,
           kidx_norm_b, w_pa, w_pb, w_out, rel_bias, final_norm_w):
    assert x.shape == (1, SEQ, D_MODEL) and w_in.shape[0] == 1
    x2 = x.reshape(SEQ, D_MODEL)
    p = _in_proj(x2, norm_w, w_in[0].astype(BF16))
    w_kv = w_ukv[0].astype(BF16).reshape(KV_RANK, B_HEADS, 2, B_HEAD_DIM)
    w_k = w_kv[:, :, 0, :].reshape(KV_RANK, B_WIDTH)
    w_v_t = w_kv[:, :, 1, :].reshape(KV_RANK, B_WIDTH).T
    q_t, q_idx_t, k, v_t, k_idx, w_idx_t = _proj2(
        p, q_norm_w, kv_norm_w, w_uq[0].astype(BF16).T, w_qidx[0].astype(BF16).T, w_k, w_v_t,
        kidx_norm_w, kidx_norm_b)
    y_a = _hgrn2(p, lb_table, gnorm_a)
    y_b = _dsa(rel_bias, q_idx_t, w_idx_t, q_t, p, k_idx, k, v_t)
    out = _out_proj(y_a, y_b, p, x2, w_pa[0].astype(BF16), w_pb[0].astype(BF16), w_out[0].astype(BF16),
                    final_norm_w.reshape(1, D_MODEL))
    return out.reshape(1, SEQ, D_MODEL)
```

```python
import math

import jax
import jax.numpy as jnp
import numpy as np
from jax import lax
from jax.experimental import pallas as pl
from jax.experimental.pallas import tpu as pltpu

F32 = jnp.float32
BF16 = jnp.bfloat16

D_MODEL = 2048
SEQ = 8192
EPS = 1e-6
A_HEADS = 8
A_HEAD_DIM = 128
A_WIDTH = 1024
B_HEADS = 8
B_HEAD_DIM = 128
B_WIDTH = 1024
Q_RANK = 512
KV_RANK = 256
IDX_HEADS = 16
IDX_DIM = 128
TOPK = 256
CHUNK = 64
REL_BUCKETS = 32
REL_MAX_DIST = 128

P_AQ, P_AF, P_AI, P_AG = 0, 1024, 2048, 3072
P_CQ = 4096
P_MISC = 4608
P_BG = 5120
P_MA = 6144
P_MB = 8192
P_WIDTH = 10240
IN_WIDTH = 10128
IN_TN = 1024
IN_SHIFT = 112

VMEM_LIMIT = 63 * 1024 * 1024

QB = 256
KT = 256
NKT = SEQ // KT
KSUB = 128
IDX_UNROLL = 4
CNT_ROWS = 32
assert SEQ // CNT_ROWS <= 256
SUM_ROWS = 16
UPDATE_LAG = 2

SUB = 32
MID = SUB // 2
TB = 256

NEG_INF = float("-inf")
LOG2E = math.log2(math.e)
I16_MIN = -32768
I32_MIN = -2147483648
KEY16_NEG_INF = -32641
SEQ_BITS = 13
assert SEQ == 1 << SEQ_BITS
FINE_BELOW = 1 << 16
FINE_BITS = 17

NT_DIMS = (((1,), (1,)), ((), ()))
TN_DIMS = (((0,), (0,)), ((), ()))


def _t5_thresholds():
    half = REL_BUCKETS // 2
    max_exact = half // 2
    n = np.arange(1, 4 * REL_MAX_DIST, dtype=np.int64)
    large = max_exact + (np.log(np.maximum(n, 1).astype(np.float64) / max_exact)
                         / math.log(REL_MAX_DIST / max_exact) * (half - max_exact)).astype(np.int32)
    large = np.minimum(large, half - 1)
    b = np.where(n < max_exact, n, large)
    thr = [int(n[i + 1]) for i in np.nonzero(np.diff(b))[0] if n[i + 1] > max_exact]
    assert len(thr) == half - 1 - max_exact and b[-1] == half - 1 and thr[-1] < REL_MAX_DIST
    return max_exact, half, thr


T5_MAX_EXACT, T5_HALF, T5_THR = _t5_thresholds()


def _in_proj_kernel(x_ref, nw_ref, w_ref, o_ref, h_ref):
    @pl.when(pl.program_id(1) == 0)
    def _():
        x = x_ref[...]
        ms = jnp.mean(x * x, axis=-1, keepdims=True)
        h_ref[...] = (x * lax.rsqrt(ms + EPS) * nw_ref[...]).astype(BF16)

    j = pl.program_id(1)

    @pl.when(j < P_BG // IN_TN)
    def _():
        o_ref[...] = jnp.dot(h_ref[...], w_ref[:, :IN_TN], preferred_element_type=F32)

    @pl.when(j >= P_BG // IN_TN)
    def _():
        w = pltpu.roll(w_ref[...], IN_TN + 128 - (128 - IN_SHIFT), axis=1)[:, :IN_TN]
        o_ref[...] = jnp.dot(h_ref[...], w, preferred_element_type=F32)


def _in_proj(x2, norm_w, w_in_b):
    tm, tn = 1024, IN_TN
    n_plain = P_BG // tn

    def w_index(i, j):
        return 0, pl.multiple_of(jnp.where(j < n_plain, j * tn, j * tn - 128), 128)

    return pl.pallas_call(
        _in_proj_kernel,
        grid=(SEQ // tm, P_WIDTH // tn),
        in_specs=[
            pl.BlockSpec((tm, D_MODEL), lambda i, j: (i, 0)),
            pl.BlockSpec((1, D_MODEL), lambda i, j: (0, 0)),
            pl.BlockSpec((pl.Element(D_MODEL), pl.Element(tn + 128, padding=(0, P_WIDTH - IN_WIDTH))), w_index),
        ],
        out_specs=pl.BlockSpec((tm, tn), lambda i, j: (i, j)),
        out_shape=jax.ShapeDtypeStruct((SEQ, P_WIDTH), F32),
        scratch_shapes=[pltpu.VMEM((tm, D_MODEL), BF16)],
        compiler_params=pltpu.CompilerParams(
            dimension_semantics=("arbitrary", "arbitrary"), vmem_limit_bytes=VMEM_LIMIT),
        name="in_proj",
    )(x2, norm_w, w_in_b)


def _rms(x, w):
    return x * lax.rsqrt(jnp.mean(x * x, axis=-1, keepdims=True) + EPS) * w


def _proj2_kernel(cq_ref, misc_ref, qnw_ref, kvnw_ref, wuqt_ref, wqit_ref, wk_ref, wvt_ref, lnw_ref, lnb_ref,
                  qt_ref, qit_ref, k_ref, vt_ref, kidx_ref, wt_ref):
    cqn = _rms(cq_ref[...], qnw_ref[...]).astype(BF16)
    qt = lax.dot_general(wuqt_ref[...], cqn, NT_DIMS, preferred_element_type=F32)
    qt_ref[...] = (qt * (B_HEAD_DIM ** -0.5 * LOG2E)).astype(BF16)
    qit = lax.dot_general(wqit_ref[...], cqn, NT_DIMS, preferred_element_type=F32)
    for h in range(IDX_HEADS):
        qit_ref[h] = qit[h * IDX_DIM:(h + 1) * IDX_DIM, :].astype(BF16)
    misc = misc_ref[...]
    ckvn = _rms(misc[:, :KV_RANK], kvnw_ref[...]).astype(BF16)
    k_ref[...] = jnp.dot(ckvn, wk_ref[...], preferred_element_type=F32).astype(BF16)
    vt = lax.dot_general(wvt_ref[...], ckvn, NT_DIMS, preferred_element_type=F32).astype(BF16)
    for t in range(vt_ref.shape[0]):
        vt_ref[t] = vt[:, t * KT:(t + 1) * KT]
    kr = misc[:, KV_RANK:KV_RANK + IDX_DIM]
    mu = jnp.mean(kr, axis=-1, keepdims=True)
    var = jnp.mean(jnp.square(kr - mu), axis=-1, keepdims=True)
    kidx_ref[...] = ((kr - mu) * lax.rsqrt(var + EPS) * lnw_ref[...] + lnb_ref[...]).astype(BF16)
    wt_ref[...] = (misc[:, KV_RANK + IDX_DIM:] * (IDX_HEADS ** -0.5 * IDX_DIM ** -0.5)).T


def _proj2(p, q_norm_w, kv_norm_w, w_uq_t, w_qidx_t, w_k, w_v_t, ln_w, ln_b):
    tm = 4 * KT
    const = lambda i: (0, 0)
    return pl.pallas_call(
        _proj2_kernel,
        grid=(SEQ // tm,),
        in_specs=[
            pl.BlockSpec((tm, 512), lambda i: (i, P_CQ // 512)),
            pl.BlockSpec((tm, 512), lambda i: (i, P_MISC // 512)),
            pl.BlockSpec((1, Q_RANK), const),
            pl.BlockSpec((1, KV_RANK), const),
            pl.BlockSpec((B_WIDTH, Q_RANK), const),
            pl.BlockSpec((IDX_HEADS * IDX_DIM, Q_RANK), const),
            pl.BlockSpec((KV_RANK, B_WIDTH), const),
            pl.BlockSpec((B_WIDTH, KV_RANK), const),
            pl.BlockSpec((1, IDX_DIM), const),
            pl.BlockSpec((1, IDX_DIM), const),
        ],
        out_specs=[
            pl.BlockSpec((B_WIDTH, tm), lambda i: (0, i)),
            pl.BlockSpec((IDX_HEADS, IDX_DIM, tm), lambda i: (0, 0, i)),
            pl.BlockSpec((tm, B_WIDTH), lambda i: (i, 0)),
            pl.BlockSpec((tm // KT, B_WIDTH, KT), lambda i: (i, 0, 0)),
            pl.BlockSpec((tm, IDX_DIM), lambda i: (i, 0)),
            pl.BlockSpec((128, tm), lambda i: (0, i)),
        ],
        out_shape=[
            jax.ShapeDtypeStruct((B_WIDTH, SEQ), BF16),
            jax.ShapeDtypeStruct((IDX_HEADS, IDX_DIM, SEQ), BF16),
            jax.ShapeDtypeStruct((SEQ, B_WIDTH), BF16),
            jax.ShapeDtypeStruct((NKT, B_WIDTH, KT), BF16),
            jax.ShapeDtypeStruct((SEQ, IDX_DIM), BF16),
            jax.ShapeDtypeStruct((128, SEQ), F32),
        ],
        compiler_params=pltpu.CompilerParams(
            dimension_semantics=("arbitrary",), vmem_limit_bytes=VMEM_LIMIT),
        name="proj2",
    )(p, p, q_norm_w, kv_norm_w, w_uq_t, w_qidx_t, w_k, w_v_t, ln_w, ln_b)


def _sigmoid(x):
    return 0.5 * jnp.tanh(0.5 * x) + 0.5


def _split3(x):
    x1 = x.astype(BF16)
    r1 = x - x1.astype(F32)
    x2 = r1.astype(BF16)
    x3 = (r1 - x2.astype(F32)).astype(BF16)
    return x1, x2, x3


def _hgrn2_kernel(aq_ref, af_ref, ai_ref, ag_ref, lbt_ref, gn_ref, o_ref,
                  st_ref, sel_ref, a_ref, a2_ref, b2_ref, kd_ref, v_ref, e_ref, oacc_ref):
    @pl.when(pl.program_id(0) == 0)
    def _():
        st_ref[...] = jnp.zeros_like(st_ref)
        r = lax.broadcasted_iota(jnp.int32, (TB, TB), 0)
        c = lax.broadcasted_iota(jnp.int32, (TB, TB), 1)
        same = (r // SUB) == (c // SUB)
        sel_ref[...] = jnp.where(same & (c <= r), 1.0, 0.0).astype(BF16)

    lbt = lbt_ref[...]
    mx = jnp.maximum(lbt[0:1], lbt[1:2])
    e0 = jnp.exp(lbt[0:1] - mx)
    e1 = jnp.exp(lbt[1:2] - mx)
    lb = e0 / (e0 + e1)

    f = lb + (1.0 - lb) * _sigmoid(af_ref[...])
    g = jnp.log(f)
    sel = sel_ref[...]
    g1, g2, g3 = _split3(g)
    cum = (jnp.dot(sel, g1, preferred_element_type=F32) + jnp.dot(sel, g2, preferred_element_type=F32)
           + jnp.dot(sel, g3, preferred_element_type=F32))
    cum3 = cum.reshape(TB // SUB, SUB, A_WIDTH)
    mid = jnp.broadcast_to(cum3[:, MID - 1:MID, :], cum3.shape).reshape(TB, A_WIDTH)
    tot = jnp.broadcast_to(cum3[:, SUB - 1:SUB, :], cum3.shape).reshape(TB, A_WIDTH)
    aq = aq_ref[...]
    q = aq * _sigmoid(aq) * (A_HEAD_DIM ** -0.5)
    kk = 1.0 - f
    a_ref[...] = (q * jnp.exp(cum)).astype(BF16)
    a2_ref[...] = (q * jnp.exp(cum - mid)).astype(BF16)
    b2_ref[...] = (kk * jnp.exp(mid - cum)).astype(BF16)
    kd_ref[...] = (kk * jnp.exp(tot - cum)).astype(BF16)
    v_ref[...] = ai_ref[...].astype(BF16)
    e_ref[...] = jnp.exp(tot)

    tr = lax.broadcasted_iota(jnp.int32, (SUB, SUB), 0)
    tc = lax.broadcasted_iota(jnp.int32, (SUB, SUB), 1)
    causal = tc <= tr

    def chunk_body(c, carry):
        r0 = pl.multiple_of(c * SUB, SUB)
        heads = [slice(h * A_HEAD_DIM, (h + 1) * A_HEAD_DIM) for h in range(A_HEADS)]
        ah = [a_ref[pl.ds(r0, SUB), cs] for cs in heads]
        vh = [v_ref[pl.ds(r0, SUB), cs] for cs in heads]
        sc = [lax.dot_general(a2_ref[pl.ds(r0, SUB), heads[h]], b2_ref[pl.ds(r0, SUB), heads[h]], NT_DIMS,
                              preferred_element_type=F32) for h in range(A_HEADS)]
        inter = []
        for h in range(A_HEADS):
            st = st_ref[h]
            inter.append(lax.dot_general(ah[h], st.astype(BF16), NT_DIMS, preferred_element_type=F32))
            dst = lax.dot_general(vh[h], kd_ref[pl.ds(r0, SUB), heads[h]], TN_DIMS, preferred_element_type=F32)
            st_ref[h] = st * e_ref[pl.ds(r0, 1), heads[h]] + dst
        for h in range(A_HEADS):
            pm = jnp.where(causal, sc[h], 0.0).astype(BF16)
            oacc_ref[pl.ds(r0, SUB), heads[h]] = inter[h] + jnp.dot(pm, vh[h], preferred_element_type=F32)
        return carry

    lax.fori_loop(0, TB // SUB, chunk_body, 0, unroll=True)

    gn = gn_ref[...]
    ag = ag_ref[...]
    gate = ag * _sigmoid(ag)
    for h in range(A_HEADS):
        cs = slice(h * A_HEAD_DIM, (h + 1) * A_HEAD_DIM)
        o = oacc_ref[:, cs]
        on = o * lax.rsqrt(jnp.mean(o * o, axis=-1, keepdims=True) + EPS) * gn
        o_ref[:, cs] = (on * gate[:, cs]).astype(BF16)


def _hgrn2(p, lb_table, gnorm):
    blk = lambda col: pl.BlockSpec((TB, A_WIDTH), lambda i: (i, col // A_WIDTH))
    return pl.pallas_call(
        _hgrn2_kernel,
        grid=(SEQ // TB,),
        in_specs=[blk(P_AQ), blk(P_AF), blk(P_AI), blk(P_AG),
                  pl.BlockSpec((2, A_WIDTH), lambda i: (0, 0)),
                  pl.BlockSpec((1, A_HEAD_DIM), lambda i: (0, 0))],
        out_specs=pl.BlockSpec((TB, A_WIDTH), lambda i: (i, 0)),
        out_shape=jax.ShapeDtypeStruct((SEQ, A_WIDTH), BF16),
        scratch_shapes=[
            pltpu.VMEM((A_HEADS, A_HEAD_DIM, A_HEAD_DIM), F32),
            pltpu.VMEM((TB, TB), BF16),
            pltpu.VMEM((TB, A_WIDTH), BF16),
            pltpu.VMEM((TB, A_WIDTH), BF16),
            pltpu.VMEM((TB, A_WIDTH), BF16),
            pltpu.VMEM((TB, A_WIDTH), BF16),
            pltpu.VMEM((TB, A_WIDTH), BF16),
            pltpu.VMEM((TB, A_WIDTH), F32),
            pltpu.VMEM((TB, A_WIDTH), F32),
        ],
        compiler_params=pltpu.CompilerParams(
            dimension_semantics=("arbitrary",), vmem_limit_bytes=VMEM_LIMIT),
        name="hgrn2",
    )(p, p, p, p, lb_table, gnorm)


def _t5_bucket(rel):
    n = jnp.abs(rel)
    large = jnp.full(rel.shape, T5_MAX_EXACT, jnp.int32)
    for t in T5_THR:
        large = large + jnp.where(n >= t, 1, 0)
    return jnp.where(rel > 0, T5_HALF, 0) + jnp.where(n < T5_MAX_EXACT, n, large)


def _dsa_kernel(relb_ref, qi_ref, w_ref, q_ref, bg_ref, kidx_ref, k_ref, v_ref, o_ref,
                sc_ref, hi_ref, b0_ref, b1_ref, thr_ref, m_ref, l_ref, acc_ref, s_ref, ms_ref, al_ref):
    qb = pl.program_id(0)

    @pl.when(qb == 0)
    def _():
        def bias_tile(rel, h):
            bucket = _t5_bucket(rel)
            val = jnp.zeros(rel.shape, F32)
            for b in range(REL_BUCKETS):
                val = jnp.where(bucket == b, relb_ref[b, h], val)
            return (val - relb_ref[T5_HALF - 1, h]) * LOG2E

        r0 = lax.broadcasted_iota(jnp.int32, (KT, QB), 0)
        c0 = lax.broadcasted_iota(jnp.int32, (KT, QB), 1)
        r1 = lax.broadcasted_iota(jnp.int32, (128, 128), 0)
        c1 = lax.broadcasted_iota(jnp.int32, (128, 128), 1)
        for h in range(B_HEADS):
            b0_ref[h] = bias_tile(r0 - c0, h)
            b1_ref[h] = bias_tile(r1 - 128 - c1, h)

    def idx_tile(j, diag):
        for ks in range(KT // KSUB):
            kt = kidx_ref[pl.ds(pl.multiple_of(j * KT + ks * KSUB, KSUB), KSUB), :]
            acc = jnp.zeros((KSUB, QB), F32)
            for h in range(IDX_HEADS):
                r = jnp.dot(kt, qi_ref[h], preferred_element_type=F32)
                acc = acc + w_ref[h:h + 1, :] * jnp.maximum(r, 0.0)
            if diag:
                rk = ks * KSUB + lax.broadcasted_iota(jnp.int32, (KSUB, QB), 0)
                ct = lax.broadcasted_iota(jnp.int32, (KSUB, QB), 1)
                acc = jnp.where((rk // CHUNK) <= (ct // CHUNK), acc, NEG_INF)
            sc_ref[j, ks * KSUB:(ks + 1) * KSUB, :] = acc
            hi_ref[j, ks * KSUB:(ks + 1) * KSUB, :] = acc.astype(BF16)

    def idx_body(i, carry):
        for u in range(IDX_UNROLL):
            idx_tile(IDX_UNROLL * i + u, False)
        return carry

    lax.fori_loop(0, qb // IDX_UNROLL, idx_body, 0)
    for rem in range(IDX_UNROLL):
        @pl.when(qb % IDX_UNROLL == rem)
        def _():
            for u in range(rem):
                idx_tile(qb - rem + u, False)
            idx_tile(qb, True)

    npairs = (qb + 2) // 2

    @pl.when(qb % 2 == 0)
    def _():
        sc_ref[qb + 1] = jnp.full((KT, QB), NEG_INF, F32)
        hi_ref[qb + 1] = jnp.full((KT, QB), NEG_INF, BF16)

    def count(ref, thr):
        one = jnp.ones((KT, QB), ref.dtype)
        zero = jnp.zeros((KT, QB), ref.dtype)

        def tiles(first, n, cnt):
            for u in range(n):
                sel = jnp.where(ref[first + u] >= thr, one, zero)
                for r in range(KT // CNT_ROWS):
                    cnt = cnt + sel[r * CNT_ROWS:(r + 1) * CNT_ROWS]
            return cnt

        cnt = lax.fori_loop(0, npairs // 2, lambda i, c: tiles(4 * i, 4, c), jnp.zeros((CNT_ROWS, QB), ref.dtype))
        cnt = lax.fori_loop(0, npairs % 2, lambda i, c: tiles(2 * (npairs - 1), 2, c), cnt)
        return jnp.sum(cnt.astype(F32), axis=0, keepdims=True)

    def key16_to_bf16(k):
        bits = jnp.where(k >= 0, k, k ^ 0x7FFF).astype(jnp.int16)
        return lax.bitcast_convert_type(bits, BF16)

    def key32_to_f32(k):
        return lax.bitcast_convert_type(jnp.where(k >= 0, k, k ^ 0x7FFFFFFF), F32)

    c0 = count(hi_ref, jnp.zeros((1, QB), BF16))
    k16 = jnp.where(c0 >= float(TOPK), 0, I16_MIN).astype(jnp.int32)

    def coarse_body(i, k):
        cand = k | jnp.left_shift(jnp.int32(1), 14 - i)
        return jnp.where(count(hi_ref, key16_to_bf16(cand)) >= float(TOPK), cand, k)

    k16 = jnp.maximum(lax.fori_loop(0, 15, coarse_body, k16), KEY16_NEG_INF)
    k_mid = jnp.where(k16 >= 0, k16 << 16, (k16 << 16) | 0xFFFF)
    k_base = jnp.maximum(k_mid, I32_MIN + FINE_BELOW) - FINE_BELOW

    def fine_body(i, carry):
        off, n_ge = carry
        cand = off | jnp.left_shift(jnp.int32(1), FINE_BITS - 1 - i)
        c = count(sc_ref, key32_to_f32(k_base + cand))
        ok = c >= float(TOPK)
        return jnp.where(ok, cand, off), jnp.where(ok, c, n_ge)

    off, n_ge = lax.fori_loop(0, FINE_BITS, fine_body,
                              (jnp.zeros((1, QB), jnp.int32), jnp.full((1, QB), TOPK + 1.0, F32)))
    thr = key32_to_f32(k_base + off)
    thr_ref[...] = thr

    @pl.when(jnp.max(n_ge) > float(TOPK))
    def _():
        rows = lax.broadcasted_iota(jnp.int32, (KT, QB), 0)

        def tile_sum(fn):
            def body(jp, cnt):
                for u in range(2):
                    sel = fn(2 * jp + u)
                    for r in range(KT // CNT_ROWS):
                        cnt = cnt + sel[r * CNT_ROWS:(r + 1) * CNT_ROWS]
                return cnt
            cnt = lax.fori_loop(0, npairs, body, jnp.zeros((CNT_ROWS, QB), F32))
            return jnp.sum(cnt, axis=0, keepdims=True)

        n_gt = tile_sum(lambda j: jnp.where(sc_ref[j] > thr, 1.0, 0.0))
        ties_kept = float(TOPK) - n_gt

        def ties_before(p):
            return tile_sum(lambda j: jnp.where((sc_ref[j] == thr) & (rows < p - j * KT), 1.0, 0.0))

        def idx_body(i, p):
            cand = p | jnp.left_shift(jnp.int32(1), SEQ_BITS - 1 - i)
            return jnp.where(ties_before(cand) < ties_kept, cand, p)

        last = lax.fori_loop(0, SEQ_BITS, idx_body, jnp.zeros((1, QB), jnp.int32))

        def drop_body(j, carry):
            x = sc_ref[j]
            sc_ref[j] = jnp.where((x == thr) & (rows > last - j * KT), NEG_INF, x)
            return carry

        lax.fori_loop(0, 2 * npairs, drop_body, 0)

    m_ref[...] = jnp.full(m_ref.shape, NEG_INF, F32)
    l_ref[...] = jnp.zeros_like(l_ref)
    acc_ref[...] = jnp.zeros_like(acc_ref)
    ones_rows = jnp.ones((SUM_ROWS, KT), BF16)

    def att_step(logits=None, values=None):
        if logits is not None:
            j1, kind, slot1 = logits
            s_t = sc_ref[j1]
            mask = jnp.where((s_t >= thr_ref[...]) & (s_t > NEG_INF), 0.0, NEG_INF)
            k0 = pl.multiple_of(j1 * KT, KT)
        if values is not None:
            j2, slot2 = values
        pvs = []

        def update(h):
            a = al_ref[slot2, h]
            l_ref[h] = a * l_ref[h] + pvs[h][B_HEAD_DIM:B_HEAD_DIM + 1, :]
            acc_ref[h] = a * acc_ref[h] + pvs[h][:B_HEAD_DIM, :]

        for h in range(B_HEADS):
            hs = slice(h * B_HEAD_DIM, (h + 1) * B_HEAD_DIM)
            if logits is not None:
                s = jnp.dot(k_ref[pl.ds(k0, KT), hs], q_ref[hs, :], preferred_element_type=F32)
                if kind == "diag":
                    s = s + b0_ref[h]
                elif kind == "prev":
                    lo = jnp.concatenate([s[128:, :128] + b1_ref[h], s[128:, 128:]], axis=1)
                    s = jnp.concatenate([s[:128, :], lo], axis=0)
                s = s + mask
                s_ref[slot1, h] = s
                m_prev = m_ref[h]
                m_next = jnp.maximum(m_prev, jnp.max(s, axis=0, keepdims=True))
                m_safe = jnp.where(m_next == NEG_INF, 0.0, m_next)
                ms_ref[slot1, h] = m_safe
                al_ref[slot1, h] = jnp.exp2(m_prev - m_safe)
                m_ref[h] = m_next
            if values is not None:
                p = jnp.exp2(s_ref[slot2, h] - ms_ref[slot2, h]).astype(BF16)
                pvs.append(jnp.dot(jnp.concatenate([v_ref[j2, hs, :], ones_rows], axis=0), p,
                                   preferred_element_type=F32))
                if h >= UPDATE_LAG:
                    update(h - UPDATE_LAG)
        if values is not None:
            for h in range(B_HEADS - UPDATE_LAG, B_HEADS):
                update(h)

    odd = qb % 2

    @pl.when(qb == 0)
    def _():
        att_step(logits=(0, "diag", 0))

    @pl.when(qb == 1)
    def _():
        att_step(logits=(0, "prev", 1))

    @pl.when(qb >= 2)
    def _():
        @pl.when(odd == 0)
        def _():
            att_step(logits=(0, "far", 0))

        @pl.when(odd == 1)
        def _():
            att_step(logits=(0, "far", 1))
            att_step(logits=(1, "far", 0), values=(0, 1))

        def att_pairs(j, n):
            for u in range(n):
                att_step(logits=(j + 2 * u + 1, "far", 1), values=(j + 2 * u, 0))
                att_step(logits=(j + 2 * u + 2, "far", 0), values=(j + 2 * u + 1, 1))

        def att_body(i, carry):
            att_pairs(odd + 4 * i, 2)
            return carry

        npair_far = (qb - 2) // 2
        lax.fori_loop(0, npair_far // 2, att_body, 0)

        @pl.when(npair_far % 2 == 1)
        def _():
            att_pairs(odd + 2 * (npair_far - 1), 1)

        att_step(logits=(qb - 1, "prev", 1), values=(qb - 2, 0))

    @pl.when(qb >= 1)
    def _():
        att_step(logits=(qb, "diag", 0), values=(qb - 1, 1))

    att_step(values=(qb, 0))

    for h in range(B_HEADS):
        hs = slice(h * B_HEAD_DIM, (h + 1) * B_HEAD_DIM)
        bg = bg_ref[:, hs]
        o_ref[:, hs] = ((acc_ref[h] / l_ref[h]).T * (bg * jax.nn.sigmoid(bg))).astype(BF16)


def _dsa(rel_bias, q_idx_t, w_idx_t, q_t, p, k_idx, k, v_t):
    single = pl.Buffered(1)
    return pl.pallas_call(
        _dsa_kernel,
        grid=(SEQ // QB,),
        in_specs=[
            pl.BlockSpec(memory_space=pltpu.SMEM),
            pl.BlockSpec((IDX_HEADS, IDX_DIM, QB), lambda i: (0, 0, i)),
            pl.BlockSpec((128, QB), lambda i: (0, i)),
            pl.BlockSpec((B_WIDTH, QB), lambda i: (0, i)),
            pl.BlockSpec((QB, B_WIDTH), lambda i: (i, P_BG // B_WIDTH)),
            pl.BlockSpec((SEQ, IDX_DIM), lambda i: (0, 0), pipeline_mode=single),
            pl.BlockSpec((SEQ, B_WIDTH), lambda i: (0, 0), pipeline_mode=single),
            pl.BlockSpec((NKT, B_WIDTH, KT), lambda i: (0, 0, 0), pipeline_mode=single),
        ],
        out_specs=pl.BlockSpec((QB, B_WIDTH), lambda i: (i, 0)),
        out_shape=jax.ShapeDtypeStruct((SEQ, B_WIDTH), BF16),
        scratch_shapes=[
            pltpu.VMEM((NKT, KT, QB), F32),
            pltpu.VMEM((NKT, KT, QB), BF16),
            pltpu.VMEM((B_HEADS, KT, QB), F32),
            pltpu.VMEM((B_HEADS, 128, 128), F32),
            pltpu.VMEM((1, QB), F32),
            pltpu.VMEM((B_HEADS, 1, QB), F32),
            pltpu.VMEM((B_HEADS, 1, QB), F32),
            pltpu.VMEM((B_HEADS, B_HEAD_DIM, QB), F32),
            pltpu.VMEM((2, B_HEADS, KT, QB), F32),
            pltpu.VMEM((2, B_HEADS, 1, QB), F32),
            pltpu.VMEM((2, B_HEADS, 1, QB), F32),
        ],
        compiler_params=pltpu.CompilerParams(
            dimension_semantics=("arbitrary",), vmem_limit_bytes=VMEM_LIMIT),
        name="dsa",
    )(rel_bias, q_idx_t, w_idx_t, q_t, p, k_idx, k, v_t)


def _out_proj_kernel(ya_ref, yb_ref, ma_ref, mb_ref, x_ref, wpa_ref, wpb_ref, wo_ref, fnw_ref, o_ref):
    pa = jnp.dot(ya_ref[...], wpa_ref[...], preferred_element_type=F32)
    pb = jnp.dot(yb_ref[...], wpb_ref[...], preferred_element_type=F32)
    merged = jax.nn.sigmoid(ma_ref[...]) * pa + jax.nn.sigmoid(mb_ref[...]) * pb
    y = x_ref[...] + jnp.dot(merged.astype(BF16), wo_ref[...], preferred_element_type=F32)
    o_ref[...] = _rms(y, fnw_ref[...])


def _out_proj(y_a, y_b, p, x2, w_pa, w_pb, w_out, final_norm_w):
    tm = 512
    const = lambda i: (0, 0)
    single = pl.Buffered(1)
    return pl.pallas_call(
        _out_proj_kernel,
        grid=(SEQ // tm,),
        in_specs=[
            pl.BlockSpec((tm, A_WIDTH), lambda i: (i, 0)),
            pl.BlockSpec((tm, B_WIDTH), lambda i: (i, 0)),
            pl.BlockSpec((tm, D_MODEL), lambda i: (i, P_MA // D_MODEL)),
            pl.BlockSpec((tm, D_MODEL), lambda i: (i, P_MB // D_MODEL)),
            pl.BlockSpec((tm, D_MODEL), lambda i: (i, 0)),
            pl.BlockSpec((A_WIDTH, D_MODEL), const, pipeline_mode=single),
            pl.BlockSpec((B_WIDTH, D_MODEL), const, pipeline_mode=single),
            pl.BlockSpec((D_MODEL, D_MODEL), const, pipeline_mode=single),
            pl.BlockSpec((1, D_MODEL), const),
        ],
        out_specs=pl.BlockSpec((tm, D_MODEL), lambda i: (i, 0)),
        out_shape=jax.ShapeDtypeStruct((SEQ, D_MODEL), F32),
        compiler_params=pltpu.CompilerParams(
            dimension_semantics=("arbitrary",), vmem_limit_bytes=VMEM_LIMIT),
        name="out_proj",
    )(y_a, y_b, p, p, x2, w_pa, w_pb, w_out, final_norm_w)


def kernel(x, norm_w, w_in, lb_table, gnorm_a, q_norm_w, kv_norm_w, w_uq, w_qidx, w_ukv, kidx_norm_w,
           kidx_norm_b, w_pa, w_pb, w_out, rel_bias, final_norm_w):
    assert x.shape == (1, SEQ, D_MODEL) and w_in.shape[0] == 1
    x2 = x.reshape(SEQ, D_MODEL)
    p = _in_proj(x2, norm_w, w_in[0].astype(BF16))
    w_kv = w_ukv[0].astype(BF16).reshape(KV_RANK, B_HEADS, 2, B_HEAD_DIM)
    w_k = w_kv[:, :, 0, :].reshape(KV_RANK, B_WIDTH)
    w_v_t = w_kv[:, :, 1, :].reshape(KV_RANK, B_WIDTH).T
    q_t, q_idx_t, k, v_t, k_idx, w_idx_t = _proj2(
        p, q_norm_w, kv_norm_w, w_uq[0].astype(BF16).T, w_qidx[0].astype(BF16).T, w_k, w_v_t,
        kidx_norm_w, kidx_norm_b)
    y_a = _hgrn2(p, lb_table, gnorm_a)
    y_b = _dsa(rel_bias, q_idx_t, w_idx_t, q_t, p, k_idx, k, v_t)
    out = _out_proj(y_a, y_b, p, x2, w_pa[0].astype(BF16), w_pb[0].astype(BF16), w_out[0].astype(BF16),
                    final_norm_w.reshape(1, D_MODEL))
    return out.reshape(1, SEQ, D_MODEL)
```

```python
import math

import jax
import jax.numpy as jnp
import numpy as np
from jax import lax
from jax.experimental import pallas as pl
from jax.experimental.pallas import tpu as pltpu

F32 = jnp.float32
BF16 = jnp.bfloat16

D_MODEL = 2048
SEQ = 8192
EPS = 1e-6
A_HEADS = 8
A_HEAD_DIM = 128
A_WIDTH = 1024
B_HEADS = 8
B_HEAD_DIM = 128
B_WIDTH = 1024
Q_RANK = 512
KV_RANK = 256
IDX_HEADS = 16
IDX_DIM = 128
TOPK = 256
CHUNK = 64
REL_BUCKETS = 32
REL_MAX_DIST = 128

P_AQ, P_AF, P_AI, P_AG = 0, 1024, 2048, 3072
P_CQ = 4096
P_MISC = 4608
P_BG = 5120
P_MA = 6144
P_MB = 8192
P_WIDTH = 10240
IN_WIDTH = 10128
IN_TN = 1024
IN_SHIFT = 112

VMEM_LIMIT = 63 * 1024 * 1024

QB = 256
KT = 256
NKT = SEQ // KT
KSUB = 128
IDX_UNROLL = 4
CNT_ROWS = 32
assert SEQ // CNT_ROWS <= 256
SUM_ROWS = 16
UPDATE_LAG = 2
ATT_PAIRS = 4

SUB = 32
MID = SUB // 2
TB = 256

NEG_INF = float("-inf")
LOG2E = math.log2(math.e)
I16_MIN = -32768
I32_MIN = -2147483648
KEY16_NEG_INF = -32641
SEQ_BITS = 13
assert SEQ == 1 << SEQ_BITS
FINE_BELOW = 1 << 16
FINE_BITS = 17

NT_DIMS = (((1,), (1,)), ((), ()))
TN_DIMS = (((0,), (0,)), ((), ()))


def _t5_thresholds():
    half = REL_BUCKETS // 2
    max_exact = half // 2
    n = np.arange(1, 4 * REL_MAX_DIST, dtype=np.int64)
    large = max_exact + (np.log(np.maximum(n, 1).astype(np.float64) / max_exact)
                         / math.log(REL_MAX_DIST / max_exact) * (half - max_exact)).astype(np.int32)
    large = np.minimum(large, half - 1)
    b = np.where(n < max_exact, n, large)
    thr = [int(n[i + 1]) for i in np.nonzero(np.diff(b))[0] if n[i + 1] > max_exact]
    assert len(thr) == half - 1 - max_exact and b[-1] == half - 1 and thr[-1] < REL_MAX_DIST
    return max_exact, half, thr


T5_MAX_EXACT, T5_HALF, T5_THR = _t5_thresholds()


def _in_proj_kernel(x_ref, nw_ref, w_ref, o_ref, h_ref):
    @pl.when(pl.program_id(1) == 0)
    def _():
        x = x_ref[...]
        ms = jnp.mean(x * x, axis=-1, keepdims=True)
        h_ref[...] = (x * lax.rsqrt(ms + EPS) * nw_ref[...]).astype(BF16)

    j = pl.program_id(1)

    @pl.when(j < P_BG // IN_TN)
    def _():
        o_ref[...] = jnp.dot(h_ref[...], w_ref[:, :IN_TN], preferred_element_type=F32)

    @pl.when(j >= P_BG // IN_TN)
    def _():
        w = pltpu.roll(w_ref[...], IN_TN + 128 - (128 - IN_SHIFT), axis=1)[:, :IN_TN]
        o_ref[...] = jnp.dot(h_ref[...], w, preferred_element_type=F32)


def _in_proj(x2, norm_w, w_in_b):
    tm, tn = 1024, IN_TN
    n_plain = P_BG // tn

    def w_index(i, j):
        return 0, pl.multiple_of(jnp.where(j < n_plain, j * tn, j * tn - 128), 128)

    return pl.pallas_call(
        _in_proj_kernel,
        grid=(SEQ // tm, P_WIDTH // tn),
        in_specs=[
            pl.BlockSpec((tm, D_MODEL), lambda i, j: (i, 0)),
            pl.BlockSpec((1, D_MODEL), lambda i, j: (0, 0)),
            pl.BlockSpec((pl.Element(D_MODEL), pl.Element(tn + 128, padding=(0, P_WIDTH - IN_WIDTH))), w_index),
        ],
        out_specs=pl.BlockSpec((tm, tn), lambda i, j: (i, j)),
        out_shape=jax.ShapeDtypeStruct((SEQ, P_WIDTH), F32),
        scratch_shapes=[pltpu.VMEM((tm, D_MODEL), BF16)],
        compiler_params=pltpu.CompilerParams(
            dimension_semantics=("arbitrary", "arbitrary"), vmem_limit_bytes=VMEM_LIMIT),
        name="in_proj",
    )(x2, norm_w, w_in_b)


def _rms(x, w):
    return x * lax.rsqrt(jnp.mean(x * x, axis=-1, keepdims=True) + EPS) * w


def _proj2_kernel(cq_ref, misc_ref, qnw_ref, kvnw_ref, wuqt_ref, wqit_ref, wk_ref, wvt_ref, lnw_ref, lnb_ref,
                  qt_ref, qit_ref, k_ref, vt_ref, kidx_ref, wt_ref):
    cqn = _rms(cq_ref[...], qnw_ref[...]).astype(BF16)
    qt = lax.dot_general(wuqt_ref[...], cqn, NT_DIMS, preferred_element_type=F32)
    qt_ref[...] = (qt * (B_HEAD_DIM ** -0.5 * LOG2E)).astype(BF16)
    qit = lax.dot_general(wqit_ref[...], cqn, NT_DIMS, preferred_element_type=F32)
    for h in range(IDX_HEADS):
        qit_ref[h] = qit[h * IDX_DIM:(h + 1) * IDX_DIM, :].astype(BF16)
    misc = misc_ref[...]
    ckvn = _rms(misc[:, :KV_RANK], kvnw_ref[...]).astype(BF16)
    k_ref[...] = jnp.dot(ckvn, wk_ref[...], preferred_element_type=F32).astype(BF16)
    vt = lax.dot_general(wvt_ref[...], ckvn, NT_DIMS, preferred_element_type=F32).astype(BF16)
    for t in range(vt_ref.shape[0]):
        vt_ref[t] = vt[:, t * KT:(t + 1) * KT]
    kr = misc[:, KV_RANK:KV_RANK + IDX_DIM]
    mu = jnp.mean(kr, axis=-1, keepdims=True)
    var = jnp.mean(jnp.square(kr - mu), axis=-1, keepdims=True)
    kidx_ref[...] = ((kr - mu) * lax.rsqrt(var + EPS) * lnw_ref[...] + lnb_ref[...]).astype(BF16)
    wt_ref[...] = (misc[:, KV_RANK + IDX_DIM:] * (IDX_HEADS ** -0.5 * IDX_DIM ** -0.5)).T


def _proj2(p, q_norm_w, kv_norm_w, w_uq_t, w_qidx_t, w_k, w_v_t, ln_w, ln_b):
    tm = 4 * KT
    const = lambda i: (0, 0)
    return pl.pallas_call(
        _proj2_kernel,
        grid=(SEQ // tm,),
        in_specs=[
            pl.BlockSpec((tm, 512), lambda i: (i, P_CQ // 512)),
            pl.BlockSpec((tm, 512), lambda i: (i, P_MISC // 512)),
            pl.BlockSpec((1, Q_RANK), const),
            pl.BlockSpec((1, KV_RANK), const),
            pl.BlockSpec((B_WIDTH, Q_RANK), const),
            pl.BlockSpec((IDX_HEADS * IDX_DIM, Q_RANK), const),
            pl.BlockSpec((KV_RANK, B_WIDTH), const),
            pl.BlockSpec((B_WIDTH, KV_RANK), const),
            pl.BlockSpec((1, IDX_DIM), const),
            pl.BlockSpec((1, IDX_DIM), const),
        ],
        out_specs=[
            pl.BlockSpec((B_WIDTH, tm), lambda i: (0, i)),
            pl.BlockSpec((IDX_HEADS, IDX_DIM, tm), lambda i: (0, 0, i)),
            pl.BlockSpec((tm, B_WIDTH), lambda i: (i, 0)),
            pl.BlockSpec((tm // KT, B_WIDTH, KT), lambda i: (i, 0, 0)),
            pl.BlockSpec((tm, IDX_DIM), lambda i: (i, 0)),
            pl.BlockSpec((128, tm), lambda i: (0, i)),
        ],
        out_shape=[
            jax.ShapeDtypeStruct((B_WIDTH, SEQ), BF16),
            jax.ShapeDtypeStruct((IDX_HEADS, IDX_DIM, SEQ), BF16),
            jax.ShapeDtypeStruct((SEQ, B_WIDTH), BF16),
            jax.ShapeDtypeStruct((NKT, B_WIDTH, KT), BF16),
            jax.ShapeDtypeStruct((SEQ, IDX_DIM), BF16),
            jax.ShapeDtypeStruct((128, SEQ), F32),
        ],
        compiler_params=pltpu.CompilerParams(
            dimension_semantics=("arbitrary",), vmem_limit_bytes=VMEM_LIMIT),
        name="proj2",
    )(p, p, q_norm_w, kv_norm_w, w_uq_t, w_qidx_t, w_k, w_v_t, ln_w, ln_b)


def _sigmoid(x):
    return 0.5 * jnp.tanh(0.5 * x) + 0.5


def _split3(x):
    x1 = x.astype(BF16)
    r1 = x - x1.astype(F32)
    x2 = r1.astype(BF16)
    x3 = (r1 - x2.astype(F32)).astype(BF16)
    return x1, x2, x3


def _hgrn2_kernel(aq_ref, af_ref, ai_ref, ag_ref, lbt_ref, gn_ref, o_ref,
                  st_ref, sel_ref, a_ref, a2_ref, b2_ref, kd_ref, v_ref, e_ref, oacc_ref):
    @pl.when(pl.program_id(0) == 0)
    def _():
        st_ref[...] = jnp.zeros_like(st_ref)
        r = lax.broadcasted_iota(jnp.int32, (TB, TB), 0)
        c = lax.broadcasted_iota(jnp.int32, (TB, TB), 1)
        same = (r // SUB) == (c // SUB)
        sel_ref[...] = jnp.where(same & (c <= r), 1.0, 0.0).astype(BF16)

    lbt = lbt_ref[...]
    mx = jnp.maximum(lbt[0:1], lbt[1:2])
    e0 = jnp.exp(lbt[0:1] - mx)
    e1 = jnp.exp(lbt[1:2] - mx)
    lb = e0 / (e0 + e1)

    f = lb + (1.0 - lb) * _sigmoid(af_ref[...])
    g = jnp.log(f)
    sel = sel_ref[...]
    g1, g2, g3 = _split3(g)
    cum = (jnp.dot(sel, g1, preferred_element_type=F32) + jnp.dot(sel, g2, preferred_element_type=F32)
           + jnp.dot(sel, g3, preferred_element_type=F32))
    cum3 = cum.reshape(TB // SUB, SUB, A_WIDTH)
    mid = jnp.broadcast_to(cum3[:, MID - 1:MID, :], cum3.shape).reshape(TB, A_WIDTH)
    tot = jnp.broadcast_to(cum3[:, SUB - 1:SUB, :], cum3.shape).reshape(TB, A_WIDTH)
    aq = aq_ref[...]
    q = aq * _sigmoid(aq) * (A_HEAD_DIM ** -0.5)
    kk = 1.0 - f
    a_ref[...] = (q * jnp.exp(cum)).astype(BF16)
    a2_ref[...] = (q * jnp.exp(cum - mid)).astype(BF16)
    b2_ref[...] = (kk * jnp.exp(mid - cum)).astype(BF16)
    kd_ref[...] = (kk * jnp.exp(tot - cum)).astype(BF16)
    v_ref[...] = ai_ref[...].astype(BF16)
    e_ref[...] = jnp.exp(tot)

    tr = lax.broadcasted_iota(jnp.int32, (SUB, SUB), 0)
    tc = lax.broadcasted_iota(jnp.int32, (SUB, SUB), 1)
    causal = tc <= tr

    def chunk_body(c, carry):
        r0 = pl.multiple_of(c * SUB, SUB)
        heads = [slice(h * A_HEAD_DIM, (h + 1) * A_HEAD_DIM) for h in range(A_HEADS)]
        ah = [a_ref[pl.ds(r0, SUB), cs] for cs in heads]
        vh = [v_ref[pl.ds(r0, SUB), cs] for cs in heads]
        sc = [lax.dot_general(a2_ref[pl.ds(r0, SUB), heads[h]], b2_ref[pl.ds(r0, SUB), heads[h]], NT_DIMS,
                              preferred_element_type=F32) for h in range(A_HEADS)]
        inter = []
        for h in range(A_HEADS):
            st = st_ref[h]
            inter.append(lax.dot_general(ah[h], st.astype(BF16), NT_DIMS, preferred_element_type=F32))
            dst = lax.dot_general(vh[h], kd_ref[pl.ds(r0, SUB), heads[h]], TN_DIMS, preferred_element_type=F32)
            st_ref[h] = st * e_ref[pl.ds(r0, 1), heads[h]] + dst
        for h in range(A_HEADS):
            pm = jnp.where(causal, sc[h], 0.0).astype(BF16)
            oacc_ref[pl.ds(r0, SUB), heads[h]] = inter[h] + jnp.dot(pm, vh[h], preferred_element_type=F32)
        return carry

    lax.fori_loop(0, TB // SUB, chunk_body, 0, unroll=True)

    gn = gn_ref[...]
    ag = ag_ref[...]
    gate = ag * _sigmoid(ag)
    for h in range(A_HEADS):
        cs = slice(h * A_HEAD_DIM, (h + 1) * A_HEAD_DIM)
        o = oacc_ref[:, cs]
        on = o * lax.rsqrt(jnp.mean(o * o, axis=-1, keepdims=True) + EPS) * gn
        o_ref[:, cs] = (on * gate[:, cs]).astype(BF16)


def _hgrn2(p, lb_table, gnorm):
    blk = lambda col: pl.BlockSpec((TB, A_WIDTH), lambda i: (i, col // A_WIDTH))
    return pl.pallas_call(
        _hgrn2_kernel,
        grid=(SEQ // TB,),
        in_specs=[blk(P_AQ), blk(P_AF), blk(P_AI), blk(P_AG),
                  pl.BlockSpec((2, A_WIDTH), lambda i: (0, 0)),
                  pl.BlockSpec((1, A_HEAD_DIM), lambda i: (0, 0))],
        out_specs=pl.BlockSpec((TB, A_WIDTH), lambda i: (i, 0)),
        out_shape=jax.ShapeDtypeStruct((SEQ, A_WIDTH), BF16),
        scratch_shapes=[
            pltpu.VMEM((A_HEADS, A_HEAD_DIM, A_HEAD_DIM), F32),
            pltpu.VMEM((TB, TB), BF16),
            pltpu.VMEM((TB, A_WIDTH), BF16),
            pltpu.VMEM((TB, A_WIDTH), BF16),
            pltpu.VMEM((TB, A_WIDTH), BF16),
            pltpu.VMEM((TB, A_WIDTH), BF16),
            pltpu.VMEM((TB, A_WIDTH), BF16),
            pltpu.VMEM((TB, A_WIDTH), F32),
            pltpu.VMEM((TB, A_WIDTH), F32),
        ],
        compiler_params=pltpu.CompilerParams(
            dimension_semantics=("arbitrary",), vmem_limit_bytes=VMEM_LIMIT),
        name="hgrn2",
    )(p, p, p, p, lb_table, gnorm)


def _t5_bucket(rel):
    n = jnp.abs(rel)
    large = jnp.full(rel.shape, T5_MAX_EXACT, jnp.int32)
    for t in T5_THR:
        large = large + jnp.where(n >= t, 1, 0)
    return jnp.where(rel > 0, T5_HALF, 0) + jnp.where(n < T5_MAX_EXACT, n, large)


def _dsa_kernel(relb_ref, qi_ref, w_ref, q_ref, bg_ref, kidx_ref, k_ref, v_ref, o_ref,
                sc_ref, hi_ref, b0_ref, b1_ref, thr_ref, m_ref, l_ref, acc_ref, s_ref, ms_ref, al_ref):
    qb = pl.program_id(0)

    @pl.when(qb == 0)
    def _():
        def bias_tile(rel, h):
            bucket = _t5_bucket(rel)
            val = jnp.zeros(rel.shape, F32)
            for b in range(REL_BUCKETS):
                val = jnp.where(bucket == b, relb_ref[b, h], val)
            return (val - relb_ref[T5_HALF - 1, h]) * LOG2E

        r0 = lax.broadcasted_iota(jnp.int32, (KT, QB), 0)
        c0 = lax.broadcasted_iota(jnp.int32, (KT, QB), 1)
        r1 = lax.broadcasted_iota(jnp.int32, (128, 128), 0)
        c1 = lax.broadcasted_iota(jnp.int32, (128, 128), 1)
        for h in range(B_HEADS):
            b0_ref[h] = bias_tile(r0 - c0, h)
            b1_ref[h] = bias_tile(r1 - 128 - c1, h)

    def idx_tile(j, diag):
        for ks in range(KT // KSUB):
            kt = kidx_ref[pl.ds(pl.multiple_of(j * KT + ks * KSUB, KSUB), KSUB), :]
            acc = jnp.zeros((KSUB, QB), F32)
            for h in range(IDX_HEADS):
                r = jnp.dot(kt, qi_ref[h], preferred_element_type=F32)
                acc = acc + w_ref[h:h + 1, :] * jnp.maximum(r, 0.0)
            if diag:
                rk = ks * KSUB + lax.broadcasted_iota(jnp.int32, (KSUB, QB), 0)
                ct = lax.broadcasted_iota(jnp.int32, (KSUB, QB), 1)
                acc = jnp.where((rk // CHUNK) <= (ct // CHUNK), acc, NEG_INF)
            sc_ref[j, ks * KSUB:(ks + 1) * KSUB, :] = acc
            hi_ref[j, ks * KSUB:(ks + 1) * KSUB, :] = acc.astype(BF16)

    def idx_body(i, carry):
        for u in range(IDX_UNROLL):
            idx_tile(IDX_UNROLL * i + u, False)
        return carry

    lax.fori_loop(0, qb // IDX_UNROLL, idx_body, 0)
    for rem in range(IDX_UNROLL):
        @pl.when(qb % IDX_UNROLL == rem)
        def _():
            for u in range(rem):
                idx_tile(qb - rem + u, False)
            idx_tile(qb, True)

    npairs = (qb + 2) // 2

    @pl.when(qb % 2 == 0)
    def _():
        sc_ref[qb + 1] = jnp.full((KT, QB), NEG_INF, F32)
        hi_ref[qb + 1] = jnp.full((KT, QB), NEG_INF, BF16)

    def count(ref, thr):
        one = jnp.ones((KT, QB), ref.dtype)
        zero = jnp.zeros((KT, QB), ref.dtype)
        rows = CNT_ROWS if ref.dtype == BF16 else CNT_ROWS // 2

        def tiles(first, n, cnt):
            for u in range(n):
                sel = jnp.where(ref[first + u] >= thr, one, zero)
                for r in range(KT // rows):
                    cnt = cnt + sel[r * rows:(r + 1) * rows]
            return cnt

        cnt = lax.fori_loop(0, npairs // 2, lambda i, c: tiles(4 * i, 4, c), jnp.zeros((rows, QB), ref.dtype))
        cnt = lax.fori_loop(0, npairs % 2, lambda i, c: tiles(2 * (npairs - 1), 2, c), cnt)
        return jnp.sum(cnt.astype(F32), axis=0, keepdims=True)

    def key16_to_bf16(k):
        bits = jnp.where(k >= 0, k, k ^ 0x7FFF).astype(jnp.int16)
        return lax.bitcast_convert_type(bits, BF16)

    def key32_to_f32(k):
        return lax.bitcast_convert_type(jnp.where(k >= 0, k, k ^ 0x7FFFFFFF), F32)

    c0 = count(hi_ref, jnp.zeros((1, QB), BF16))
    k16 = jnp.where(c0 >= float(TOPK), 0, I16_MIN).astype(jnp.int32)

    def coarse_body(i, k):
        cand = k | jnp.left_shift(jnp.int32(1), 14 - i)
        return jnp.where(count(hi_ref, key16_to_bf16(cand)) >= float(TOPK), cand, k)

    k16 = jnp.maximum(lax.fori_loop(0, 15, coarse_body, k16), KEY16_NEG_INF)
    k_mid = jnp.where(k16 >= 0, k16 << 16, (k16 << 16) | 0xFFFF)
    k_base = jnp.maximum(k_mid, I32_MIN + FINE_BELOW) - FINE_BELOW

    def fine_body(i, carry):
        off, n_ge = carry
        cand = off | jnp.left_shift(jnp.int32(1), FINE_BITS - 1 - i)
        c = count(sc_ref, key32_to_f32(k_base + cand))
        ok = c >= float(TOPK)
        return jnp.where(ok, cand, off), jnp.where(ok, c, n_ge)

    off, n_ge = lax.fori_loop(0, FINE_BITS, fine_body,
                              (jnp.zeros((1, QB), jnp.int32), jnp.full((1, QB), TOPK + 1.0, F32)))
    thr = key32_to_f32(k_base + off)
    thr_ref[...] = thr

    @pl.when(jnp.max(n_ge) > float(TOPK))
    def _():
        rows = lax.broadcasted_iota(jnp.int32, (KT, QB), 0)

        def tile_sum(fn):
            def body(jp, cnt):
                for u in range(2):
                    sel = fn(2 * jp + u)
                    for r in range(KT // CNT_ROWS):
                        cnt = cnt + sel[r * CNT_ROWS:(r + 1) * CNT_ROWS]
                return cnt
            cnt = lax.fori_loop(0, npairs, body, jnp.zeros((CNT_ROWS, QB), F32))
            return jnp.sum(cnt, axis=0, keepdims=True)

        n_gt = tile_sum(lambda j: jnp.where(sc_ref[j] > thr, 1.0, 0.0))
        ties_kept = float(TOPK) - n_gt

        def ties_before(p):
            return tile_sum(lambda j: jnp.where((sc_ref[j] == thr) & (rows < p - j * KT), 1.0, 0.0))

        def idx_body(i, p):
            cand = p | jnp.left_shift(jnp.int32(1), SEQ_BITS - 1 - i)
            return jnp.where(ties_before(cand) < ties_kept, cand, p)

        last = lax.fori_loop(0, SEQ_BITS, idx_body, jnp.zeros((1, QB), jnp.int32))

        def drop_body(j, carry):
            x = sc_ref[j]
            sc_ref[j] = jnp.where((x == thr) & (rows > last - j * KT), NEG_INF, x)
            return carry

        lax.fori_loop(0, 2 * npairs, drop_body, 0)

    m_ref[...] = jnp.full(m_ref.shape, NEG_INF, F32)
    l_ref[...] = jnp.zeros_like(l_ref)
    acc_ref[...] = jnp.zeros_like(acc_ref)
    ones_rows = jnp.ones((SUM_ROWS, KT), BF16)

    def att_step(logits=None, values=None):
        if logits is not None:
            j1, kind, slot1 = logits
            s_t = sc_ref[j1]
            mask = jnp.where((s_t >= thr_ref[...]) & (s_t > NEG_INF), 0.0, NEG_INF)
            k0 = pl.multiple_of(j1 * KT, KT)
        if values is not None:
            j2, slot2 = values
        pvs = []

        def update(h):
            a = al_ref[slot2, h]
            l_ref[h] = a * l_ref[h] + pvs[h][B_HEAD_DIM:B_HEAD_DIM + 1, :]
            acc_ref[h] = a * acc_ref[h] + pvs[h][:B_HEAD_DIM, :]

        for h in range(B_HEADS):
            hs = slice(h * B_HEAD_DIM, (h + 1) * B_HEAD_DIM)
            if logits is not None:
                s = jnp.dot(k_ref[pl.ds(k0, KT), hs], q_ref[hs, :], preferred_element_type=F32)
                if kind == "diag":
                    s = s + b0_ref[h]
                elif kind == "prev":
                    lo = jnp.concatenate([s[128:, :128] + b1_ref[h], s[128:, 128:]], axis=1)
                    s = jnp.concatenate([s[:128, :], lo], axis=0)
                s = s + mask
                s_ref[slot1, h] = s
                m_prev = m_ref[h]
                m_next = jnp.maximum(m_prev, jnp.max(s, axis=0, keepdims=True))
                m_safe = jnp.where(m_next == NEG_INF, 0.0, m_next)
                ms_ref[slot1, h] = m_safe
                al_ref[slot1, h] = jnp.exp2(m_prev - m_safe)
                m_ref[h] = m_next
            if values is not None:
                p = jnp.exp2(s_ref[slot2, h] - ms_ref[slot2, h]).astype(BF16)
                pvs.append(jnp.dot(jnp.concatenate([v_ref[j2, hs, :], ones_rows], axis=0), p,
                                   preferred_element_type=F32))
                if h >= UPDATE_LAG:
                    update(h - UPDATE_LAG)
        if values is not None:
            for h in range(B_HEADS - UPDATE_LAG, B_HEADS):
                update(h)

    odd = qb % 2

    @pl.when(qb == 0)
    def _():
        att_step(logits=(0, "diag", 0))

    @pl.when(qb == 1)
    def _():
        att_step(logits=(0, "prev", 1))

    @pl.when(qb >= 2)
    def _():
        @pl.when(odd == 0)
        def _():
            att_step(logits=(0, "far", 0))

        @pl.when(odd == 1)
        def _():
            att_step(logits=(0, "far", 1))
            att_step(logits=(1, "far", 0), values=(0, 1))

        def att_pairs(j, n):
            for u in range(n):
                att_step(logits=(j + 2 * u + 1, "far", 1), values=(j + 2 * u, 0))
                att_step(logits=(j + 2 * u + 2, "far", 0), values=(j + 2 * u + 1, 1))

        def att_body(i, carry):
            att_pairs(odd + 2 * ATT_PAIRS * i, ATT_PAIRS)
            return carry

        def att_rest(i, carry):
            att_pairs(odd + 2 * (npair_far - npair_far % ATT_PAIRS + i), 1)
            return carry

        npair_far = (qb - 2) // 2
        lax.fori_loop(0, npair_far // ATT_PAIRS, att_body, 0)
        lax.fori_loop(0, npair_far % ATT_PAIRS, att_rest, 0)

        att_step(logits=(qb - 1, "prev", 1), values=(qb - 2, 0))

    @pl.when(qb >= 1)
    def _():
        att_step(logits=(qb, "diag", 0), values=(qb - 1, 1))

    att_step(values=(qb, 0))

    for h in range(B_HEADS):
        hs = slice(h * B_HEAD_DIM, (h + 1) * B_HEAD_DIM)
        bg = bg_ref[:, hs]
        o_ref[:, hs] = ((acc_ref[h] / l_ref[h]).T * (bg * jax.nn.sigmoid(bg))).astype(BF16)


def _dsa(rel_bias, q_idx_t, w_idx_t, q_t, p, k_idx, k, v_t):
    single = pl.Buffered(1)
    return pl.pallas_call(
        _dsa_kernel,
        grid=(SEQ // QB,),
        in_specs=[
            pl.BlockSpec(memory_space=pltpu.SMEM),
            pl.BlockSpec((IDX_HEADS, IDX_DIM, QB), lambda i: (0, 0, i)),
            pl.BlockSpec((128, QB), lambda i: (0, i)),
            pl.BlockSpec((B_WIDTH, QB), lambda i: (0, i)),
            pl.BlockSpec((QB, B_WIDTH), lambda i: (i, P_BG // B_WIDTH)),
            pl.BlockSpec((SEQ, IDX_DIM), lambda i: (0, 0), pipeline_mode=single),
            pl.BlockSpec((SEQ, B_WIDTH), lambda i: (0, 0), pipeline_mode=single),
            pl.BlockSpec((NKT, B_WIDTH, KT), lambda i: (0, 0, 0), pipeline_mode=single),
        ],
        out_specs=pl.BlockSpec((QB, B_WIDTH), lambda i: (i, 0)),
        out_shape=jax.ShapeDtypeStruct((SEQ, B_WIDTH), BF16),
        scratch_shapes=[
            pltpu.VMEM((NKT, KT, QB), F32),
            pltpu.VMEM((NKT, KT, QB), BF16),
            pltpu.VMEM((B_HEADS, KT, QB), F32),
            pltpu.VMEM((B_HEADS, 128, 128), F32),
            pltpu.VMEM((1, QB), F32),
            pltpu.VMEM((B_HEADS, 1, QB), F32),
            pltpu.VMEM((B_HEADS, 1, QB), F32),
            pltpu.VMEM((B_HEADS, B_HEAD_DIM, QB), F32),
            pltpu.VMEM((2, B_HEADS, KT, QB), F32),
            pltpu.VMEM((2, B_HEADS, 1, QB), F32),
            pltpu.VMEM((2, B_HEADS, 1, QB), F32),
        ],
        compiler_params=pltpu.CompilerParams(
            dimension_semantics=("arbitrary",), vmem_limit_bytes=VMEM_LIMIT),
        name="dsa",
    )(rel_bias, q_idx_t, w_idx_t, q_t, p, k_idx, k, v_t)


def _out_proj_kernel(ya_ref, yb_ref, ma_ref, mb_ref, x_ref, wpa_ref, wpb_ref, wo_ref, fnw_ref, o_ref):
    pa = jnp.dot(ya_ref[...], wpa_ref[...], preferred_element_type=F32)
    pb = jnp.dot(yb_ref[...], wpb_ref[...], preferred_element_type=F32)
    merged = jax.nn.sigmoid(ma_ref[...]) * pa + jax.nn.sigmoid(mb_ref[...]) * pb
    y = x_ref[...] + jnp.dot(merged.astype(BF16), wo_ref[...], preferred_element_type=F32)
    o_ref[...] = _rms(y, fnw_ref[...])


def _out_proj(y_a, y_b, p, x2, w_pa, w_pb, w_out, final_norm_w):
    tm = 512
    const = lambda i: (0, 0)
    single = pl.Buffered(1)
    return pl.pallas_call(
        _out_proj_kernel,
        grid=(SEQ // tm,),
        in_specs=[
            pl.BlockSpec((tm, A_WIDTH), lambda i: (i, 0)),
            pl.BlockSpec((tm, B_WIDTH), lambda i: (i, 0)),
            pl.BlockSpec((tm, D_MODEL), lambda i: (i, P_MA // D_MODEL)),
            pl.BlockSpec((tm, D_MODEL), lambda i: (i, P_MB // D_MODEL)),
            pl.BlockSpec((tm, D_MODEL), lambda i: (i, 0)),
            pl.BlockSpec((A_WIDTH, D_MODEL), const, pipeline_mode=single),
            pl.BlockSpec((B_WIDTH, D_MODEL), const, pipeline_mode=single),
            pl.BlockSpec((D_MODEL, D_MODEL), const, pipeline_mode=single),
            pl.BlockSpec((1, D_MODEL), const),
        ],
        out_specs=pl.BlockSpec((tm, D_MODEL), lambda i: (i, 0)),
        out_shape=jax.ShapeDtypeStruct((SEQ, D_MODEL), F32),
        compiler_params=pltpu.CompilerParams(
            dimension_semantics=("arbitrary",), vmem_limit_bytes=VMEM_LIMIT),
        name="out_proj",
    )(y_a, y_b, p, p, x2, w_pa, w_pb, w_out, final_norm_w)


def kernel(x, norm_w, w_in, lb_table, gnorm_a, q_norm_w, kv_norm_w, w_uq, w_qidx, w_ukv, kidx_norm_w,
           kidx_norm_b, w_pa, w_pb, w_out, rel_bias, final_norm_w):
    assert x.shape == (1, SEQ, D_MODEL) and w_in.shape[0] == 1
    x2 = x.reshape(SEQ, D_MODEL)
    p = _in_proj(x2, norm_w, w_in[0].astype(BF16))
    w_kv = w_ukv[0].astype(BF16).reshape(KV_RANK, B_HEADS, 2, B_HEAD_DIM)
    w_k = w_kv[:, :, 0, :].reshape(KV_RANK, B_WIDTH)
    w_v_t = w_kv[:, :, 1, :].reshape(KV_RANK, B_WIDTH).T
    q_t, q_idx_t, k, v_t, k_idx, w_idx_t = _proj2(
        p, q_norm_w, kv_norm_w, w_uq[0].astype(BF16).T, w_qidx[0].astype(BF16).T, w_k, w_v_t,
        kidx_norm_w, kidx_norm_b)
    y_a = _hgrn2(p, lb_table, gnorm_a)
    y_b = _dsa(rel_bias, q_idx_t, w_idx_t, q_t, p, k_idx, k, v_t)
    out = _out_proj(y_a, y_b, p, x2, w_pa[0].astype(BF16), w_pb[0].astype(BF16), w_out[0].astype(BF16),
                    final_norm_w.reshape(1, D_MODEL))
    return out.reshape(1, SEQ, D_MODEL)
```

```python
import math

import jax
import jax.numpy as jnp
import numpy as np
from jax import lax
from jax.experimental import pallas as pl
from jax.experimental.pallas import tpu as pltpu

F32 = jnp.float32
BF16 = jnp.bfloat16

D_MODEL = 2048
SEQ = 8192
EPS = 1e-6
A_HEADS = 8
A_HEAD_DIM = 128
A_WIDTH = 1024
B_HEADS = 8
B_HEAD_DIM = 128
B_WIDTH = 1024
Q_RANK = 512
KV_RANK = 256
IDX_HEADS = 16
IDX_DIM = 128
TOPK = 256
CHUNK = 64
REL_BUCKETS = 32
REL_MAX_DIST = 128

P_AQ, P_AF, P_AI, P_AG = 0, 1024, 2048, 3072
P_CQ = 4096
P_MISC = 4608
P_BG = 5120
P_MA = 6144
P_MB = 8192
P_WIDTH = 10240
IN_WIDTH = 10128
IN_TN = 1024
IN_SHIFT = 112

VMEM_LIMIT = 63 * 1024 * 1024

QB = 256
KT = 256
NKT = SEQ // KT
KSUB = 128
IDX_UNROLL = 8
CNT_ROWS = 32
assert SEQ // CNT_ROWS <= 256
SUM_ROWS = 16
UPDATE_LAG = 2
ATT_PAIRS = 4

SUB = 32
MID = SUB // 2
TB = 256

NEG_INF = float("-inf")
LOG2E = math.log2(math.e)
I16_MIN = -32768
I32_MIN = -2147483648
KEY16_NEG_INF = -32641
SEQ_BITS = 13
assert SEQ == 1 << SEQ_BITS
FINE_BELOW = 1 << 16
FINE_BITS = 17

NT_DIMS = (((1,), (1,)), ((), ()))
TN_DIMS = (((0,), (0,)), ((), ()))


def _t5_thresholds():
    half = REL_BUCKETS // 2
    max_exact = half // 2
    n = np.arange(1, 4 * REL_MAX_DIST, dtype=np.int64)
    large = max_exact + (np.log(np.maximum(n, 1).astype(np.float64) / max_exact)
                         / math.log(REL_MAX_DIST / max_exact) * (half - max_exact)).astype(np.int32)
    large = np.minimum(large, half - 1)
    b = np.where(n < max_exact, n, large)
    thr = [int(n[i + 1]) for i in np.nonzero(np.diff(b))[0] if n[i + 1] > max_exact]
    assert len(thr) == half - 1 - max_exact and b[-1] == half - 1 and thr[-1] < REL_MAX_DIST
    return max_exact, half, thr


T5_MAX_EXACT, T5_HALF, T5_THR = _t5_thresholds()


def _in_proj_kernel(x_ref, nw_ref, w_ref, o_ref, h_ref):
    @pl.when(pl.program_id(1) == 0)
    def _():
        x = x_ref[...]
        ms = jnp.mean(x * x, axis=-1, keepdims=True)
        h_ref[...] = (x * lax.rsqrt(ms + EPS) * nw_ref[...]).astype(BF16)

    j = pl.program_id(1)

    @pl.when(j < P_BG // IN_TN)
    def _():
        o_ref[...] = jnp.dot(h_ref[...], w_ref[:, :IN_TN], preferred_element_type=F32)

    @pl.when(j >= P_BG // IN_TN)
    def _():
        w = pltpu.roll(w_ref[...], IN_TN + 128 - (128 - IN_SHIFT), axis=1)[:, :IN_TN]
        o_ref[...] = jnp.dot(h_ref[...], w, preferred_element_type=F32)


def _in_proj(x2, norm_w, w_in_b):
    tm, tn = 1024, IN_TN
    n_plain = P_BG // tn

    def w_index(i, j):
        return 0, pl.multiple_of(jnp.where(j < n_plain, j * tn, j * tn - 128), 128)

    return pl.pallas_call(
        _in_proj_kernel,
        grid=(SEQ // tm, P_WIDTH // tn),
        in_specs=[
            pl.BlockSpec((tm, D_MODEL), lambda i, j: (i, 0)),
            pl.BlockSpec((1, D_MODEL), lambda i, j: (0, 0)),
            pl.BlockSpec((pl.Element(D_MODEL), pl.Element(tn + 128, padding=(0, P_WIDTH - IN_WIDTH))), w_index),
        ],
        out_specs=pl.BlockSpec((tm, tn), lambda i, j: (i, j)),
        out_shape=jax.ShapeDtypeStruct((SEQ, P_WIDTH), F32),
        scratch_shapes=[pltpu.VMEM((tm, D_MODEL), BF16)],
        compiler_params=pltpu.CompilerParams(
            dimension_semantics=("arbitrary", "arbitrary"), vmem_limit_bytes=VMEM_LIMIT),
        name="in_proj",
    )(x2, norm_w, w_in_b)


def _rms(x, w):
    return x * lax.rsqrt(jnp.mean(x * x, axis=-1, keepdims=True) + EPS) * w


def _proj2_kernel(cq_ref, misc_ref, qnw_ref, kvnw_ref, wuqt_ref, wqit_ref, wk_ref, wvt_ref, lnw_ref, lnb_ref,
                  qt_ref, qit_ref, k_ref, vt_ref, kidx_ref, wt_ref):
    cqn = _rms(cq_ref[...], qnw_ref[...]).astype(BF16)
    qt = lax.dot_general(wuqt_ref[...], cqn, NT_DIMS, preferred_element_type=F32)
    qt_ref[...] = (qt * (B_HEAD_DIM ** -0.5 * LOG2E)).astype(BF16)
    qit = lax.dot_general(wqit_ref[...], cqn, NT_DIMS, preferred_element_type=F32)
    for h in range(IDX_HEADS):
        qit_ref[h] = qit[h * IDX_DIM:(h + 1) * IDX_DIM, :].astype(BF16)
    misc = misc_ref[...]
    ckvn = _rms(misc[:, :KV_RANK], kvnw_ref[...]).astype(BF16)
    k_ref[...] = jnp.dot(ckvn, wk_ref[...], preferred_element_type=F32).astype(BF16)
    vt = lax.dot_general(wvt_ref[...], ckvn, NT_DIMS, preferred_element_type=F32).astype(BF16)
    for t in range(vt_ref.shape[0]):
        vt_ref[t] = vt[:, t * KT:(t + 1) * KT]
    kr = misc[:, KV_RANK:KV_RANK + IDX_DIM]
    mu = jnp.mean(kr, axis=-1, keepdims=True)
    var = jnp.mean(jnp.square(kr - mu), axis=-1, keepdims=True)
    kidx_ref[...] = ((kr - mu) * lax.rsqrt(var + EPS) * lnw_ref[...] + lnb_ref[...]).astype(BF16)
    wt_ref[...] = (misc[:, KV_RANK + IDX_DIM:] * (IDX_HEADS ** -0.5 * IDX_DIM ** -0.5)).T


def _proj2(p, q_norm_w, kv_norm_w, w_uq_t, w_qidx_t, w_k, w_v_t, ln_w, ln_b):
    tm = 4 * KT
    const = lambda i: (0, 0)
    return pl.pallas_call(
        _proj2_kernel,
        grid=(SEQ // tm,),
        in_specs=[
            pl.BlockSpec((tm, 512), lambda i: (i, P_CQ // 512)),
            pl.BlockSpec((tm, 512), lambda i: (i, P_MISC // 512)),
            pl.BlockSpec((1, Q_RANK), const),
            pl.BlockSpec((1, KV_RANK), const),
            pl.BlockSpec((B_WIDTH, Q_RANK), const),
            pl.BlockSpec((IDX_HEADS * IDX_DIM, Q_RANK), const),
            pl.BlockSpec((KV_RANK, B_WIDTH), const),
            pl.BlockSpec((B_WIDTH, KV_RANK), const),
            pl.BlockSpec((1, IDX_DIM), const),
            pl.BlockSpec((1, IDX_DIM), const),
        ],
        out_specs=[
            pl.BlockSpec((B_WIDTH, tm), lambda i: (0, i)),
            pl.BlockSpec((IDX_HEADS, IDX_DIM, tm), lambda i: (0, 0, i)),
            pl.BlockSpec((tm, B_WIDTH), lambda i: (i, 0)),
            pl.BlockSpec((tm // KT, B_WIDTH, KT), lambda i: (i, 0, 0)),
            pl.BlockSpec((tm, IDX_DIM), lambda i: (i, 0)),
            pl.BlockSpec((128, tm), lambda i: (0, i)),
        ],
        out_shape=[
            jax.ShapeDtypeStruct((B_WIDTH, SEQ), BF16),
            jax.ShapeDtypeStruct((IDX_HEADS, IDX_DIM, SEQ), BF16),
            jax.ShapeDtypeStruct((SEQ, B_WIDTH), BF16),
            jax.ShapeDtypeStruct((NKT, B_WIDTH, KT), BF16),
            jax.ShapeDtypeStruct((SEQ, IDX_DIM), BF16),
            jax.ShapeDtypeStruct((128, SEQ), F32),
        ],
        compiler_params=pltpu.CompilerParams(
            dimension_semantics=("arbitrary",), vmem_limit_bytes=VMEM_LIMIT),
        name="proj2",
    )(p, p, q_norm_w, kv_norm_w, w_uq_t, w_qidx_t, w_k, w_v_t, ln_w, ln_b)


def _sigmoid(x):
    return 0.5 * jnp.tanh(0.5 * x) + 0.5


def _split3(x):
    x1 = x.astype(BF16)
    r1 = x - x1.astype(F32)
    x2 = r1.astype(BF16)
    x3 = (r1 - x2.astype(F32)).astype(BF16)
    return x1, x2, x3


def _hgrn2_kernel(aq_ref, af_ref, ai_ref, ag_ref, lbt_ref, gn_ref, o_ref,
                  st_ref, sel_ref, a_ref, a2_ref, b2_ref, kd_ref, v_ref, e_ref, oacc_ref):
    @pl.when(pl.program_id(0) == 0)
    def _():
        st_ref[...] = jnp.zeros_like(st_ref)
        r = lax.broadcasted_iota(jnp.int32, (TB, TB), 0)
        c = lax.broadcasted_iota(jnp.int32, (TB, TB), 1)
        same = (r // SUB) == (c // SUB)
        sel_ref[...] = jnp.where(same & (c <= r), 1.0, 0.0).astype(BF16)

    lbt = lbt_ref[...]
    mx = jnp.maximum(lbt[0:1], lbt[1:2])
    e0 = jnp.exp(lbt[0:1] - mx)
    e1 = jnp.exp(lbt[1:2] - mx)
    lb = e0 / (e0 + e1)

    f = lb + (1.0 - lb) * _sigmoid(af_ref[...])
    g = jnp.log(f)
    sel = sel_ref[...]
    g1, g2, g3 = _split3(g)
    cum = (jnp.dot(sel, g1, preferred_element_type=F32) + jnp.dot(sel, g2, preferred_element_type=F32)
           + jnp.dot(sel, g3, preferred_element_type=F32))
    cum3 = cum.reshape(TB // SUB, SUB, A_WIDTH)
    mid = jnp.broadcast_to(cum3[:, MID - 1:MID, :], cum3.shape).reshape(TB, A_WIDTH)
    tot = jnp.broadcast_to(cum3[:, SUB - 1:SUB, :], cum3.shape).reshape(TB, A_WIDTH)
    aq = aq_ref[...]
    q = aq * _sigmoid(aq) * (A_HEAD_DIM ** -0.5)
    kk = 1.0 - f
    a_ref[...] = (q * jnp.exp(cum)).astype(BF16)
    a2_ref[...] = (q * jnp.exp(cum - mid)).astype(BF16)
    b2_ref[...] = (kk * jnp.exp(mid - cum)).astype(BF16)
    kd_ref[...] = (kk * jnp.exp(tot - cum)).astype(BF16)
    v_ref[...] = ai_ref[...].astype(BF16)
    e_ref[...] = jnp.exp(tot)

    tr = lax.broadcasted_iota(jnp.int32, (SUB, SUB), 0)
    tc = lax.broadcasted_iota(jnp.int32, (SUB, SUB), 1)
    causal = tc <= tr

    def chunk_body(c, carry):
        r0 = pl.multiple_of(c * SUB, SUB)
        heads = [slice(h * A_HEAD_DIM, (h + 1) * A_HEAD_DIM) for h in range(A_HEADS)]
        ah = [a_ref[pl.ds(r0, SUB), cs] for cs in heads]
        vh = [v_ref[pl.ds(r0, SUB), cs] for cs in heads]
        sc = [lax.dot_general(a2_ref[pl.ds(r0, SUB), heads[h]], b2_ref[pl.ds(r0, SUB), heads[h]], NT_DIMS,
                              preferred_element_type=F32) for h in range(A_HEADS)]
        inter = []
        for h in range(A_HEADS):
            st = st_ref[h]
            inter.append(lax.dot_general(ah[h], st.astype(BF16), NT_DIMS, preferred_element_type=F32))
            dst = lax.dot_general(vh[h], kd_ref[pl.ds(r0, SUB), heads[h]], TN_DIMS, preferred_element_type=F32)
            st_ref[h] = st * e_ref[pl.ds(r0, 1), heads[h]] + dst
        for h in range(A_HEADS):
            pm = jnp.where(causal, sc[h], 0.0).astype(BF16)
            oacc_ref[pl.ds(r0, SUB), heads[h]] = inter[h] + jnp.dot(pm, vh[h], preferred_element_type=F32)
        return carry

    lax.fori_loop(0, TB // SUB, chunk_body, 0, unroll=True)

    gn = gn_ref[...]
    ag = ag_ref[...]
    gate = ag * _sigmoid(ag)
    for h in range(A_HEADS):
        cs = slice(h * A_HEAD_DIM, (h + 1) * A_HEAD_DIM)
        o = oacc_ref[:, cs]
        on = o * lax.rsqrt(jnp.mean(o * o, axis=-1, keepdims=True) + EPS) * gn
        o_ref[:, cs] = (on * gate[:, cs]).astype(BF16)


def _hgrn2(p, lb_table, gnorm):
    blk = lambda col: pl.BlockSpec((TB, A_WIDTH), lambda i: (i, col // A_WIDTH))
    return pl.pallas_call(
        _hgrn2_kernel,
        grid=(SEQ // TB,),
        in_specs=[blk(P_AQ), blk(P_AF), blk(P_AI), blk(P_AG),
                  pl.BlockSpec((2, A_WIDTH), lambda i: (0, 0)),
                  pl.BlockSpec((1, A_HEAD_DIM), lambda i: (0, 0))],
        out_specs=pl.BlockSpec((TB, A_WIDTH), lambda i: (i, 0)),
        out_shape=jax.ShapeDtypeStruct((SEQ, A_WIDTH), BF16),
        scratch_shapes=[
            pltpu.VMEM((A_HEADS, A_HEAD_DIM, A_HEAD_DIM), F32),
            pltpu.VMEM((TB, TB), BF16),
            pltpu.VMEM((TB, A_WIDTH), BF16),
            pltpu.VMEM((TB, A_WIDTH), BF16),
            pltpu.VMEM((TB, A_WIDTH), BF16),
            pltpu.VMEM((TB, A_WIDTH), BF16),
            pltpu.VMEM((TB, A_WIDTH), BF16),
            pltpu.VMEM((TB, A_WIDTH), F32),
            pltpu.VMEM((TB, A_WIDTH), F32),
        ],
        compiler_params=pltpu.CompilerParams(
            dimension_semantics=("arbitrary",), vmem_limit_bytes=VMEM_LIMIT),
        name="hgrn2",
    )(p, p, p, p, lb_table, gnorm)


def _t5_bucket(rel):
    n = jnp.abs(rel)
    large = jnp.full(rel.shape, T5_MAX_EXACT, jnp.int32)
    for t in T5_THR:
        large = large + jnp.where(n >= t, 1, 0)
    return jnp.where(rel > 0, T5_HALF, 0) + jnp.where(n < T5_MAX_EXACT, n, large)


def _dsa_kernel(relb_ref, qi_ref, w_ref, q_ref, bg_ref, kidx_ref, k_ref, v_ref, o_ref,
                sc_ref, hi_ref, b0_ref, b1_ref, thr_ref, m_ref, l_ref, acc_ref, s_ref, ms_ref, al_ref):
    qb = pl.program_id(0)

    @pl.when(qb == 0)
    def _():
        def bias_tile(rel, h):
            bucket = _t5_bucket(rel)
            val = jnp.zeros(rel.shape, F32)
            for b in range(REL_BUCKETS):
                val = jnp.where(bucket == b, relb_ref[b, h], val)
            return (val - relb_ref[T5_HALF - 1, h]) * LOG2E

        r0 = lax.broadcasted_iota(jnp.int32, (KT, QB), 0)
        c0 = lax.broadcasted_iota(jnp.int32, (KT, QB), 1)
        r1 = lax.broadcasted_iota(jnp.int32, (128, 128), 0)
        c1 = lax.broadcasted_iota(jnp.int32, (128, 128), 1)
        for h in range(B_HEADS):
            b0_ref[h] = bias_tile(r0 - c0, h)
            b1_ref[h] = bias_tile(r1 - 128 - c1, h)

    def idx_tile(j, diag):
        for ks in range(KT // KSUB):
            kt = kidx_ref[pl.ds(pl.multiple_of(j * KT + ks * KSUB, KSUB), KSUB), :]
            acc = jnp.zeros((KSUB, QB), F32)
            for h in range(IDX_HEADS):
                r = jnp.dot(kt, qi_ref[h], preferred_element_type=F32)
                acc = acc + w_ref[h:h + 1, :] * jnp.maximum(r, 0.0)
            if diag:
                rk = ks * KSUB + lax.broadcasted_iota(jnp.int32, (KSUB, QB), 0)
                ct = lax.broadcasted_iota(jnp.int32, (KSUB, QB), 1)
                acc = jnp.where((rk // CHUNK) <= (ct // CHUNK), acc, NEG_INF)
            sc_ref[j, ks * KSUB:(ks + 1) * KSUB, :] = acc
            hi_ref[j, ks * KSUB:(ks + 1) * KSUB, :] = acc.astype(BF16)

    def idx_body(i, carry):
        for u in range(IDX_UNROLL):
            idx_tile(IDX_UNROLL * i + u, False)
        return carry

    lax.fori_loop(0, qb // IDX_UNROLL, idx_body, 0)
    for rem in range(IDX_UNROLL):
        @pl.when(qb % IDX_UNROLL == rem)
        def _():
            for u in range(rem):
                idx_tile(qb - rem + u, False)
            idx_tile(qb, True)

    npairs = (qb + 2) // 2

    @pl.when(qb % 2 == 0)
    def _():
        sc_ref[qb + 1] = jnp.full((KT, QB), NEG_INF, F32)
        hi_ref[qb + 1] = jnp.full((KT, QB), NEG_INF, BF16)

    def count(ref, thr):
        one = jnp.ones((KT, QB), ref.dtype)
        zero = jnp.zeros((KT, QB), ref.dtype)
        rows = CNT_ROWS if ref.dtype == BF16 else CNT_ROWS // 2

        def tiles(first, n, cnt):
            for u in range(n):
                sel = jnp.where(ref[first + u] >= thr, one, zero)
                for r in range(KT // rows):
                    cnt = cnt + sel[r * rows:(r + 1) * rows]
            return cnt

        cnt = lax.fori_loop(0, npairs // 2, lambda i, c: tiles(4 * i, 4, c), jnp.zeros((rows, QB), ref.dtype))
        cnt = lax.fori_loop(0, npairs % 2, lambda i, c: tiles(2 * (npairs - 1), 2, c), cnt)
        return jnp.sum(cnt.astype(F32), axis=0, keepdims=True)

    def key16_to_bf16(k):
        bits = jnp.where(k >= 0, k, k ^ 0x7FFF).astype(jnp.int16)
        return lax.bitcast_convert_type(bits, BF16)

    def key32_to_f32(k):
        return lax.bitcast_convert_type(jnp.where(k >= 0, k, k ^ 0x7FFFFFFF), F32)

    c0 = count(hi_ref, jnp.zeros((1, QB), BF16))
    k16 = jnp.where(c0 >= float(TOPK), 0, I16_MIN).astype(jnp.int32)

    def coarse_body(i, k):
        cand = k | jnp.left_shift(jnp.int32(1), 14 - i)
        return jnp.where(count(hi_ref, key16_to_bf16(cand)) >= float(TOPK), cand, k)

    k16 = jnp.maximum(lax.fori_loop(0, 15, coarse_body, k16), KEY16_NEG_INF)
    k_mid = jnp.where(k16 >= 0, k16 << 16, (k16 << 16) | 0xFFFF)
    k_base = jnp.maximum(k_mid, I32_MIN + FINE_BELOW) - FINE_BELOW

    def fine_body(i, carry):
        off, n_ge = carry
        cand = off | jnp.left_shift(jnp.int32(1), FINE_BITS - 1 - i)
        c = count(sc_ref, key32_to_f32(k_base + cand))
        ok = c >= float(TOPK)
        return jnp.where(ok, cand, off), jnp.where(ok, c, n_ge)

    off, n_ge = lax.fori_loop(0, FINE_BITS, fine_body,
                              (jnp.zeros((1, QB), jnp.int32), jnp.full((1, QB), TOPK + 1.0, F32)))
    thr = key32_to_f32(k_base + off)
    thr_ref[...] = thr

    @pl.when(jnp.max(n_ge) > float(TOPK))
    def _():
        rows = lax.broadcasted_iota(jnp.int32, (KT, QB), 0)

        def tile_sum(fn):
            def body(jp, cnt):
                for u in range(2):
                    sel = fn(2 * jp + u)
                    for r in range(KT // CNT_ROWS):
                        cnt = cnt + sel[r * CNT_ROWS:(r + 1) * CNT_ROWS]
                return cnt
            cnt = lax.fori_loop(0, npairs, body, jnp.zeros((CNT_ROWS, QB), F32))
            return jnp.sum(cnt, axis=0, keepdims=True)

        n_gt = tile_sum(lambda j: jnp.where(sc_ref[j] > thr, 1.0, 0.0))
        ties_kept = float(TOPK) - n_gt

        def ties_before(p):
            return tile_sum(lambda j: jnp.where((sc_ref[j] == thr) & (rows < p - j * KT), 1.0, 0.0))

        def idx_body(i, p):
            cand = p | jnp.left_shift(jnp.int32(1), SEQ_BITS - 1 - i)
            return jnp.where(ties_before(cand) < ties_kept, cand, p)

        last = lax.fori_loop(0, SEQ_BITS, idx_body, jnp.zeros((1, QB), jnp.int32))

        def drop_body(j, carry):
            x = sc_ref[j]
            sc_ref[j] = jnp.where((x == thr) & (rows > last - j * KT), NEG_INF, x)
            return carry

        lax.fori_loop(0, 2 * npairs, drop_body, 0)

    m_ref[...] = jnp.full(m_ref.shape, NEG_INF, F32)
    l_ref[...] = jnp.zeros_like(l_ref)
    acc_ref[...] = jnp.zeros_like(acc_ref)
    ones_rows = jnp.ones((SUM_ROWS, KT), BF16)

    def att_step(logits=None, values=None):
        if logits is not None:
            j1, kind, slot1 = logits
            s_t = sc_ref[j1]
            mask = jnp.where((s_t >= thr_ref[...]) & (s_t > NEG_INF), 0.0, NEG_INF)
            k0 = pl.multiple_of(j1 * KT, KT)
        if values is not None:
            j2, slot2 = values
        pvs = []

        def update(h):
            a = al_ref[slot2, h]
            l_ref[h] = a * l_ref[h] + pvs[h][B_HEAD_DIM:B_HEAD_DIM + 1, :]
            acc_ref[h] = a * acc_ref[h] + pvs[h][:B_HEAD_DIM, :]

        for h in range(B_HEADS):
            hs = slice(h * B_HEAD_DIM, (h + 1) * B_HEAD_DIM)
            if logits is not None:
                s = jnp.dot(k_ref[pl.ds(k0, KT), hs], q_ref[hs, :], preferred_element_type=F32)
                if kind == "diag":
                    s = s + b0_ref[h]
                elif kind == "prev":
                    lo = jnp.concatenate([s[128:, :128] + b1_ref[h], s[128:, 128:]], axis=1)
                    s = jnp.concatenate([s[:128, :], lo], axis=0)
                s = s + mask
                s_ref[slot1, h] = s
                m_prev = m_ref[h]
                m_next = jnp.maximum(m_prev, jnp.max(s, axis=0, keepdims=True))
                m_safe = jnp.where(m_next == NEG_INF, 0.0, m_next)
                ms_ref[slot1, h] = m_safe
                al_ref[slot1, h] = jnp.exp2(m_prev - m_safe)
                m_ref[h] = m_next
            if values is not None:
                p = jnp.exp2(s_ref[slot2, h] - ms_ref[slot2, h]).astype(BF16)
                pvs.append(jnp.dot(jnp.concatenate([v_ref[j2, hs, :], ones_rows], axis=0), p,
                                   preferred_element_type=F32))
                if h >= UPDATE_LAG:
                    update(h - UPDATE_LAG)
        if values is not None:
            for h in range(B_HEADS - UPDATE_LAG, B_HEADS):
                update(h)

    odd = qb % 2

    @pl.when(qb == 0)
    def _():
        att_step(logits=(0, "diag", 0))

    @pl.when(qb == 1)
    def _():
        att_step(logits=(0, "prev", 1))

    @pl.when(qb >= 2)
    def _():
        @pl.when(odd == 0)
        def _():
            att_step(logits=(0, "far", 0))

        @pl.when(odd == 1)
        def _():
            att_step(logits=(0, "far", 1))
            att_step(logits=(1, "far", 0), values=(0, 1))

        def att_pairs(j, n):
            for u in range(n):
                att_step(logits=(j + 2 * u + 1, "far", 1), values=(j + 2 * u, 0))
                att_step(logits=(j + 2 * u + 2, "far", 0), values=(j + 2 * u + 1, 1))

        def att_body(i, carry):
            att_pairs(odd + 2 * ATT_PAIRS * i, ATT_PAIRS)
            return carry

        def att_rest(i, carry):
            att_pairs(odd + 2 * (npair_far - npair_far % ATT_PAIRS + i), 1)
            return carry

        npair_far = (qb - 2) // 2
        lax.fori_loop(0, npair_far // ATT_PAIRS, att_body, 0)
        lax.fori_loop(0, npair_far % ATT_PAIRS, att_rest, 0)

        att_step(logits=(qb - 1, "prev", 1), values=(qb - 2, 0))

    @pl.when(qb >= 1)
    def _():
        att_step(logits=(qb, "diag", 0), values=(qb - 1, 1))

    att_step(values=(qb, 0))

    for h in range(B_HEADS):
        hs = slice(h * B_HEAD_DIM, (h + 1) * B_HEAD_DIM)
        bg = bg_ref[:, hs]
        o_ref[:, hs] = ((acc_ref[h] / l_ref[h]).T * (bg * jax.nn.sigmoid(bg))).astype(BF16)


def _dsa(rel_bias, q_idx_t, w_idx_t, q_t, p, k_idx, k, v_t):
    single = pl.Buffered(1)
    return pl.pallas_call(
        _dsa_kernel,
        grid=(SEQ // QB,),
        in_specs=[
            pl.BlockSpec(memory_space=pltpu.SMEM),
            pl.BlockSpec((IDX_HEADS, IDX_DIM, QB), lambda i: (0, 0, i)),
            pl.BlockSpec((128, QB), lambda i: (0, i)),
            pl.BlockSpec((B_WIDTH, QB), lambda i: (0, i)),
            pl.BlockSpec((QB, B_WIDTH), lambda i: (i, P_BG // B_WIDTH)),
            pl.BlockSpec((SEQ, IDX_DIM), lambda i: (0, 0), pipeline_mode=single),
            pl.BlockSpec((SEQ, B_WIDTH), lambda i: (0, 0), pipeline_mode=single),
            pl.BlockSpec((NKT, B_WIDTH, KT), lambda i: (0, 0, 0), pipeline_mode=single),
        ],
        out_specs=pl.BlockSpec((QB, B_WIDTH), lambda i: (i, 0)),
        out_shape=jax.ShapeDtypeStruct((SEQ, B_WIDTH), BF16),
        scratch_shapes=[
            pltpu.VMEM((NKT, KT, QB), F32),
            pltpu.VMEM((NKT, KT, QB), BF16),
            pltpu.VMEM((B_HEADS, KT, QB), F32),
            pltpu.VMEM((B_HEADS, 128, 128), F32),
            pltpu.VMEM((1, QB), F32),
            pltpu.VMEM((B_HEADS, 1, QB), F32),
            pltpu.VMEM((B_HEADS, 1, QB), F32),
            pltpu.VMEM((B_HEADS, B_HEAD_DIM, QB), F32),
            pltpu.VMEM((2, B_HEADS, KT, QB), F32),
            pltpu.VMEM((2, B_HEADS, 1, QB), F32),
            pltpu.VMEM((2, B_HEADS, 1, QB), F32),
        ],
        compiler_params=pltpu.CompilerParams(
            dimension_semantics=("arbitrary",), vmem_limit_bytes=VMEM_LIMIT),
        name="dsa",
    )(rel_bias, q_idx_t, w_idx_t, q_t, p, k_idx, k, v_t)


def _out_proj_kernel(ya_ref, yb_ref, ma_ref, mb_ref, x_ref, wpa_ref, wpb_ref, wo_ref, fnw_ref, o_ref):
    pa = jnp.dot(ya_ref[...], wpa_ref[...], preferred_element_type=F32)
    pb = jnp.dot(yb_ref[...], wpb_ref[...], preferred_element_type=F32)
    merged = jax.nn.sigmoid(ma_ref[...]) * pa + jax.nn.sigmoid(mb_ref[...]) * pb
    y = x_ref[...] + jnp.dot(merged.astype(BF16), wo_ref[...], preferred_element_type=F32)
    o_ref[...] = _rms(y, fnw_ref[...])


def _out_proj(y_a, y_b, p, x2, w_pa, w_pb, w_out, final_norm_w):
    tm = 512
    const = lambda i: (0, 0)
    single = pl.Buffered(1)
    return pl.pallas_call(
        _out_proj_kernel,
        grid=(SEQ // tm,),
        in_specs=[
            pl.BlockSpec((tm, A_WIDTH), lambda i: (i, 0)),
            pl.BlockSpec((tm, B_WIDTH), lambda i: (i, 0)),
            pl.BlockSpec((tm, D_MODEL), lambda i: (i, P_MA // D_MODEL)),
            pl.BlockSpec((tm, D_MODEL), lambda i: (i, P_MB // D_MODEL)),
            pl.BlockSpec((tm, D_MODEL), lambda i: (i, 0)),
            pl.BlockSpec((A_WIDTH, D_MODEL), const, pipeline_mode=single),
            pl.BlockSpec((B_WIDTH, D_MODEL), const, pipeline_mode=single),
            pl.BlockSpec((D_MODEL, D_MODEL), const, pipeline_mode=single),
            pl.BlockSpec((1, D_MODEL), const),
        ],
        out_specs=pl.BlockSpec((tm, D_MODEL), lambda i: (i, 0)),
        out_shape=jax.ShapeDtypeStruct((SEQ, D_MODEL), F32),
        compiler_params=pltpu.CompilerParams(
            dimension_semantics=("arbitrary",), vmem_limit_bytes=VMEM_LIMIT),
        name="out_proj",
    )(y_a, y_b, p, p, x2, w_pa, w_pb, w_out, final_norm_w)


def kernel(x, norm_w, w_in, lb_table, gnorm_a, q_norm_w, kv_norm_w, w_uq, w_qidx, w_ukv, kidx_norm_w,
           kidx_norm_b, w_pa, w_pb, w_out, rel_bias, final_norm_w):
    assert x.shape == (1, SEQ, D_MODEL) and w_in.shape[0] == 1
    x2 = x.reshape(SEQ, D_MODEL)
    p = _in_proj(x2, norm_w, w_in[0].astype(BF16))
    w_kv = w_ukv[0].astype(BF16).reshape(KV_RANK, B_HEADS, 2, B_HEAD_DIM)
    w_k = w_kv[:, :, 0, :].reshape(KV_RANK, B_WIDTH)
    w_v_t = w_kv[:, :, 1, :].reshape(KV_RANK, B_WIDTH).T
    q_t, q_idx_t, k, v_t, k_idx, w_idx_t = _proj2(
        p, q_norm_w, kv_norm_w, w_uq[0].astype(BF16).T, w_qidx[0].astype(BF16).T, w_k, w_v_t,
        kidx_norm_w, kidx_norm_b)
    y_a = _hgrn2(p, lb_table, gnorm_a)
    y_b = _dsa(rel_bias, q_idx_t, w_idx_t, q_t, p, k_idx, k, v_t)
    out = _out_proj(y_a, y_b, p, x2, w_pa[0].astype(BF16), w_pb[0].astype(BF16), w_out[0].astype(BF16),
                    final_norm_w.reshape(1, D_MODEL))
    return out.reshape(1, SEQ, D_MODEL)
```
